```python
import math
import jax
import jax.numpy as jnp
from jax import lax
import numpy as np

D_MODEL = 1024
BATCH = 2
SEQ = 8192
DEPTH = 4
DEC_BATCH = 128
DEC_SEQ = 4
PAST_LEN = 2048
PAGE_SIZE = 128

SSM_WIDTH = D_MODEL // 2
SSM_GROUP = 16
SSM_GROUPS = SSM_WIDTH // SSM_GROUP
SSM_STATE = 64
SSM_MAX_RE = -1e-4
HGRN_WIDTH = D_MODEL // 2
HGRN_HEADS = 4
HGRN_DK = HGRN_WIDTH // HGRN_HEADS
HGRN_DV = HGRN_WIDTH // HGRN_HEADS
MOBA_WIDTH = D_MODEL // 2
MOBA_HEADS = 4
MOBA_DH = MOBA_WIDTH // MOBA_HEADS
MOBA_BLOCK = 256
MOBA_TOPK = 3
MOBA_QBLOCK = 128
GLA_WIDTH = D_MODEL // 2
GLA_HEADS = 4
GLA_DK = GLA_WIDTH // (2 * GLA_HEADS)
GLA_DV = GLA_WIDTH // GLA_HEADS
GLA_RANK = 16
GLA_TAU = 16.0
REC_CHUNK = 16
N_BRANCH = 4
N_EXPERTS = 16
N_EXPERT_GROUPS = 4
EXPERTS_PER_GROUP = N_EXPERTS // N_EXPERT_GROUPS
TOP_K = 2
EXPERT_HIDDEN = D_MODEL // 4
DEEPNORM_ALPHA = (2 * DEPTH) ** 0.25
DEEPNORM_BETA = (8 * DEPTH) ** -0.25
NORM_EPS = 1e-5
IN_SIZES = (SSM_WIDTH,
            HGRN_WIDTH, HGRN_WIDTH, HGRN_WIDTH, HGRN_WIDTH,
            MOBA_WIDTH, MOBA_WIDTH, MOBA_WIDTH,
            GLA_HEADS * GLA_DK, GLA_HEADS * GLA_DK, GLA_WIDTH, GLA_WIDTH, GLA_RANK,
            N_BRANCH * D_MODEL)
IN_COLS = sum(IN_SIZES)

kernel_name = 'hybrid_gated_branch_decoder_step'


def layer_norm(x, g, b):
    xf = x.astype(jnp.float32)
    mu = jnp.mean(xf, axis=-1, keepdims=True)
    var = jnp.mean(jnp.square(xf - mu), axis=-1, keepdims=True)
    return ((xf - mu) * lax.rsqrt(var + NORM_EPS) * g + b).astype(x.dtype)


def head_rmsnorm(o, w):
    of = o.astype(jnp.float32)
    return of * lax.rsqrt(jnp.mean(jnp.square(of), axis=-1, keepdims=True) + NORM_EPS) * w


def split_columns(a, sizes):
    offsets = []
    acc = 0
    for s in sizes[:-1]:
        acc += s
        offsets.append(acc)
    return jnp.split(a, offsets, axis=-1)


def gated_linear_recurrence(q, k, v, log_f, s0):
    f32 = jnp.float32
    bsz, t, nh, _ = q.shape
    dv = v.shape[-1]
    c = math.gcd(t, REC_CHUNK)
    n = t // c

    def blocks(a):
        return a.astype(f32).reshape(bsz, n, c, nh, a.shape[-1]).transpose(1, 0, 2, 3, 4)

    causal = jnp.tril(jnp.ones((c, c), dtype=bool))[None, :, :, None, None]

    def step(s, inp):
        qc, kc, vc, fc = inp
        b = jnp.cumsum(fc, axis=1)
        o_inter = jnp.einsum('bthk,bhkv->bthv', qc * jnp.exp(b), s)
        diff = b[:, :, None] - b[:, None, :]
        decay = jnp.exp(jnp.where(causal, diff, -jnp.inf))
        att = jnp.sum(qc[:, :, None] * kc[:, None] * decay, axis=-1)
        o_intra = jnp.einsum('btsh,bshv->bthv', att, vc)
        b_last = b[:, -1]
        k_dec = kc * jnp.exp(b_last[:, None] - b)
        s_new = jnp.exp(b_last)[..., None] * s + jnp.einsum('bshk,bshv->bhkv', k_dec, vc)
        return s_new, o_inter + o_intra

    s_fin, o = lax.scan(step, s0.astype(f32), (blocks(q), blocks(k), blocks(v), blocks(log_f)))
    o = o.transpose(1, 0, 2, 3, 4).reshape(bsz, t, nh, dv)
    return o, s_fin


def complex_affine_combine(e1, e2):
    a1r, a1i, b1r, b1i = e1
    a2r, a2i, b2r, b2i = e2
    return (a2r * a1r - a2i * a1i,
            a2r * a1i + a2i * a1r,
            a2r * b1r - a2i * b1i + b2r,
            a2r * b1i + a2i * b1r + b2i)


def s5_branch(u, p, x0_re, x0_im):
    f32 = jnp.float32
    bsz, t, _ = u.shape
    uf = u.astype(f32).reshape(bsz, t, SSM_GROUPS, SSM_GROUP)
    step = jnp.exp(p['ssm_log_step'].astype(f32))[:, None]
    lr = jnp.minimum(p['ssm_lam_re'].astype(f32), SSM_MAX_RE)
    li = p['ssm_lam_im'].astype(f32)
    mag = jnp.exp(lr * step)
    ab_re = mag * jnp.cos(li * step)
    ab_im = mag * jnp.sin(li * step)
    den = lr * lr + li * li
    coef_re = ((ab_re - 1.0) * lr + ab_im * li) / den
    coef_im = (ab_im * lr - (ab_re - 1.0) * li) / den
    b_re = p['ssm_b_re'].astype(f32)
    b_im = p['ssm_b_im'].astype(f32)
    bb_re = coef_re[..., None] * b_re - coef_im[..., None] * b_im
    bb_im = coef_re[..., None] * b_im + coef_im[..., None] * b_re
    bu_re = jnp.einsum('btgh,gph->btgp', uf, bb_re)
    bu_im = jnp.einsum('btgh,gph->btgp', uf, bb_im)
    x0r = x0_re.astype(f32)
    x0i = x0_im.astype(f32)
    bu_re = bu_re.at[:, 0].add(ab_re * x0r - ab_im * x0i)
    bu_im = bu_im.at[:, 0].add(ab_re * x0i + ab_im * x0r)
    a_re = jnp.broadcast_to(ab_re, bu_re.shape)
    a_im = jnp.broadcast_to(ab_im, bu_im.shape)
    _, _, xr, xi = lax.associative_scan(complex_affine_combine, (a_re, a_im, bu_re, bu_im), axis=1)
    c_re = p['ssm_c_re'].astype(f32)
    c_im = p['ssm_c_im'].astype(f32)
    y = (jnp.einsum('btgp,ghp->btgh', xr, c_re) - jnp.einsum('btgp,ghp->btgh', xi, c_im)
         + p['ssm_d'].astype(f32) * uf)
    y = jax.nn.gelu(y.reshape(bsz, t, SSM_WIDTH))
    za, zb = jnp.split(y @ p['ssm_w_glu'].astype(f32), 2, axis=-1)
    return (za * jax.nn.sigmoid(zb)).astype(u.dtype), xr[:, -1], xi[:, -1]


def hgrn_branch(q_b, f_b, i_b, g_b, p, lb, s0):
    f32 = jnp.float32
    bsz, t, _ = q_b.shape
    shp = (bsz, t, HGRN_HEADS, HGRN_DK)
    q = jax.nn.silu(q_b.astype(f32).reshape(shp))
    z = f_b.astype(f32).reshape(shp)
    lbh = lb.reshape(HGRN_HEADS, HGRN_DK)
    log_f = jnp.logaddexp(jnp.log(lbh), jnp.log1p(-lbh) + jax.nn.log_sigmoid(z))
    k = (1.0 - lbh) * jax.nn.sigmoid(-z)
    v = i_b.reshape(bsz, t, HGRN_HEADS, HGRN_DV)
    o, s = gated_linear_recurrence(q, k, v, log_f, s0)
    g = g_b.astype(f32).reshape(bsz, t, HGRN_HEADS, HGRN_DV)
    o = head_rmsnorm(o, p['hgrn_norm']) * jax.nn.silu(g)
    return o.reshape(bsz, t, HGRN_WIDTH) @ p['hgrn_w_proj'], s


def gla_branch(q_d, k_d, v_d, g_d, r_d, p, s0):
    f32 = jnp.float32
    bsz, t, _ = q_d.shape
    shp = (bsz, t, GLA_HEADS, GLA_DK)
    q = q_d.astype(f32).reshape(shp) * (GLA_DK ** -0.5)
    k = k_d.astype(f32).reshape(shp)
    v = v_d.reshape(bsz, t, GLA_HEADS, GLA_DV)
    gk = (r_d @ p['gla_w_gk2'] + p['gla_b_gk']).astype(f32).reshape(shp)
    log_f = jax.nn.log_sigmoid(gk) / GLA_TAU
    o, s = gated_linear_recurrence(q, k, v, log_f, s0)
    g = g_d.astype(f32).reshape(bsz, t, GLA_HEADS, GLA_DV)
    o = head_rmsnorm(o, p['gla_norm']) * jax.nn.silu(g)
    return o.reshape(bsz, t, GLA_WIDTH) @ p['gla_w_proj'], s


def moba_sequence(q, k, v, pos0):
    f32 = jnp.float32
    tq, nh, dh = q.shape
    tk = k.shape[0]
    nb = max(-(-tk // MOBA_BLOCK), MOBA_TOPK)
    pad = nb * MOBA_BLOCK - tk
    kb = jnp.pad(k.astype(f32), ((0, pad), (0, 0), (0, 0))).reshape(nb, MOBA_BLOCK, nh, dh).transpose(2, 0, 1, 3)
    vb = jnp.pad(v.astype(f32), ((0, pad), (0, 0), (0, 0))).reshape(nb, MOBA_BLOCK, nh, dh).transpose(2, 0, 1, 3)
    kmean = jnp.mean(kb, axis=2)
    qb = math.gcd(tq, MOBA_QBLOCK)
    nq = tq // qb
    q_blocks = q.astype(f32).reshape(nq, qb, nh, dh)
    pos_blocks = (pos0 + jnp.arange(tq, dtype=jnp.int32)).reshape(nq, qb)
    head_idx = jnp.arange(nh)[None, :, None]
    blk_ids = jnp.arange(nb)
    key_offsets = jnp.arange(MOBA_BLOCK)
    scale = dh ** -0.5

    def attend(args):
        qq, pp = args
        own = pp // MOBA_BLOCK
        gscore = jnp.einsum('qhd,hnd->qhn', qq, kmean)
        fully_past = blk_ids[None, None, :] < own[:, None, None]
        gscore = jnp.where(fully_past, gscore, -jnp.inf)
        _, top = lax.top_k(gscore, MOBA_TOPK)
        top_ok = top < own[:, None, None]
        sel = jnp.concatenate([top, jnp.broadcast_to(own[:, None, None], (qb, nh, 1))], axis=-1)
        sel_ok = jnp.concatenate([top_ok, jnp.ones((qb, nh, 1), dtype=bool)], axis=-1)
        kg = kb[head_idx, sel]
        vg = vb[head_idx, sel]
        key_pos = sel[..., None] * MOBA_BLOCK + key_offsets
        ok = sel_ok[..., None] & (key_pos <= pp[:, None, None, None])
        s = jnp.einsum('qhd,qhnkd->qhnk', qq, kg) * scale
        s = jnp.where(ok, s, -jnp.inf)
        w = jax.nn.softmax(s.reshape(qb, nh, -1), axis=-1).reshape(s.shape)
        return jnp.einsum('qhnk,qhnkd->qhd', w, vg)

    o = lax.map(attend, (q_blocks, pos_blocks))
    return o.reshape(tq, nh, dh)


def moba_branch(q_c, k_c, v_c, p, pos0, k_past, v_past):
    bsz, t, _ = q_c.shape
    shp = (bsz, t, MOBA_HEADS, MOBA_DH)
    q = q_c.reshape(shp)
    k = k_c.reshape(shp)
    v = v_c.reshape(shp)
    k_all = jnp.concatenate([k_past.astype(k.dtype), k], axis=1)
    v_all = jnp.concatenate([v_past.astype(v.dtype), v], axis=1)
    o = lax.map(lambda a: moba_sequence(a[0], a[1], a[2], pos0), (q, k_all, v_all))
    return o.reshape(bsz, t, MOBA_WIDTH) @ p['moba_w_proj'], k, v


def token_mixing(h, p, lb, pos0, k_past, v_past, ssm_re0, ssm_im0, hgrn0, gla0):
    bsz, t, _ = h.shape
    proj = h @ p['w_in']
    (u_a, q_b, f_b, i_b, g_b, q_c, k_c, v_c,
     q_d, k_d, v_d, g_d, r_d, gate_logits) = split_columns(proj, IN_SIZES)
    y_a, ssm_re, ssm_im = s5_branch(u_a, p, ssm_re0, ssm_im0)
    y_b, hgrn_s = hgrn_branch(q_b, f_b, i_b, g_b, p, lb, hgrn0)
    y_c, k_new, v_new = moba_branch(q_c, k_c, v_c, p, pos0, k_past, v_past)
    y_d, gla_s = gla_branch(q_d, k_d, v_d, g_d, r_d, p, gla0)
    branches = jnp.stack([y_a, y_b.astype(y_a.dtype), y_c.astype(y_a.dtype), y_d.astype(y_a.dtype)], axis=2)
    gates = jax.nn.sigmoid(gate_logits.reshape(bsz, t, N_BRANCH, D_MODEL))
    merged = jnp.sum(gates * branches, axis=2)
    return merged @ p['w_out'], (k_new, v_new, ssm_re, ssm_im, hgrn_s, gla_s)


def moe(h, p, w_router, b_router):
    f32 = jnp.float32
    bsz, t, _ = h.shape
    scores = jax.nn.sigmoid(h.astype(f32) @ w_router.astype(f32))
    biased = (scores + b_router.astype(f32)).reshape(bsz, t, N_EXPERT_GROUPS, EXPERTS_PER_GROUP)
    group_score = jnp.sum(lax.top_k(biased, TOP_K)[0], axis=-1)
    best = jnp.argmax(group_score, axis=-1)
    in_group = best[..., None] == jnp.arange(N_EXPERT_GROUPS)
    masked = jnp.where(in_group[..., None], biased, -jnp.inf).reshape(bsz, t, N_EXPERTS)
    _, top_idx = lax.top_k(masked, TOP_K)
    top_w = jnp.take_along_axis(scores, top_idx, axis=-1)
    top_w = top_w / jnp.sum(top_w, axis=-1, keepdims=True)
    combine = jnp.sum((top_idx[..., None] == jnp.arange(N_EXPERTS)) * top_w[..., None], axis=-2)
    hid = (jax.nn.silu(jnp.einsum('btd,edh->bteh', h, p['moe_w_gate']))
           * jnp.einsum('btd,edh->bteh', h, p['moe_w_up']))
    hid = hid * combine[..., None].astype(hid.dtype)
    return jnp.einsum('bteh,ehd->btd', hid, p['moe_w_down'])


def trunk_layer(x, c, p, lb, w_router, b_router, pos0, k_past, v_past, ssm_re0, ssm_im0, hgrn0, gla0):
    mod = jax.nn.silu(c) @ p['w_ada'] + p['b_ada']
    shift1, scale1, gate1, shift2, scale2, gate2 = [m[:, None, :] for m in jnp.split(mod, 6, axis=-1)]
    h1 = x * (1.0 + scale1) + shift1
    mix, new_state = token_mixing(h1, p, lb, pos0, k_past, v_past, ssm_re0, ssm_im0, hgrn0, gla0)
    x = layer_norm(DEEPNORM_ALPHA * x + gate1 * mix, p['ln1_g'], p['ln1_b'])
    h2 = x * (1.0 + scale2) + shift2
    x = layer_norm(DEEPNORM_ALPHA * x + gate2 * moe(h2, p, w_router, b_router), p['ln2_g'], p['ln2_b'])
    return x, new_state


def setup_inputs(seed: int = 0) -> dict:
    key = jax.random.key(seed)
    ks = jax.random.split(key, 64)
    it = iter(range(64))
    f32 = jnp.float32

    def nrm(shape, scale):
        return scale * jax.random.normal(ks[next(it)], shape, f32)

    n_pages = PAST_LEN // PAGE_SIZE
    n_pool = (DEC_BATCH * n_pages * 5) // 4
    d = D_MODEL
    x_prompt = nrm((BATCH, SEQ, d), 1.0)
    x_sample = nrm((DEC_BATCH, DEC_SEQ, d), 1.0)
    cache_k = nrm((DEPTH, n_pool, PAGE_SIZE, MOBA_HEADS, MOBA_DH), 1.0)
    cache_v = nrm((DEPTH, n_pool, PAGE_SIZE, MOBA_HEADS, MOBA_DH), 1.0)
    state_ssm_re = nrm((DEPTH, DEC_BATCH, SSM_GROUPS, SSM_STATE), 0.1)
    state_ssm_im = nrm((DEPTH, DEC_BATCH, SSM_GROUPS, SSM_STATE), 0.1)
    state_hgrn = nrm((DEPTH, DEC_BATCH, HGRN_HEADS, HGRN_DK, HGRN_DV), 0.5)
    state_gla = nrm((DEPTH, DEC_BATCH, GLA_HEADS, GLA_DK, GLA_DV), 0.5)
    page_table = jax.random.permutation(ks[next(it)], n_pool)[: DEC_BATCH * n_pages].reshape(DEC_BATCH, n_pages).astype(jnp.int32)
    c_prompt = nrm((BATCH, d), 1.0)
    c_sample = nrm((DEC_BATCH, d), 1.0)
    return {
        'x_prompt': x_prompt, 'x_sample': x_sample,
        'cache_k': cache_k, 'cache_v': cache_v,
        'state_ssm_re': state_ssm_re, 'state_ssm_im': state_ssm_im,
        'state_hgrn': state_hgrn, 'state_gla': state_gla,
        'page_table': page_table, 'c_prompt': c_prompt, 'c_sample': c_sample,
        'ln_in_g': 1.0 + nrm((d,), 0.1), 'ln_in_b': nrm((d,), 0.02),
        'w_ada': nrm((DEPTH, d, 6 * d), 0.5 * d ** -0.5), 'b_ada': nrm((DEPTH, 6 * d), 0.02),
        'w_in': nrm((DEPTH, d, IN_COLS), d ** -0.5),
        'ssm_lam_re': -0.5 + nrm((DEPTH, SSM_GROUPS, SSM_STATE), 0.01),
        'ssm_lam_im': jnp.pi * jnp.arange(SSM_STATE, dtype=f32) + nrm((DEPTH, SSM_GROUPS, SSM_STATE), 0.01),
        'ssm_log_step': jax.random.uniform(ks[next(it)], (DEPTH, SSM_GROUPS), f32, math.log(1e-3), math.log(1e-1)),
        'ssm_b_re': nrm((DEPTH, SSM_GROUPS, SSM_STATE, SSM_GROUP), (2 * SSM_GROUP) ** -0.5),
        'ssm_b_im': nrm((DEPTH, SSM_GROUPS, SSM_STATE, SSM_GROUP), (2 * SSM_GROUP) ** -0.5),
        'ssm_c_re': nrm((DEPTH, SSM_GROUPS, SSM_GROUP, SSM_STATE), SSM_STATE ** -0.5),
        'ssm_c_im': nrm((DEPTH, SSM_GROUPS, SSM_GROUP, SSM_STATE), SSM_STATE ** -0.5),
        'ssm_d': nrm((DEPTH, SSM_GROUPS, SSM_GROUP), 0.5),
        'ssm_w_glu': nrm((DEPTH, SSM_WIDTH, 2 * d), SSM_WIDTH ** -0.5),
        'hgrn_lb': nrm((DEPTH, HGRN_WIDTH), 0.1),
        'hgrn_norm': 1.0 + nrm((DEPTH, HGRN_DV), 0.1),
        'hgrn_w_proj': nrm((DEPTH, HGRN_WIDTH, d), HGRN_WIDTH ** -0.5),
        'moba_w_proj': nrm((DEPTH, MOBA_WIDTH, d), MOBA_WIDTH ** -0.5),
        'gla_w_gk2': nrm((DEPTH, GLA_RANK, GLA_HEADS * GLA_DK), GLA_RANK ** -0.5),
        'gla_b_gk': nrm((DEPTH, GLA_HEADS * GLA_DK), 0.1),
        'gla_norm': 1.0 + nrm((DEPTH, GLA_DV), 0.1),
        'gla_w_proj': nrm((DEPTH, GLA_WIDTH, d), GLA_WIDTH ** -0.5),
        'w_out': nrm((DEPTH, d, d), DEEPNORM_BETA * d ** -0.5),
        'ln1_g': 1.0 + nrm((DEPTH, d), 0.1), 'ln1_b': nrm((DEPTH, d), 0.02),
        'ln2_g': 1.0 + nrm((DEPTH, d), 0.1), 'ln2_b': nrm((DEPTH, d), 0.02),
        'w_router': nrm((d, N_EXPERTS), d ** -0.5), 'b_router': nrm((N_EXPERTS,), 0.01),
        'moe_w_gate': nrm((DEPTH, N_EXPERTS, d, EXPERT_HIDDEN), d ** -0.5),
        'moe_w_up': nrm((DEPTH, N_EXPERTS, d, EXPERT_HIDDEN), d ** -0.5),
        'moe_w_down': nrm((DEPTH, N_EXPERTS, EXPERT_HIDDEN, d), DEEPNORM_BETA * EXPERT_HIDDEN ** -0.5),
    }


def reference(x_prompt, x_sample, cache_k, cache_v, state_ssm_re, state_ssm_im, state_hgrn, state_gla,
              page_table, c_prompt, c_sample, ln_in_g, ln_in_b, w_ada, b_ada, w_in,
              ssm_lam_re, ssm_lam_im, ssm_log_step, ssm_b_re, ssm_b_im, ssm_c_re, ssm_c_im, ssm_d, ssm_w_glu,
              hgrn_lb, hgrn_norm, hgrn_w_proj, moba_w_proj, gla_w_gk2, gla_b_gk, gla_norm, gla_w_proj,
              w_out, ln1_g, ln1_b, ln2_g, ln2_b, w_router, b_router, moe_w_gate, moe_w_up, moe_w_down):
    f32 = jnp.float32
    bp = x_prompt.shape[0]
    ds = x_sample.shape[0]
    n_pages = page_table.shape[1]
    lb_cum = jnp.cumsum(jax.nn.softmax(hgrn_lb.astype(f32), axis=0), axis=0)
    lower_bounds = lb_cum - lb_cum[0:1]
    xp = layer_norm(x_prompt, ln_in_g, ln_in_b)
    xs = layer_norm(x_sample, ln_in_g, ln_in_b)
    dt = x_prompt.dtype
    kp_l, vp_l, ks_l, vs_l = [], [], [], []
    srp_l, sip_l, srs_l, sis_l = [], [], [], []
    hp_l, hs_l, gp_l, gs_l = [], [], [], []
    for l in range(DEPTH):
        p = {
            'w_ada': w_ada[l], 'b_ada': b_ada[l], 'w_in': w_in[l],
            'ssm_lam_re': ssm_lam_re[l], 'ssm_lam_im': ssm_lam_im[l], 'ssm_log_step': ssm_log_step[l],
            'ssm_b_re': ssm_b_re[l], 'ssm_b_im': ssm_b_im[l], 'ssm_c_re': ssm_c_re[l], 'ssm_c_im': ssm_c_im[l],
            'ssm_d': ssm_d[l], 'ssm_w_glu': ssm_w_glu[l],
            'hgrn_norm': hgrn_norm[l], 'hgrn_w_proj': hgrn_w_proj[l], 'moba_w_proj': moba_w_proj[l],
            'gla_w_gk2': gla_w_gk2[l], 'gla_b_gk': gla_b_gk[l], 'gla_norm': gla_norm[l], 'gla_w_proj': gla_w_proj[l],
            'w_out': w_out[l], 'ln1_g': ln1_g[l], 'ln1_b': ln1_b[l], 'ln2_g': ln2_g[l], 'ln2_b': ln2_b[l],
            'moe_w_gate': moe_w_gate[l], 'moe_w_up': moe_w_up[l], 'moe_w_down': moe_w_down[l],
        }
        empty_kv = jnp.zeros((bp, 0, MOBA_HEADS, MOBA_DH), dt)
        xp, st_p = trunk_layer(
            xp, c_prompt, p, lower_bounds[l], w_router, b_router, 0, empty_kv, empty_kv,
            jnp.zeros((bp, SSM_GROUPS, SSM_STATE), dt), jnp.zeros((bp, SSM_GROUPS, SSM_STATE), dt),
            jnp.zeros((bp, HGRN_HEADS, HGRN_DK, HGRN_DV), dt), jnp.zeros((bp, GLA_HEADS, GLA_DK, GLA_DV), dt))
        k_past = cache_k[l][page_table].reshape(ds, n_pages * PAGE_SIZE, MOBA_HEADS, MOBA_DH)
        v_past = cache_v[l][page_table].reshape(ds, n_pages * PAGE_SIZE, MOBA_HEADS, MOBA_DH)
        xs, st_s = trunk_layer(
            xs, c_sample, p, lower_bounds[l], w_router, b_router, PAST_LEN, k_past, v_past,
            state_ssm_re[l], state_ssm_im[l], state_hgrn[l], state_gla[l])
        kp_l.append(st_p[0]); vp_l.append(st_p[1]); srp_l.append(st_p[2]); sip_l.append(st_p[3])
        hp_l.append(st_p[4]); gp_l.append(st_p[5])
        ks_l.append(st_s[0]); vs_l.append(st_s[1]); srs_l.append(st_s[2]); sis_l.append(st_s[3])
        hs_l.append(st_s[4]); gs_l.append(st_s[5])
    y_prompt = xp
    y_sample = xs
    k_prompt = jnp.stack(kp_l)
    v_prompt = jnp.stack(vp_l)
    k_sample = jnp.stack(ks_l)
    v_sample = jnp.stack(vs_l)
    ssm_re_prompt = jnp.stack(srp_l)
    ssm_im_prompt = jnp.stack(sip_l)
    ssm_re_sample = jnp.stack(srs_l)
    ssm_im_sample = jnp.stack(sis_l)
    hgrn_prompt = jnp.stack(hp_l)
    hgrn_sample = jnp.stack(hs_l)
    gla_prompt = jnp.stack(gp_l)
    gla_sample = jnp.stack(gs_l)
    return (y_prompt, y_sample, k_prompt, v_prompt, k_sample, v_sample,
            ssm_re_prompt, ssm_im_prompt, ssm_re_sample, ssm_im_sample,
            hgrn_prompt, hgrn_sample, gla_prompt, gla_sample)
```

```python
import functools
import math

import jax
import jax.numpy as jnp
from jax import lax
from jax.experimental import pallas as pl
from jax.experimental.pallas import tpu as pltpu

F32 = jnp.float32
BF16 = jnp.bfloat16
HIGHEST = lax.Precision.HIGHEST

D_MODEL = 1024
DEPTH = 4
PAGE_SIZE = 128
SSM_WIDTH = 512
SSM_GROUP = 16
SSM_GROUPS = 32
SSM_STATE = 64
SSM_MAX_RE = -1e-4
SSM_CHUNK = 16
SSM_PAIRS = SSM_GROUPS // 2
HGRN_HEADS = 4
HGRN_DK = 128
HGRN_DV = 128
MOBA_HEADS = 4
MOBA_DH = 128
MOBA_WIDTH = 512
MOBA_BLOCK = 256
MOBA_TOPK = 3
GLA_HEADS = 4
GLA_DK = 64
GLA_DV = 128
GLA_RANK = 16
GLA_TAU = 16.0
N_BRANCH = 4
N_EXPERTS = 16
N_EXPERT_GROUPS = 4
EXPERTS_PER_GROUP = 4
EXPERT_HIDDEN = 256
DEEPNORM_ALPHA = (2 * DEPTH) ** 0.25
NORM_EPS = 1e-5

LANES = 128
SUB_CHUNK = 16
REC_CHUNK_PROMPT = 128
VMEM_LIMIT = 56 * 1024 * 1024
NEG = -1e30

COL_GATES = 0
COL_UA = 4096
COL_QB = 4608
COL_FB = 5120
COL_IB = 5632
COL_GB = 6144
COL_QC = 6656
COL_KC = 7168
COL_VC = 7680
COL_QD = 8192
COL_KD = 8448
COL_VD = 8704
COL_GD = 9216
COL_RD = 9728
IN_COLS_PAD = 10240
IN_COLS_REF = 9744
REF_GATE_START = 5648
REF_RD_START = 5632


def _cparams(*sem):
    return pltpu.CompilerParams(dimension_semantics=sem, vmem_limit_bytes=VMEM_LIMIT)


def _silu(x):
    return x * jax.nn.sigmoid(x)


def _log_sigmoid(x):
    return jnp.minimum(x, 0.0) - jnp.log1p(jnp.exp(-jnp.abs(x)))


def _dot(a, b, **kw):
    return jnp.dot(a, b, preferred_element_type=F32, **kw)


def _dot_nt(a, b, **kw):
    return lax.dot_general(a, b, (((1,), (1,)), ((), ())), preferred_element_type=F32, **kw)


def _dot_tn(a, b, **kw):
    return lax.dot_general(a, b, (((0,), (0,)), ((), ())), preferred_element_type=F32, **kw)


def _layer_norm(x, g, b):
    mu = jnp.mean(x, axis=-1, keepdims=True)
    xc = x - mu
    var = jnp.mean(xc * xc, axis=-1, keepdims=True)
    return xc * lax.rsqrt(var + NORM_EPS) * g + b


def _ln_kernel(x_ref, g_ref, b_ref, o_ref):
    o_ref[...] = _layer_norm(x_ref[...], g_ref[...], b_ref[...])


def _ln_rows(x, g, b, tm):
    n, d = x.shape
    return pl.pallas_call(
        _ln_kernel,
        grid=(n // tm,),
        in_specs=[pl.BlockSpec((tm, d), lambda i: (i, 0)),
                  pl.BlockSpec((1, d), lambda i: (0, 0)),
                  pl.BlockSpec((1, d), lambda i: (0, 0))],
        out_specs=pl.BlockSpec((tm, d), lambda i: (i, 0)),
        out_shape=jax.ShapeDtypeStruct((n, d), F32),
        compiler_params=_cparams("parallel"),
    )(x, g.reshape(1, d), b.reshape(1, d))


def _ada_kernel(c_ref, w_ref, b_ref, o_ref):
    h = _silu(c_ref[...]).astype(BF16)
    o_ref[...] = _dot(h, w_ref[...].astype(BF16)) + b_ref[...]


def _ada_all(c, w_ada, b_ada, tn=1536):
    nb, d = c.shape
    depth, _, n6 = w_ada.shape
    return pl.pallas_call(
        _ada_kernel,
        grid=(depth, n6 // tn),
        in_specs=[pl.BlockSpec((nb, d), lambda l, j: (0, 0)),
                  pl.BlockSpec((None, d, tn), lambda l, j: (l, 0, j)),
                  pl.BlockSpec((None, 1, tn), lambda l, j: (l, 0, j))],
        out_specs=pl.BlockSpec((None, nb, tn), lambda l, j: (l, 0, j)),
        out_shape=jax.ShapeDtypeStruct((depth, nb, n6), F32),
        compiler_params=_cparams("parallel", "parallel"),
    )(c, w_ada, b_ada.reshape(depth, 1, n6))


def _mod_spec(mod, tm, rows_per_seq, chunk):
    if mod.ndim == 3:
        tiles_per_seq = rows_per_seq // tm
        return pl.BlockSpec((None, 1, D_MODEL), lambda i, *_: (i // tiles_per_seq, 0, chunk))
    return pl.BlockSpec((tm, D_MODEL), lambda i, *_: (i, chunk))


def _inproj_kernel(x_ref, sh_ref, sc_ref, w_ref, o_ref, h_ref):
    @pl.when(pl.program_id(1) == 0)
    def _():
        h_ref[...] = (x_ref[...] * (1.0 + sc_ref[...]) + sh_ref[...]).astype(BF16)

    o_ref[...] = _dot(h_ref[...], w_ref[...])


def _in_proj(x, mod, rows_per_seq, w, tm, tn=1024):
    n, d = x.shape
    ncol = w.shape[1]
    return pl.pallas_call(
        _inproj_kernel,
        grid=(n // tm, ncol // tn),
        in_specs=[pl.BlockSpec((tm, d), lambda i, j: (i, 0)),
                  _mod_spec(mod, tm, rows_per_seq, 0),
                  _mod_spec(mod, tm, rows_per_seq, 1),
                  pl.BlockSpec((d, tn), lambda i, j: (0, j))],
        out_specs=pl.BlockSpec((tm, tn), lambda i, j: (i, j)),
        out_shape=jax.ShapeDtypeStruct((n, ncol), F32),
        scratch_shapes=[pltpu.VMEM((tm, d), BF16)],
        compiler_params=_cparams("parallel", "arbitrary"),
    )(x, mod, mod, w)


def _ssm_prompt_kernel(u_ref, toep_ref, pre_ref, pim_ref, qre_ref, qim_ref, d_ref, are_ref, aim_ref,
                       y_ref, sre_ref, sim_ref, *, n_steps):
    u = u_ref[...]
    ub = u.astype(BF16)
    xr = _dot(ub, pre_ref[...])
    xi = _dot(ub, pim_ref[...])
    nrow = u.shape[0]
    row = lax.broadcasted_iota(jnp.int32, xr.shape, 0)
    for k in range(n_steps):
        dist = 1 << k
        ar = are_ref[k:k + 1, :]
        ai = aim_ref[k:k + 1, :]
        keep = row >= dist
        sr = jnp.where(keep, pltpu.roll(xr, dist, 0), 0.0)
        si = jnp.where(keep, pltpu.roll(xi, dist, 0), 0.0)
        xr, xi = xr + (ar * sr - ai * si), xi + (ar * si + ai * sr)
    keep = row >= 1
    pr = jnp.where(keep, pltpu.roll(xr, 1, 0), 0.0)
    pi = jnp.where(keep, pltpu.roll(xi, 1, 0), 0.0)
    y = (_dot(ub, toep_ref[...]) + _dot(pr.astype(BF16), qre_ref[...]) + _dot(pi.astype(BF16), qim_ref[...])
         + d_ref[...] * u)
    y_ref[...] = jax.nn.gelu(y)
    sre_ref[...] = xr[nrow - 1:nrow, :]
    sim_ref[...] = xi[nrow - 1:nrow, :]


def _ssm_prompt(u, sp):
    bsz, t, _ = u.shape
    nrow = t // SSM_CHUNK
    n_steps = max(nrow - 1, 0).bit_length()
    cw = 2 * SSM_CHUNK * SSM_GROUP
    u2 = (u.reshape(bsz, nrow, SSM_CHUNK, SSM_PAIRS, 2, SSM_GROUP)
          .transpose(0, 3, 1, 4, 2, 5).reshape(bsz, SSM_PAIRS, nrow, cw))
    y2, sre, sim = pl.pallas_call(
        functools.partial(_ssm_prompt_kernel, n_steps=n_steps),
        grid=(bsz, SSM_PAIRS),
        in_specs=[pl.BlockSpec((None, None, nrow, cw), lambda b, g: (b, g, 0, 0)),
                  pl.BlockSpec((None, cw, cw), lambda b, g: (g, 0, 0)),
                  pl.BlockSpec((None, cw, LANES), lambda b, g: (g, 0, 0)),
                  pl.BlockSpec((None, cw, LANES), lambda b, g: (g, 0, 0)),
                  pl.BlockSpec((None, LANES, cw), lambda b, g: (g, 0, 0)),
                  pl.BlockSpec((None, LANES, cw), lambda b, g: (g, 0, 0)),
                  pl.BlockSpec((None, 1, cw), lambda b, g: (g, 0, 0)),
                  pl.BlockSpec((None, sp["scan_re"].shape[1], LANES), lambda b, g: (g, 0, 0)),
                  pl.BlockSpec((None, sp["scan_im"].shape[1], LANES), lambda b, g: (g, 0, 0))],
        out_specs=[pl.BlockSpec((None, None, nrow, cw), lambda b, g: (b, g, 0, 0)),
                   pl.BlockSpec((None, None, 1, LANES), lambda b, g: (b, g, 0, 0)),
                   pl.BlockSpec((None, None, 1, LANES), lambda b, g: (b, g, 0, 0))],
        out_shape=[jax.ShapeDtypeStruct((bsz, SSM_PAIRS, nrow, cw), F32),
                   jax.ShapeDtypeStruct((bsz, SSM_PAIRS, 1, LANES), F32),
                   jax.ShapeDtypeStruct((bsz, SSM_PAIRS, 1, LANES), F32)],
        compiler_params=_cparams("parallel", "parallel"),
    )(u2, sp["toep"], sp["p_re"], sp["p_im"], sp["q_re"], sp["q_im"], sp["d_chunk"], sp["scan_re"], sp["scan_im"])
    y = (y2.reshape(bsz, SSM_PAIRS, nrow, 2, SSM_CHUNK, SSM_GROUP)
         .transpose(0, 2, 4, 1, 3, 5).reshape(bsz, t, SSM_WIDTH))
    return y, sre.reshape(bsz, SSM_GROUPS, SSM_STATE), sim.reshape(bsz, SSM_GROUPS, SSM_STATE)


def _ssm_sample_kernel(u_ref, x0r_ref, x0i_ref, bbr_ref, bbi_ref, cr_ref, ci_ref, ar_ref, ai_ref, d_ref,
                       y_ref, xr_ref, xi_ref):
    xr = x0r_ref[...]
    xi = x0i_ref[...]
    ar = ar_ref[...]
    ai = ai_ref[...]
    for t in range(u_ref.shape[0]):
        u = u_ref[t]
        ub = u.astype(BF16)
        xr, xi = (ar * xr - ai * xi + _dot(ub, bbr_ref[...]),
                  ar * xi + ai * xr + _dot(ub, bbi_ref[...]))
        y = _dot(xr.astype(BF16), cr_ref[...]) - _dot(xi.astype(BF16), ci_ref[...]) + d_ref[...] * u
        y_ref[t] = jax.nn.gelu(y)
    xr_ref[...] = xr
    xi_ref[...] = xi


def _ssm_sample(u_tm, x0r, x0i, sp):
    t, bsz, w = u_tm.shape
    ns = SSM_GROUPS * SSM_STATE
    full = lambda *shape: pl.BlockSpec(shape, lambda i: (0,) * len(shape))
    return pl.pallas_call(
        _ssm_sample_kernel,
        grid=(1,),
        in_specs=[full(t, bsz, w), full(bsz, ns), full(bsz, ns), full(w, ns), full(w, ns), full(ns, w), full(ns, w),
                  full(1, ns), full(1, ns), full(1, w)],
        out_specs=[full(t, bsz, w), full(bsz, ns), full(bsz, ns)],
        out_shape=[jax.ShapeDtypeStruct((t, bsz, w), F32), jax.ShapeDtypeStruct((bsz, ns), F32),
                   jax.ShapeDtypeStruct((bsz, ns), F32)],
        compiler_params=_cparams("arbitrary"),
    )(u_tm, x0r, x0i, sp["bb_re"], sp["bb_im"], sp["c_re"], sp["c_im"], sp["a_re"], sp["a_im"], sp["d_row"])


def _ssm_tables(lam_re, lam_im, log_step, b_re, b_im, c_re, c_im, d, n_scan):
    hp = dict(precision=HIGHEST)
    g, p, hch = SSM_GROUPS, SSM_STATE, SSM_GROUP
    step = jnp.exp(log_step)[:, None]
    lr = jnp.minimum(lam_re, SSM_MAX_RE)
    li = lam_im
    mag = jnp.exp(lr * step)
    ab_re = mag * jnp.cos(li * step)
    ab_im = mag * jnp.sin(li * step)
    den = lr * lr + li * li
    coef_re = ((ab_re - 1.0) * lr + ab_im * li) / den
    coef_im = (ab_im * lr - (ab_re - 1.0) * li) / den
    bb_re = coef_re[..., None] * b_re - coef_im[..., None] * b_im
    bb_im = coef_re[..., None] * b_im + coef_im[..., None] * b_re

    def cmul(xr, xi, yr, yi):
        return xr * yr - xi * yi, xr * yi + xi * yr

    pows_re, pows_im = [jnp.ones_like(ab_re)], [jnp.zeros_like(ab_im)]
    for _ in range(SSM_CHUNK):
        nr, ni = cmul(pows_re[-1], pows_im[-1], ab_re, ab_im)
        pows_re.append(nr)
        pows_im.append(ni)
    pw_re = jnp.stack(pows_re)
    pw_im = jnp.stack(pows_im)
    ajb_re = pw_re[:SSM_CHUNK, :, :, None] * bb_re - pw_im[:SSM_CHUNK, :, :, None] * bb_im
    ajb_im = pw_re[:SSM_CHUNK, :, :, None] * bb_im + pw_im[:SSM_CHUNK, :, :, None] * bb_re
    klag = (jnp.einsum("jgpi,gop->jgio", ajb_re, c_re, **hp) - jnp.einsum("jgpi,gop->jgio", ajb_im, c_im, **hp))
    ts = jnp.arange(SSM_CHUNK)
    lag = ts[None, :] - ts[:, None]
    toep = jnp.where((lag >= 0)[:, :, None, None, None], klag[jnp.clip(lag, 0, SSM_CHUNK - 1)], 0.0)
    toep = toep.transpose(2, 0, 3, 1, 4).reshape(g, SSM_CHUNK * hch, SSM_CHUNK * hch)
    rev_re = pw_re[:SSM_CHUNK][::-1]
    rev_im = pw_im[:SSM_CHUNK][::-1]
    p_re = (rev_re[..., None] * bb_re - rev_im[..., None] * bb_im).transpose(1, 0, 3, 2).reshape(g, SSM_CHUNK * hch, p)
    p_im = (rev_re[..., None] * bb_im + rev_im[..., None] * bb_re).transpose(1, 0, 3, 2).reshape(g, SSM_CHUNK * hch, p)
    a1_re = pw_re[1:]
    a1_im = pw_im[1:]
    cr_t = c_re.transpose(0, 2, 1)
    ci_t = c_im.transpose(0, 2, 1)
    q_re = (cr_t[None] * a1_re[..., None] - ci_t[None] * a1_im[..., None]).transpose(1, 2, 0, 3).reshape(g, p, SSM_CHUNK * hch)
    q_im = (-(cr_t[None] * a1_im[..., None] + ci_t[None] * a1_re[..., None])).transpose(1, 2, 0, 3).reshape(g, p, SSM_CHUNK * hch)
    d_chunk = jnp.broadcast_to(d[:, None, :], (g, SSM_CHUNK, hch)).reshape(g, 1, SSM_CHUNK * hch)
    sc_re, sc_im = [pw_re[SSM_CHUNK]], [pw_im[SSM_CHUNK]]
    for _ in range(max(n_scan, 1) - 1):
        nr, ni = cmul(sc_re[-1], sc_im[-1], sc_re[-1], sc_im[-1])
        sc_re.append(nr)
        sc_im.append(ni)
    scan_re = jnp.stack(sc_re, axis=1)
    scan_im = jnp.stack(sc_im, axis=1)

    def pair_diag(m):
        m2 = m.reshape(SSM_PAIRS, 2, m.shape[1], m.shape[2])
        z = jnp.zeros_like(m2[:, 0])
        top = jnp.concatenate([m2[:, 0], z], axis=2)
        bot = jnp.concatenate([z, m2[:, 1]], axis=2)
        return jnp.concatenate([top, bot], axis=1)

    def pair_cat(m):
        m2 = m.reshape(SSM_PAIRS, 2, m.shape[1], m.shape[2])
        return jnp.concatenate([m2[:, 0], m2[:, 1]], axis=2)

    def dense_diag(blocks):
        eye = jnp.eye(g, dtype=blocks.dtype)
        return (blocks[:, :, None, :] * eye[:, None, :, None]).reshape(g * blocks.shape[1], g * blocks.shape[2])

    return dict(
        toep=pair_diag(toep).astype(BF16), p_re=pair_diag(p_re).astype(BF16), p_im=pair_diag(p_im).astype(BF16),
        q_re=pair_diag(q_re).astype(BF16), q_im=pair_diag(q_im).astype(BF16), d_chunk=pair_cat(d_chunk),
        scan_re=pair_cat(scan_re), scan_im=pair_cat(scan_im),
        bb_re=dense_diag(bb_re.transpose(0, 2, 1)).astype(BF16), bb_im=dense_diag(bb_im.transpose(0, 2, 1)).astype(BF16),
        c_re=dense_diag(cr_t).astype(BF16), c_im=dense_diag(ci_t).astype(BF16),
        a_re=ab_re.reshape(1, g * p), a_im=ab_im.reshape(1, g * p), d_row=d.reshape(1, g * hch))


def _linrec_kernel(*refs, mode, chunk, rows, heads, dk, dv, has_s0):
    it = iter(refs)
    q_ref, k_ref, v_ref, g_ref = next(it), next(it), next(it), next(it)
    if mode == "hgrn":
        loglb_ref, log1mlb_ref, onemlb_ref = next(it), next(it), next(it)
    else:
        r_ref, wgk_ref, bgk_ref = next(it), next(it), next(it)
    normw_ref = next(it)
    s0_ref = next(it) if has_s0 else None
    o_ref, sfin_ref = next(it), next(it)
    st_ref, qs_ref, ks_ref, bs_ref, ad_ref = next(it), next(it), next(it), next(it), next(it)
    pad_refs = [next(it) for _ in range(5)] if rows != chunk else None

    c = pl.program_id(1)

    @pl.when(c == 0)
    def _():
        for h in range(heads):
            if has_s0:
                st_ref[h] = s0_ref[h].T
            else:
                st_ref[h] = jnp.zeros((dv, dk), F32)

    def load(ref, slot):
        if pad_refs is None:
            return ref[...]
        buf = pad_refs[slot]
        buf[...] = jnp.zeros(buf.shape, F32)
        buf[0:rows, 0:ref.shape[-1]] = ref[...]
        return buf[:, 0:ref.shape[-1]]

    q_all, k_all, v_all, g_all = load(q_ref, 0), load(k_ref, 1), load(v_ref, 2), load(g_ref, 3)
    trow = lax.broadcasted_iota(jnp.int32, (chunk, chunk), 0)
    tcol = lax.broadcasted_iota(jnp.int32, (chunk, chunk), 1)
    tri = jnp.where(trow >= tcol, 1.0, 0.0).astype(F32)
    dxor = trow ^ tcol
    rowi = lax.broadcasted_iota(jnp.int32, (chunk, 1), 0)
    valid = rowi < rows
    if mode == "gla":
        r_all = load(r_ref, 4)
        gk_all = _dot(r_all.astype(BF16), wgk_ref[...]) + bgk_ref[...]
    levels = [h for h in (16, 32, 64) if h < chunk]
    n_sub = chunk // SUB_CHUNK
    lane_sub = lax.broadcasted_iota(jnp.int32, (SUB_CHUNK, chunk), 1)

    for h in range(heads):
        ksl = slice(h * dk, (h + 1) * dk)
        vsl = slice(h * dv, (h + 1) * dv)
        if mode == "hgrn":
            q = _silu(q_all[:, ksl])
            z = k_all[:, ksl]
            la = loglb_ref[:, ksl]
            lc = log1mlb_ref[:, ksl] + _log_sigmoid(z)
            lf = jnp.maximum(la, lc) + jnp.log1p(jnp.exp(-jnp.abs(la - lc)))
            k = onemlb_ref[:, ksl] * jax.nn.sigmoid(-z)
        else:
            q = q_all[:, ksl] * (dk ** -0.5)
            k = k_all[:, ksl]
            lf = _log_sigmoid(gk_all[:, ksl]) / GLA_TAU
        v = v_all[:, vsl]
        if rows != chunk:
            lf = jnp.where(valid, lf, 0.0)
        b = _dot(tri, lf, precision=HIGHEST)
        st = st_ref[h]
        o = _dot_nt((q * jnp.exp(b)).astype(BF16), st.astype(BF16))
        b_last = b[chunk - 1:chunk, :]
        k_dec = k * jnp.exp(b_last - b)
        st_new = st * jnp.exp(b_last) + _dot_tn(v.astype(BF16), k_dec.astype(BF16))
        st_ref[h] = st_new

        qs_ref[h] = q
        ks_ref[h] = k
        bs_ref[h] = b

        def sub_block(j, carry, h=h):
            r0 = j * SUB_CHUNK
            if not isinstance(j, int):
                r0 = pl.multiple_of(r0, SUB_CHUNK)
            qj = qs_ref[h, pl.ds(r0, SUB_CHUNK), :]
            kj = ks_ref[h, pl.ds(r0, SUB_CHUNK), :]
            bj = bs_ref[h, pl.ds(r0, SUB_CHUNK), :]
            blk = jnp.zeros((SUB_CHUNK, chunk), F32)
            for s in range(SUB_CHUNK):
                x = qj * kj[s:s + 1, :] * jnp.exp(jnp.minimum(bj - bj[s:s + 1, :], 0.0))
                a = jnp.sum(x, axis=-1, keepdims=True)
                blk = jnp.where(lane_sub == r0 + s, a, blk)
            ad_ref[h, pl.ds(r0, SUB_CHUNK), :] = blk
            return carry

        if n_sub == 1:
            sub_block(0, 0)
        else:
            lax.fori_loop(0, n_sub, sub_block, 0)
        att = ad_ref[h]
        for half in levels:
            nblk = chunk // (2 * half)
            b3 = b.reshape(nblk, 2 * half, dk)
            beta = jnp.broadcast_to(b3[:, half - 1:half, :], (nblk, 2 * half, dk)).reshape(chunk, dk)
            e = jnp.exp(-jnp.abs(b - beta))
            upper = (rowi & half) != 0
            qh = jnp.where(upper, q * e, 0.0)
            kh = jnp.where(upper, 0.0, k * e)
            att = jnp.where(dxor >= half, _dot_nt(qh.astype(BF16), kh.astype(BF16)), att)
        att = jnp.where(trow >= tcol, att, 0.0)
        o = o + _dot(att.astype(BF16), v.astype(BF16))
        o = o * lax.rsqrt(jnp.mean(o * o, axis=-1, keepdims=True) + NORM_EPS) * normw_ref[...]
        o = o * _silu(g_all[:, vsl])
        o_ref[:, vsl] = o[0:rows, :]

    @pl.when(c == pl.num_programs(1) - 1)
    def _():
        for h in range(heads):
            sfin_ref[h] = st_ref[h].T


def _lin_rec(mode, proj3, cols, extras, normw, s0, layer, chunk):
    bsz, t, _ = proj3.shape
    heads, dk, dv = (HGRN_HEADS, HGRN_DK, HGRN_DV) if mode == "hgrn" else (GLA_HEADS, GLA_DK, GLA_DV)
    rows = min(chunk, t)
    nchunk = max(t // chunk, 1)
    wk, wv = heads * dk, heads * dv

    def col_spec(width, col):
        return pl.BlockSpec((None, rows, width), lambda b, c: (b, c, col // width))

    in_specs = [col_spec(wk, cols[0]), col_spec(wk, cols[1]), col_spec(wv, cols[2]), col_spec(wv, cols[3])]
    args = [proj3, proj3, proj3, proj3]
    if mode == "hgrn":
        in_specs += [pl.BlockSpec((1, wk), lambda b, c: (0, 0))] * 3
        args += list(extras)
    else:
        in_specs += [col_spec(LANES, COL_RD), pl.BlockSpec((LANES, wk), lambda b, c: (0, 0)),
                     pl.BlockSpec((1, wk), lambda b, c: (0, 0))]
        args += [proj3] + list(extras)
    in_specs.append(pl.BlockSpec((1, dv), lambda b, c: (0, 0)))
    args.append(normw.reshape(1, dv))
    has_s0 = s0 is not None
    if has_s0:
        in_specs.append(pl.BlockSpec((None, None, heads, dk, dv), lambda b, c: (layer, b, 0, 0, 0)))
        args.append(s0)
    scratch = [pltpu.VMEM((heads, dv, dk), F32), pltpu.VMEM((heads, chunk, dk), F32),
               pltpu.VMEM((heads, chunk, dk), F32), pltpu.VMEM((heads, chunk, dk), F32),
               pltpu.VMEM((heads, chunk, chunk), F32)]
    if rows != chunk:
        scratch += [pltpu.VMEM((chunk, wk), F32), pltpu.VMEM((chunk, wk), F32), pltpu.VMEM((chunk, wv), F32),
                    pltpu.VMEM((chunk, wv), F32), pltpu.VMEM((chunk, LANES), F32)]
    return pl.pallas_call(
        functools.partial(_linrec_kernel, mode=mode, chunk=chunk, rows=rows, heads=heads, dk=dk, dv=dv,
                          has_s0=has_s0),
        grid=(bsz, nchunk),
        in_specs=in_specs,
        out_specs=[pl.BlockSpec((None, rows, wv), lambda b, c: (b, c, 0)),
                   pl.BlockSpec((None, heads, dk, dv), lambda b, c: (b, 0, 0, 0))],
        out_shape=[jax.ShapeDtypeStruct((bsz, t, wv), F32), jax.ShapeDtypeStruct((bsz, heads, dk, dv), F32)],
        scratch_shapes=scratch,
        compiler_params=_cparams("parallel", "arbitrary"),
    )(*args)


def _kmean_kernel(k_ref, o_ref):
    o_ref[...] = jnp.mean(k_ref[...], axis=0, keepdims=True)


def _block_key_means(proj3):
    bsz, t, _ = proj3.shape
    nb = t // MOBA_BLOCK
    out = pl.pallas_call(
        _kmean_kernel,
        grid=(bsz, nb),
        in_specs=[pl.BlockSpec((None, MOBA_BLOCK, MOBA_WIDTH), lambda b, n: (b, n, COL_KC // MOBA_WIDTH))],
        out_specs=pl.BlockSpec((None, None, 1, MOBA_WIDTH), lambda b, n: (b, n, 0, 0)),
        out_shape=jax.ShapeDtypeStruct((bsz, nb, 1, MOBA_WIDTH), F32),
        compiler_params=_cparams("parallel", "parallel"),
    )(proj3)
    return out.reshape(bsz, nb, MOBA_WIDTH)


def _select_top_blocks(gscore, n_valid, lane):
    g = jnp.where(lane < n_valid, gscore, -jnp.inf)
    sel = jnp.zeros(gscore.shape, F32)
    width = gscore.shape[-1]
    for _ in range(MOBA_TOPK):
        m = jnp.max(g, axis=-1, keepdims=True)
        first = jnp.min(jnp.where(g == m, lane, width), axis=-1, keepdims=True)
        pick = lane == jnp.where(m > -jnp.inf, first, -1)
        sel = jnp.where(pick, 1.0, sel)
        g = jnp.where(pick, -jnp.inf, g)
    return sel


def _moba_prompt_kernel(q_ref, k_ref, v_ref, km_ref, o_ref, kb_ref, vb_ref, sel_ref):
    i = pl.program_id(2)
    blk = MOBA_BLOCK

    @pl.when(i == 0)
    def _():
        kb_ref[...] = k_ref[...].astype(BF16)
        vb_ref[...] = v_ref[...].astype(BF16)

    q = q_ref[...]
    lane = lax.broadcasted_iota(jnp.int32, (blk, LANES), 1)
    gscore = _dot_nt(q, km_ref[...], precision=HIGHEST)
    sel_ref[...] = _select_top_blocks(gscore, i, lane)
    qb = (q * (MOBA_DH ** -0.5)).astype(BF16)

    def past_block(n, carry):
        m, l, acc = carry
        r0 = pl.multiple_of(n * blk, blk)
        s = _dot_nt(qb, kb_ref[pl.ds(r0, blk), :])
        chosen = jnp.max(jnp.where(lane == n, sel_ref[...], 0.0), axis=-1, keepdims=True) > 0.5
        s = jnp.where(chosen, s, NEG)
        m_new = jnp.maximum(m, jnp.max(s, axis=-1, keepdims=True))
        alpha = jnp.exp(m - m_new)
        p = jnp.exp(s - m_new)
        l = alpha * l + jnp.sum(p, axis=-1, keepdims=True)
        acc = alpha * acc + _dot(p.astype(BF16), vb_ref[pl.ds(r0, blk), :])
        return m_new, l, acc

    init = (jnp.full((blk, 1), NEG, F32), jnp.zeros((blk, 1), F32), jnp.zeros((blk, MOBA_DH), F32))
    m, l, acc = lax.fori_loop(0, i, past_block, init)
    r0 = pl.multiple_of(i * blk, blk)
    s = _dot_nt(qb, kb_ref[pl.ds(r0, blk), :])
    qpos = lax.broadcasted_iota(jnp.int32, (blk, blk), 0)
    kpos = lax.broadcasted_iota(jnp.int32, (blk, blk), 1)
    s = jnp.where(kpos <= qpos, s, NEG)
    m_new = jnp.maximum(m, jnp.max(s, axis=-1, keepdims=True))
    alpha = jnp.exp(m - m_new)
    p = jnp.exp(s - m_new)
    l = alpha * l + jnp.sum(p, axis=-1, keepdims=True)
    acc = alpha * acc + _dot(p.astype(BF16), vb_ref[pl.ds(r0, blk), :])
    o_ref[...] = acc / l


def _moba_prompt(proj3):
    bsz, t, _ = proj3.shape
    nb = t // MOBA_BLOCK
    km = _block_key_means(proj3)
    km = jnp.pad(km, ((0, 0), (0, LANES - nb), (0, 0)))
    hq, hk, hv = COL_QC // MOBA_DH, COL_KC // MOBA_DH, COL_VC // MOBA_DH
    return pl.pallas_call(
        _moba_prompt_kernel,
        grid=(bsz, MOBA_HEADS, nb),
        in_specs=[pl.BlockSpec((None, MOBA_BLOCK, MOBA_DH), lambda b, h, i: (b, i, hq + h)),
                  pl.BlockSpec((None, t, MOBA_DH), lambda b, h, i: (b, 0, hk + h)),
                  pl.BlockSpec((None, t, MOBA_DH), lambda b, h, i: (b, 0, hv + h)),
                  pl.BlockSpec((None, LANES, MOBA_DH), lambda b, h, i: (b, 0, h))],
        out_specs=pl.BlockSpec((None, MOBA_BLOCK, MOBA_DH), lambda b, h, i: (b, i, h)),
        out_shape=jax.ShapeDtypeStruct((bsz, t, MOBA_WIDTH), F32),
        scratch_shapes=[pltpu.VMEM((t, MOBA_DH), BF16), pltpu.VMEM((t, MOBA_DH), BF16),
                        pltpu.VMEM((MOBA_BLOCK, LANES), F32)],
        compiler_params=_cparams("parallel", "parallel", "arbitrary"),
    )(proj3, proj3, proj3, km)


def _moba_sample_kernel(pt_ref, q_ref, kn_ref, vn_ref, *refs, n_pages, t_new):
    k_pages = refs[:n_pages]
    v_pages = refs[n_pages:2 * n_pages]
    o_ref = refs[2 * n_pages]
    s_ref, km_ref = refs[2 * n_pages + 1], refs[2 * n_pages + 2]
    nq = q_ref.shape[0]
    pages_per_block = MOBA_BLOCK // PAGE_SIZE
    n_blocks = n_pages // pages_per_block
    q = q_ref[...]
    for n in range(n_blocks):
        tot = jnp.sum(k_pages[n * pages_per_block][...], axis=0, keepdims=True)
        for j in range(1, pages_per_block):
            tot = tot + jnp.sum(k_pages[n * pages_per_block + j][...], axis=0, keepdims=True)
        km_ref[n:n + 1, :] = tot * (1.0 / MOBA_BLOCK)
    gscore = _dot_nt(q, km_ref[...], precision=HIGHEST)
    lane_b = lax.broadcasted_iota(jnp.int32, gscore.shape, 1)
    sel = _select_top_blocks(gscore, n_blocks, lane_b)
    qb = (q * (MOBA_DH ** -0.5)).astype(BF16)
    for j in range(n_pages):
        s = _dot_nt(qb, k_pages[j][...].astype(BF16))
        nblk = j // pages_per_block
        s_ref[:, j * PAGE_SIZE:(j + 1) * PAGE_SIZE] = jnp.where(sel[:, nblk:nblk + 1] > 0.5, s, NEG)
    s_new = _dot_nt(qb, kn_ref[...].astype(BF16))
    rown = lax.broadcasted_iota(jnp.int32, s_new.shape, 0)
    coln = lax.broadcasted_iota(jnp.int32, s_new.shape, 1)
    s_new = jnp.where(coln <= rown % t_new, s_new, NEG)
    s_past = s_ref[...]
    m = jnp.maximum(jnp.max(s_past, axis=-1, keepdims=True), jnp.max(s_new, axis=-1, keepdims=True))
    p_past = jnp.exp(s_past - m)
    p_new = jnp.exp(s_new - m)
    l = jnp.sum(p_past, axis=-1, keepdims=True) + jnp.sum(p_new, axis=-1, keepdims=True)
    o = _dot(p_new.astype(BF16), vn_ref[...].astype(BF16))
    for j in range(n_pages):
        o = o + _dot(p_past[:, j * PAGE_SIZE:(j + 1) * PAGE_SIZE].astype(BF16), v_pages[j][...].astype(BF16))
    o = o / l
    for h in range(MOBA_HEADS):
        o_ref[:, h * MOBA_DH:(h + 1) * MOBA_DH] = o[h * t_new:(h + 1) * t_new, h * MOBA_DH:(h + 1) * MOBA_DH]


def _moba_sample(q, k_new, v_new, cache_k4, cache_v4, page_table, layer):
    bsz, t_new, w = q.shape
    n_pages = page_table.shape[1]
    nq = MOBA_HEADS * t_new
    head_of_lane = jnp.arange(w) // MOBA_DH
    head_mask = (head_of_lane[None, :] == jnp.arange(MOBA_HEADS)[:, None]).astype(F32)
    q_rows = (q[:, None, :, :] * head_mask[None, :, None, :]).reshape(bsz, nq, w)
    pad_keys = nq - t_new
    kn = jnp.pad(k_new, ((0, 0), (0, pad_keys), (0, 0)))
    vn = jnp.pad(v_new, ((0, 0), (0, pad_keys), (0, 0)))

    def page_spec(j):
        return pl.BlockSpec((None, None, PAGE_SIZE, w), lambda b, pt: (layer, pt[b * n_pages + j], 0, 0))

    grid_spec = pltpu.PrefetchScalarGridSpec(
        num_scalar_prefetch=1,
        grid=(bsz,),
        in_specs=([pl.BlockSpec((None, nq, w), lambda b, pt: (b, 0, 0))] * 3
                  + [page_spec(j) for j in range(n_pages)] * 2),
        out_specs=pl.BlockSpec((None, t_new, w), lambda b, pt: (b, 0, 0)),
        scratch_shapes=[pltpu.VMEM((nq, n_pages * PAGE_SIZE), F32),
                        pltpu.VMEM((n_pages * PAGE_SIZE // MOBA_BLOCK, w), F32)],
    )
    return pl.pallas_call(
        functools.partial(_moba_sample_kernel, n_pages=n_pages, t_new=t_new),
        grid_spec=grid_spec,
        out_shape=jax.ShapeDtypeStruct((bsz, t_new, w), F32),
        compiler_params=_cparams("arbitrary"),
    )(page_table.reshape(-1), q_rows, kn, vn, *([cache_k4] * n_pages), *([cache_v4] * n_pages))


def _mix_kernel(ya_ref, ob_ref, oc_ref, od_ref, gl_ref, x_ref, g1_ref, wglu_ref, wb_ref, wc_ref, wd_ref, wo_ref,
                lng_ref, lnb_ref, o_ref):
    d = D_MODEL
    z = _dot(ya_ref[...].astype(BF16), wglu_ref[...])
    merged = jax.nn.sigmoid(gl_ref[:, 0:d]) * (z[:, 0:d] * jax.nn.sigmoid(z[:, d:2 * d]))
    merged += jax.nn.sigmoid(gl_ref[:, d:2 * d]) * _dot(ob_ref[...].astype(BF16), wb_ref[...])
    merged += jax.nn.sigmoid(gl_ref[:, 2 * d:3 * d]) * _dot(oc_ref[...].astype(BF16), wc_ref[...])
    merged += jax.nn.sigmoid(gl_ref[:, 3 * d:4 * d]) * _dot(od_ref[...].astype(BF16), wd_ref[...])
    mix = _dot(merged.astype(BF16), wo_ref[...])
    o_ref[...] = _layer_norm(DEEPNORM_ALPHA * x_ref[...] + g1_ref[...] * mix, lng_ref[...], lnb_ref[...])


def _mix(ya, ob, oc, od, proj, x, mod, rows_per_seq, wl, tm):
    n, d = x.shape
    w512 = SSM_WIDTH
    tok = lambda width, col=0: pl.BlockSpec((tm, width), lambda i: (i, col))
    const = lambda r, c: pl.BlockSpec((r, c), lambda i: (0, 0))
    return pl.pallas_call(
        _mix_kernel,
        grid=(n // tm,),
        in_specs=[tok(w512), tok(w512), tok(w512), tok(w512), tok(N_BRANCH * d, COL_GATES), tok(d),
                  _mod_spec(mod, tm, rows_per_seq, 2),
                  const(w512, 2 * d), const(w512, d), const(w512, d), const(w512, d), const(d, d),
                  const(1, d), const(1, d)],
        out_specs=tok(d),
        out_shape=jax.ShapeDtypeStruct((n, d), F32),
        compiler_params=_cparams("parallel"),
    )(ya, ob, oc, od, proj, x, mod, wl["w_glu"], wl["w_hgrn"], wl["w_moba"], wl["w_gla"], wl["w_out"],
      wl["ln1_g"], wl["ln1_b"])


def _first_max(vals, lane, width):
    m = jnp.max(vals, axis=-1, keepdims=True)
    first = jnp.min(jnp.where(vals == m, lane, width), axis=-1, keepdims=True)
    return m, first


def _route(h2, wr_ref, br_ref):
    logits = _dot(h2, wr_ref[...], precision=HIGHEST)
    lane = lax.broadcasted_iota(jnp.int32, logits.shape, 1)
    width = logits.shape[-1]
    scores = jax.nn.sigmoid(logits)
    biased = jnp.where(lane < N_EXPERTS, scores + br_ref[...], -jnp.inf)
    group = lane // EXPERTS_PER_GROUP
    best = jnp.zeros((logits.shape[0], 1), jnp.int32)
    best_score = None
    for gidx in range(N_EXPERT_GROUPS):
        vals = jnp.where(group == gidx, biased, -jnp.inf)
        m1, i1 = _first_max(vals, lane, width)
        m2 = jnp.max(jnp.where(lane == i1, -jnp.inf, vals), axis=-1, keepdims=True)
        gs = m1 + m2
        if best_score is None:
            best_score = gs
        else:
            better = gs > best_score
            best = jnp.where(better, gidx, best)
            best_score = jnp.where(better, gs, best_score)
    masked = jnp.where(group == best, biased, -jnp.inf)
    _, i1 = _first_max(masked, lane, width)
    _, i2 = _first_max(jnp.where(lane == i1, -jnp.inf, masked), lane, width)
    w1 = jnp.sum(jnp.where(lane == i1, scores, 0.0), axis=-1, keepdims=True)
    w2 = jnp.sum(jnp.where(lane == i2, scores, 0.0), axis=-1, keepdims=True)
    tot = w1 + w2
    return jnp.where(lane == i1, w1 / tot, 0.0) + jnp.where(lane == i2, w2 / tot, 0.0)


def _moe_kernel(x_ref, sh_ref, sc_ref, g2_ref, wr_ref, br_ref, wg_ref, wu_ref, wd_ref, lng_ref, lnb_ref, o_ref,
                h_ref, comb_ref, acc_ref, *, experts_per_step):
    e = pl.program_id(1)

    @pl.when(e == 0)
    def _():
        h2 = x_ref[...] * (1.0 + sc_ref[...]) + sh_ref[...]
        h_ref[...] = h2.astype(BF16)
        comb_ref[...] = _route(h2, wr_ref, br_ref)
        acc_ref[...] = jnp.zeros(acc_ref.shape, F32)

    hb = h_ref[...]
    lane = lax.broadcasted_iota(jnp.int32, comb_ref.shape, 1)
    for j in range(experts_per_step):
        eid = e * experts_per_step + j
        cw = jnp.sum(jnp.where(lane == eid, comb_ref[...], 0.0), axis=-1, keepdims=True)
        hid = _silu(_dot(hb, wg_ref[j])) * _dot(hb, wu_ref[j]) * cw
        acc_ref[...] += _dot(hid.astype(BF16), wd_ref[j])

    @pl.when(e == pl.num_programs(1) - 1)
    def _():
        o_ref[...] = _layer_norm(DEEPNORM_ALPHA * x_ref[...] + g2_ref[...] * acc_ref[...], lng_ref[...], lnb_ref[...])


def _moe(x, mod, rows_per_seq, wl, w_router, b_router, tm, experts_per_step=4):
    n, d = x.shape
    nh = EXPERT_HIDDEN
    tok = pl.BlockSpec((tm, d), lambda i, e: (i, 0))
    const = lambda r, c: pl.BlockSpec((r, c), lambda i, e: (0, 0))
    return pl.pallas_call(
        functools.partial(_moe_kernel, experts_per_step=experts_per_step),
        grid=(n // tm, N_EXPERTS // experts_per_step),
        in_specs=[tok, _mod_spec(mod, tm, rows_per_seq, 3), _mod_spec(mod, tm, rows_per_seq, 4),
                  _mod_spec(mod, tm, rows_per_seq, 5), const(d, LANES), const(1, LANES),
                  pl.BlockSpec((experts_per_step, d, nh), lambda i, e: (e, 0, 0)),
                  pl.BlockSpec((experts_per_step, d, nh), lambda i, e: (e, 0, 0)),
                  pl.BlockSpec((experts_per_step, nh, d), lambda i, e: (e, 0, 0)),
                  const(1, d), const(1, d)],
        out_specs=tok,
        out_shape=jax.ShapeDtypeStruct((n, d), F32),
        scratch_shapes=[pltpu.VMEM((tm, d), BF16), pltpu.VMEM((tm, LANES), F32), pltpu.VMEM((tm, d), F32)],
        compiler_params=_cparams("parallel", "arbitrary"),
    )(x, mod, mod, mod, w_router, b_router, wl["moe_gate"], wl["moe_up"], wl["moe_down"], wl["ln2_g"], wl["ln2_b"])


def _token_mixers_prompt(proj, bsz, t, wl):
    proj3 = proj.reshape(bsz, t, IN_COLS_PAD)
    u = proj3[:, :, COL_UA:COL_UA + SSM_WIDTH]
    ya, ssm_re, ssm_im = _ssm_prompt(u, wl["ssm"])
    ob, hgrn_s = _lin_rec("hgrn", proj3, (COL_QB, COL_FB, COL_IB, COL_GB), wl["hgrn_extras"], wl["hgrn_norm"],
                          None, 0, REC_CHUNK_PROMPT)
    od, gla_s = _lin_rec("gla", proj3, (COL_QD, COL_KD, COL_VD, COL_GD), wl["gla_extras"], wl["gla_norm"],
                         None, 0, REC_CHUNK_PROMPT)
    oc = _moba_prompt(proj3)
    n = bsz * t
    k_new = proj3[:, :, COL_KC:COL_KC + MOBA_WIDTH].reshape(bsz, t, MOBA_HEADS, MOBA_DH)
    v_new = proj3[:, :, COL_VC:COL_VC + MOBA_WIDTH].reshape(bsz, t, MOBA_HEADS, MOBA_DH)
    return (ya.reshape(n, -1), ob.reshape(n, -1), oc.reshape(n, -1), od.reshape(n, -1),
            (k_new, v_new, ssm_re, ssm_im, hgrn_s, gla_s))


def _token_mixers_sample(proj, bsz, t, wl, layer, cache_k4, cache_v4, page_table, st_re, st_im, st_hgrn, st_gla):
    proj3 = proj.reshape(bsz, t, IN_COLS_PAD)
    u_tm = proj3[:, :, COL_UA:COL_UA + SSM_WIDTH].transpose(1, 0, 2)
    ns = SSM_GROUPS * SSM_STATE
    ya_tm, xr, xi = _ssm_sample(u_tm, st_re[layer].reshape(bsz, ns), st_im[layer].reshape(bsz, ns), wl["ssm"])
    ya = ya_tm.transpose(1, 0, 2)
    ob, hgrn_s = _lin_rec("hgrn", proj3, (COL_QB, COL_FB, COL_IB, COL_GB), wl["hgrn_extras"], wl["hgrn_norm"],
                          st_hgrn, layer, SUB_CHUNK)
    od, gla_s = _lin_rec("gla", proj3, (COL_QD, COL_KD, COL_VD, COL_GD), wl["gla_extras"], wl["gla_norm"],
                         st_gla, layer, SUB_CHUNK)
    q = proj3[:, :, COL_QC:COL_QC + MOBA_WIDTH]
    k_new = proj3[:, :, COL_KC:COL_KC + MOBA_WIDTH]
    v_new = proj3[:, :, COL_VC:COL_VC + MOBA_WIDTH]
    oc = _moba_sample(q, k_new, v_new, cache_k4, cache_v4, page_table, layer)
    n = bsz * t
    return (ya.reshape(n, -1), ob.reshape(n, -1), oc.reshape(n, -1), od.reshape(n, -1),
            (k_new.reshape(bsz, t, MOBA_HEADS, MOBA_DH), v_new.reshape(bsz, t, MOBA_HEADS, MOBA_DH),
             xr.reshape(bsz, SSM_GROUPS, SSM_STATE), xi.reshape(bsz, SSM_GROUPS, SSM_STATE), hgrn_s, gla_s))


def _layer_weights(l, w_in_p, ssm_tabs, lower_bounds, gla_w_gk2, gla_b_gk, hgrn_norm, gla_norm, ssm_w_glu,
                   hgrn_w_proj, moba_w_proj, gla_w_proj, w_out, ln1_g, ln1_b, ln2_g, ln2_b, moe_w_gate, moe_w_up,
                   moe_w_down):
    lb = lower_bounds[l].reshape(1, -1)
    wgk = jnp.pad(gla_w_gk2[l], ((0, LANES - GLA_RANK), (0, 0))).astype(BF16)
    row = lambda a: a.reshape(1, -1)
    return dict(
        w_in=w_in_p[l], ssm={k: v[l] for k, v in ssm_tabs.items()},
        hgrn_extras=(jnp.log(lb), jnp.log1p(-lb), 1.0 - lb), hgrn_norm=hgrn_norm[l],
        gla_extras=(wgk, row(gla_b_gk[l])), gla_norm=gla_norm[l],
        w_glu=ssm_w_glu[l].astype(BF16), w_hgrn=hgrn_w_proj[l].astype(BF16), w_moba=moba_w_proj[l].astype(BF16),
        w_gla=gla_w_proj[l].astype(BF16), w_out=w_out[l].astype(BF16),
        ln1_g=row(ln1_g[l]), ln1_b=row(ln1_b[l]), ln2_g=row(ln2_g[l]), ln2_b=row(ln2_b[l]),
        moe_gate=moe_w_gate[l].astype(BF16), moe_up=moe_w_up[l].astype(BF16), moe_down=moe_w_down[l].astype(BF16))


def kernel(x_prompt, x_sample, cache_k, cache_v, state_ssm_re, state_ssm_im, state_hgrn, state_gla, page_table,
           c_prompt, c_sample, ln_in_g, ln_in_b, w_ada, b_ada, w_in, ssm_lam_re, ssm_lam_im, ssm_log_step,
           ssm_b_re, ssm_b_im, ssm_c_re, ssm_c_im, ssm_d, ssm_w_glu, hgrn_lb, hgrn_norm, hgrn_w_proj, moba_w_proj,
           gla_w_gk2, gla_b_gk, gla_norm, gla_w_proj, w_out, ln1_g, ln1_b, ln2_g, ln2_b, w_router, b_router,
           moe_w_gate, moe_w_up, moe_w_down):
    bp, tp, d = x_prompt.shape
    bs, ts, _ = x_sample.shape
    depth = w_in.shape[0]
    n_pool = cache_k.shape[1]
    np_tok, ns_tok = bp * tp, bs * ts
    tm_p = min(1024, tp)
    tm_s = min(512, ns_tok)

    lb_cum = jnp.cumsum(jax.nn.softmax(hgrn_lb.astype(F32), axis=0), axis=0)
    lower_bounds = lb_cum - lb_cum[0:1]
    w_in_p = jnp.concatenate(
        [w_in[:, :, REF_GATE_START:IN_COLS_REF], w_in[:, :, :REF_RD_START], w_in[:, :, REF_RD_START:REF_GATE_START],
         jnp.zeros((depth, d, IN_COLS_PAD - IN_COLS_REF), w_in.dtype)], axis=-1).astype(BF16)
    n_scan = max(tp // SSM_CHUNK - 1, 0).bit_length()
    ssm_tabs = jax.vmap(functools.partial(_ssm_tables, n_scan=n_scan))(
        ssm_lam_re, ssm_lam_im, ssm_log_step, ssm_b_re, ssm_b_im, ssm_c_re, ssm_c_im, ssm_d)
    w_router_p = jnp.pad(w_router, ((0, 0), (0, LANES - N_EXPERTS)))
    b_router_p = jnp.pad(b_router, (0, LANES - N_EXPERTS)).reshape(1, LANES)
    cache_k4 = cache_k.reshape(depth, n_pool, PAGE_SIZE, MOBA_WIDTH)
    cache_v4 = cache_v.reshape(depth, n_pool, PAGE_SIZE, MOBA_WIDTH)

    nc = bs + bp
    nc_pad = -(-nc // 8) * 8
    c_all = jnp.pad(jnp.concatenate([c_sample, c_prompt], axis=0), ((0, nc_pad - nc), (0, 0)))
    mod_all = _ada_all(c_all, w_ada, b_ada)

    xp = _ln_rows(x_prompt.reshape(np_tok, d), ln_in_g, ln_in_b, tm_p)
    xs = _ln_rows(x_sample.reshape(ns_tok, d), ln_in_g, ln_in_b, tm_s)

    outs_p, outs_s = [], []
    for l in range(depth):
        wl = _layer_weights(l, w_in_p, ssm_tabs, lower_bounds, gla_w_gk2, gla_b_gk, hgrn_norm, gla_norm, ssm_w_glu,
                            hgrn_w_proj, moba_w_proj, gla_w_proj, w_out, ln1_g, ln1_b, ln2_g, ln2_b, moe_w_gate,
                            moe_w_up, moe_w_down)
        mod_p = mod_all[l, bs:bs + bp].reshape(bp, 1, 6 * d)
        mod_s = jnp.repeat(mod_all[l, :bs], ts, axis=0)

        proj_p = _in_proj(xp, mod_p, tp, wl["w_in"], tm_p)
        ya, ob, oc, od, st_p = _token_mixers_prompt(proj_p, bp, tp, wl)
        xp = _mix(ya, ob, oc, od, proj_p, xp, mod_p, tp, wl, min(256, tp))
        xp = _moe(xp, mod_p, tp, wl, w_router_p, b_router_p, tm_p)
        outs_p.append(st_p)

        proj_s = _in_proj(xs, mod_s, ts, wl["w_in"], tm_s)
        ya, ob, oc, od, st_s = _token_mixers_sample(proj_s, bs, ts, wl, l, cache_k4, cache_v4, page_table,
                                                    state_ssm_re, state_ssm_im, state_hgrn, state_gla)
        xs = _mix(ya, ob, oc, od, proj_s, xs, mod_s, ts, wl, min(256, ns_tok))
        xs = _moe(xs, mod_s, ts, wl, w_router_p, b_router_p, tm_s)
        outs_s.append(st_s)

    stack = lambda outs, idx: jnp.stack([o[idx] for o in outs])
    return (xp.reshape(bp, tp, d), xs.reshape(bs, ts, d),
            stack(outs_p, 0), stack(outs_p, 1), stack(outs_s, 0), stack(outs_s, 1),
            stack(outs_p, 2), stack(outs_p, 3), stack(outs_s, 2), stack(outs_s, 3),
            stack(outs_p, 4), stack(outs_s, 4), stack(outs_p, 5), stack(outs_s, 5))
```

```python
import functools
import math

import jax
import jax.numpy as jnp
from jax import lax
from jax.experimental import pallas as pl
from jax.experimental.pallas import tpu as pltpu

F32 = jnp.float32
BF16 = jnp.bfloat16
HIGHEST = lax.Precision.HIGHEST

D_MODEL = 1024
DEPTH = 4
PAGE_SIZE = 128
SSM_WIDTH = 512
SSM_GROUP = 16
SSM_GROUPS = 32
SSM_STATE = 64
SSM_MAX_RE = -1e-4
SSM_CHUNK = 16
SSM_PAIRS = SSM_GROUPS // 2
HGRN_HEADS = 4
HGRN_DK = 128
HGRN_DV = 128
MOBA_HEADS = 4
MOBA_DH = 128
MOBA_WIDTH = 512
MOBA_BLOCK = 256
MOBA_TOPK = 3
GLA_HEADS = 4
GLA_DK = 64
GLA_DV = 128
GLA_RANK = 16
GLA_TAU = 16.0
N_BRANCH = 4
N_EXPERTS = 16
N_EXPERT_GROUPS = 4
EXPERTS_PER_GROUP = 4
EXPERT_HIDDEN = 256
DEEPNORM_ALPHA = (2 * DEPTH) ** 0.25
NORM_EPS = 1e-5

LANES = 128
SUB_CHUNK = 16
REC_CHUNK_PROMPT = 128
MOBA_KEY_TILE = 128
VMEM_LIMIT = 56 * 1024 * 1024
NEG = -1e30

COL_GATES = 0
COL_UA = 4096
COL_QB = 4608
COL_FB = 5120
COL_IB = 5632
COL_GB = 6144
COL_QC = 6656
COL_KC = 7168
COL_VC = 7680
COL_QD = 8192
COL_KD = 8448
COL_VD = 8704
COL_GD = 9216
COL_RD = 9728
IN_COLS_PAD = 10240
IN_COLS_REF = 9744
REF_GATE_START = 5648
REF_RD_START = 5632


def _cparams(*sem):
    return pltpu.CompilerParams(dimension_semantics=sem, vmem_limit_bytes=VMEM_LIMIT)


def _silu(x):
    return x * jax.nn.sigmoid(x)


def _log_sigmoid(x):
    return jnp.minimum(x, 0.0) - jnp.log1p(jnp.exp(-jnp.abs(x)))


def _dot(a, b, **kw):
    return jnp.dot(a, b, preferred_element_type=F32, **kw)


def _dot_nt(a, b, **kw):
    return lax.dot_general(a, b, (((1,), (1,)), ((), ())), preferred_element_type=F32, **kw)


def _dot_tn(a, b, **kw):
    return lax.dot_general(a, b, (((0,), (0,)), ((), ())), preferred_element_type=F32, **kw)


def _layer_norm(x, g, b):
    mu = jnp.mean(x, axis=-1, keepdims=True)
    xc = x - mu
    var = jnp.mean(xc * xc, axis=-1, keepdims=True)
    return xc * lax.rsqrt(var + NORM_EPS) * g + b


def _ln_kernel(x_ref, g_ref, b_ref, o_ref):
    o_ref[...] = _layer_norm(x_ref[...], g_ref[...], b_ref[...])


def _ln_rows(x, g, b, tm):
    n, d = x.shape
    return pl.pallas_call(
        _ln_kernel,
        grid=(n // tm,),
        in_specs=[pl.BlockSpec((tm, d), lambda i: (i, 0)),
                  pl.BlockSpec((1, d), lambda i: (0, 0)),
                  pl.BlockSpec((1, d), lambda i: (0, 0))],
        out_specs=pl.BlockSpec((tm, d), lambda i: (i, 0)),
        out_shape=jax.ShapeDtypeStruct((n, d), F32),
        compiler_params=_cparams("parallel"),
    )(x, g.reshape(1, d), b.reshape(1, d))


def _ada_kernel(c_ref, w_ref, b_ref, o_ref):
    h = _silu(c_ref[...]).astype(BF16)
    o_ref[...] = _dot(h, w_ref[...].astype(BF16)) + b_ref[...]


def _ada_all(c, w_ada, b_ada, tn=1536):
    nb, d = c.shape
    depth, _, n6 = w_ada.shape
    return pl.pallas_call(
        _ada_kernel,
        grid=(depth, n6 // tn),
        in_specs=[pl.BlockSpec((nb, d), lambda l, j: (0, 0)),
                  pl.BlockSpec((None, d, tn), lambda l, j: (l, 0, j)),
                  pl.BlockSpec((None, 1, tn), lambda l, j: (l, 0, j))],
        out_specs=pl.BlockSpec((None, nb, tn), lambda l, j: (l, 0, j)),
        out_shape=jax.ShapeDtypeStruct((depth, nb, n6), F32),
        compiler_params=_cparams("parallel", "parallel"),
    )(c, w_ada, b_ada.reshape(depth, 1, n6))


def _mod_spec(mod, tm, rows_per_seq, chunk):
    if mod.ndim == 3:
        tiles_per_seq = rows_per_seq // tm
        return pl.BlockSpec((None, 1, D_MODEL), lambda i, *_: (i // tiles_per_seq, 0, chunk))
    return pl.BlockSpec((tm, D_MODEL), lambda i, *_: (i, chunk))


def _inproj_kernel(x_ref, sh_ref, sc_ref, w_ref, o_ref, k_ref, v_ref, h_ref, *, kv_tile):
    j = pl.program_id(1)

    @pl.when(j == 0)
    def _():
        h_ref[...] = (x_ref[...] * (1.0 + sc_ref[...]) + sh_ref[...]).astype(BF16)

    acc = _dot(h_ref[...], w_ref[...])
    o_ref[...] = acc

    @pl.when(j == kv_tile)
    def _():
        tm = acc.shape[0]
        for h in range(MOBA_HEADS):
            k_ref[pl.ds(h, tm, stride=MOBA_HEADS), :] = acc[:, h * MOBA_DH:(h + 1) * MOBA_DH]
            v_ref[pl.ds(h, tm, stride=MOBA_HEADS), :] = acc[:, MOBA_WIDTH + h * MOBA_DH:MOBA_WIDTH + (h + 1) * MOBA_DH]


def _in_proj(x, mod, rows_per_seq, w, tm, tn=1024):
    n, d = x.shape
    ncol = w.shape[1]
    assert COL_KC % tn == 0 and COL_VC == COL_KC + MOBA_WIDTH and tn == 2 * MOBA_WIDTH
    kv_spec = pl.BlockSpec((tm * MOBA_HEADS, MOBA_DH), lambda i, j: (i, 0))
    kv_shape = jax.ShapeDtypeStruct((n * MOBA_HEADS, MOBA_DH), F32)
    return pl.pallas_call(
        functools.partial(_inproj_kernel, kv_tile=COL_KC // tn),
        grid=(n // tm, ncol // tn),
        in_specs=[pl.BlockSpec((tm, d), lambda i, j: (i, 0)),
                  _mod_spec(mod, tm, rows_per_seq, 0),
                  _mod_spec(mod, tm, rows_per_seq, 1),
                  pl.BlockSpec((d, tn), lambda i, j: (0, j))],
        out_specs=[pl.BlockSpec((tm, tn), lambda i, j: (i, j)), kv_spec, kv_spec],
        out_shape=[jax.ShapeDtypeStruct((n, ncol), F32), kv_shape, kv_shape],
        scratch_shapes=[pltpu.VMEM((tm, d), BF16)],
        compiler_params=_cparams("parallel", "arbitrary"),
        name="in_proj",
    )(x, mod, mod, w)


def _ssm_prompt_kernel(u_ref, toep_ref, pre_ref, pim_ref, qre_ref, qim_ref, d_ref, are_ref, aim_ref,
                       y_ref, sre_ref, sim_ref, *, n_steps):
    u = u_ref[...]
    ub = u.astype(BF16)
    xr = _dot(ub, pre_ref[...])
    xi = _dot(ub, pim_ref[...])
    nrow = u.shape[0]
    row = lax.broadcasted_iota(jnp.int32, xr.shape, 0)
    for k in range(n_steps):
        dist = 1 << k
        ar = are_ref[k:k + 1, :]
        ai = aim_ref[k:k + 1, :]
        keep = row >= dist
        sr = jnp.where(keep, pltpu.roll(xr, dist, 0), 0.0)
        si = jnp.where(keep, pltpu.roll(xi, dist, 0), 0.0)
        xr, xi = xr + (ar * sr - ai * si), xi + (ar * si + ai * sr)
    keep = row >= 1
    pr = jnp.where(keep, pltpu.roll(xr, 1, 0), 0.0)
    pi = jnp.where(keep, pltpu.roll(xi, 1, 0), 0.0)
    y = (_dot(ub, toep_ref[...]) + _dot(pr.astype(BF16), qre_ref[...]) + _dot(pi.astype(BF16), qim_ref[...])
         + d_ref[...] * u)
    y_ref[...] = jax.nn.gelu(y)
    sre_ref[...] = xr[nrow - 1:nrow, :]
    sim_ref[...] = xi[nrow - 1:nrow, :]


def _ssm_prompt(u, sp):
    bsz, t, _ = u.shape
    nrow = t // SSM_CHUNK
    n_steps = max(nrow - 1, 0).bit_length()
    cw = 2 * SSM_CHUNK * SSM_GROUP
    u2 = (u.reshape(bsz, nrow, SSM_CHUNK, SSM_PAIRS, 2, SSM_GROUP)
          .transpose(0, 3, 1, 4, 2, 5).reshape(bsz, SSM_PAIRS, nrow, cw))
    y2, sre, sim = pl.pallas_call(
        functools.partial(_ssm_prompt_kernel, n_steps=n_steps),
        grid=(bsz, SSM_PAIRS),
        in_specs=[pl.BlockSpec((None, None, nrow, cw), lambda b, g: (b, g, 0, 0)),
                  pl.BlockSpec((None, cw, cw), lambda b, g: (g, 0, 0)),
                  pl.BlockSpec((None, cw, LANES), lambda b, g: (g, 0, 0)),
                  pl.BlockSpec((None, cw, LANES), lambda b, g: (g, 0, 0)),
                  pl.BlockSpec((None, LANES, cw), lambda b, g: (g, 0, 0)),
                  pl.BlockSpec((None, LANES, cw), lambda b, g: (g, 0, 0)),
                  pl.BlockSpec((None, 1, cw), lambda b, g: (g, 0, 0)),
                  pl.BlockSpec((None, sp["scan_re"].shape[1], LANES), lambda b, g: (g, 0, 0)),
                  pl.BlockSpec((None, sp["scan_im"].shape[1], LANES), lambda b, g: (g, 0, 0))],
        out_specs=[pl.BlockSpec((None, None, nrow, cw), lambda b, g: (b, g, 0, 0)),
                   pl.BlockSpec((None, None, 1, LANES), lambda b, g: (b, g, 0, 0)),
                   pl.BlockSpec((None, None, 1, LANES), lambda b, g: (b, g, 0, 0))],
        out_shape=[jax.ShapeDtypeStruct((bsz, SSM_PAIRS, nrow, cw), F32),
                   jax.ShapeDtypeStruct((bsz, SSM_PAIRS, 1, LANES), F32),
                   jax.ShapeDtypeStruct((bsz, SSM_PAIRS, 1, LANES), F32)],
        compiler_params=_cparams("parallel", "parallel"),
    )(u2, sp["toep"], sp["p_re"], sp["p_im"], sp["q_re"], sp["q_im"], sp["d_chunk"], sp["scan_re"], sp["scan_im"])
    y = (y2.reshape(bsz, SSM_PAIRS, nrow, 2, SSM_CHUNK, SSM_GROUP)
         .transpose(0, 2, 4, 1, 3, 5).reshape(bsz, t, SSM_WIDTH))
    return y, sre.reshape(bsz, SSM_GROUPS, SSM_STATE), sim.reshape(bsz, SSM_GROUPS, SSM_STATE)


def _ssm_sample_kernel(u_ref, x0r_ref, x0i_ref, bbr_ref, bbi_ref, cr_ref, ci_ref, ar_ref, ai_ref, d_ref,
                       y_ref, xr_ref, xi_ref):
    xr = x0r_ref[...]
    xi = x0i_ref[...]
    ar = ar_ref[...]
    ai = ai_ref[...]
    for t in range(u_ref.shape[0]):
        u = u_ref[t]
        ub = u.astype(BF16)
        xr, xi = (ar * xr - ai * xi + _dot(ub, bbr_ref[...]),
                  ar * xi + ai * xr + _dot(ub, bbi_ref[...]))
        y = _dot(xr.astype(BF16), cr_ref[...]) - _dot(xi.astype(BF16), ci_ref[...]) + d_ref[...] * u
        y_ref[t] = jax.nn.gelu(y)
    xr_ref[...] = xr
    xi_ref[...] = xi


def _ssm_sample(u_tm, x0r, x0i, sp):
    t, bsz, w = u_tm.shape
    ns = SSM_GROUPS * SSM_STATE
    full = lambda *shape: pl.BlockSpec(shape, lambda i: (0,) * len(shape))
    return pl.pallas_call(
        _ssm_sample_kernel,
        grid=(1,),
        in_specs=[full(t, bsz, w), full(bsz, ns), full(bsz, ns), full(w, ns), full(w, ns), full(ns, w), full(ns, w),
                  full(1, ns), full(1, ns), full(1, w)],
        out_specs=[full(t, bsz, w), full(bsz, ns), full(bsz, ns)],
        out_shape=[jax.ShapeDtypeStruct((t, bsz, w), F32), jax.ShapeDtypeStruct((bsz, ns), F32),
                   jax.ShapeDtypeStruct((bsz, ns), F32)],
        compiler_params=_cparams("arbitrary"),
    )(u_tm, x0r, x0i, sp["bb_re"], sp["bb_im"], sp["c_re"], sp["c_im"], sp["a_re"], sp["a_im"], sp["d_row"])


def _ssm_tables(lam_re, lam_im, log_step, b_re, b_im, c_re, c_im, d, n_scan):
    hp = dict(precision=HIGHEST)
    g, p, hch = SSM_GROUPS, SSM_STATE, SSM_GROUP
    step = jnp.exp(log_step)[:, None]
    lr = jnp.minimum(lam_re, SSM_MAX_RE)
    li = lam_im
    mag = jnp.exp(lr * step)
    ab_re = mag * jnp.cos(li * step)
    ab_im = mag * jnp.sin(li * step)
    den = lr * lr + li * li
    coef_re = ((ab_re - 1.0) * lr + ab_im * li) / den
    coef_im = (ab_im * lr - (ab_re - 1.0) * li) / den
    bb_re = coef_re[..., None] * b_re - coef_im[..., None] * b_im
    bb_im = coef_re[..., None] * b_im + coef_im[..., None] * b_re

    def cmul(xr, xi, yr, yi):
        return xr * yr - xi * yi, xr * yi + xi * yr

    pows_re, pows_im = [jnp.ones_like(ab_re)], [jnp.zeros_like(ab_im)]
    for _ in range(SSM_CHUNK):
        nr, ni = cmul(pows_re[-1], pows_im[-1], ab_re, ab_im)
        pows_re.append(nr)
        pows_im.append(ni)
    pw_re = jnp.stack(pows_re)
    pw_im = jnp.stack(pows_im)
    ajb_re = pw_re[:SSM_CHUNK, :, :, None] * bb_re - pw_im[:SSM_CHUNK, :, :, None] * bb_im
    ajb_im = pw_re[:SSM_CHUNK, :, :, None] * bb_im + pw_im[:SSM_CHUNK, :, :, None] * bb_re
    klag = (jnp.einsum("jgpi,gop->jgio", ajb_re, c_re, **hp) - jnp.einsum("jgpi,gop->jgio", ajb_im, c_im, **hp))
    ts = jnp.arange(SSM_CHUNK)
    lag = ts[None, :] - ts[:, None]
    toep = jnp.where((lag >= 0)[:, :, None, None, None], klag[jnp.clip(lag, 0, SSM_CHUNK - 1)], 0.0)
    toep = toep.transpose(2, 0, 3, 1, 4).reshape(g, SSM_CHUNK * hch, SSM_CHUNK * hch)
    rev_re = pw_re[:SSM_CHUNK][::-1]
    rev_im = pw_im[:SSM_CHUNK][::-1]
    p_re = (rev_re[..., None] * bb_re - rev_im[..., None] * bb_im).transpose(1, 0, 3, 2).reshape(g, SSM_CHUNK * hch, p)
    p_im = (rev_re[..., None] * bb_im + rev_im[..., None] * bb_re).transpose(1, 0, 3, 2).reshape(g, SSM_CHUNK * hch, p)
    a1_re = pw_re[1:]
    a1_im = pw_im[1:]
    cr_t = c_re.transpose(0, 2, 1)
    ci_t = c_im.transpose(0, 2, 1)
    q_re = (cr_t[None] * a1_re[..., None] - ci_t[None] * a1_im[..., None]).transpose(1, 2, 0, 3).reshape(g, p, SSM_CHUNK * hch)
    q_im = (-(cr_t[None] * a1_im[..., None] + ci_t[None] * a1_re[..., None])).transpose(1, 2, 0, 3).reshape(g, p, SSM_CHUNK * hch)
    d_chunk = jnp.broadcast_to(d[:, None, :], (g, SSM_CHUNK, hch)).reshape(g, 1, SSM_CHUNK * hch)
    sc_re, sc_im = [pw_re[SSM_CHUNK]], [pw_im[SSM_CHUNK]]
    for _ in range(max(n_scan, 1) - 1):
        nr, ni = cmul(sc_re[-1], sc_im[-1], sc_re[-1], sc_im[-1])
        sc_re.append(nr)
        sc_im.append(ni)
    scan_re = jnp.stack(sc_re, axis=1)
    scan_im = jnp.stack(sc_im, axis=1)

    def pair_diag(m):
        m2 = m.reshape(SSM_PAIRS, 2, m.shape[1], m.shape[2])
        z = jnp.zeros_like(m2[:, 0])
        top = jnp.concatenate([m2[:, 0], z], axis=2)
        bot = jnp.concatenate([z, m2[:, 1]], axis=2)
        return jnp.concatenate([top, bot], axis=1)

    def pair_cat(m):
        m2 = m.reshape(SSM_PAIRS, 2, m.shape[1], m.shape[2])
        return jnp.concatenate([m2[:, 0], m2[:, 1]], axis=2)

    def dense_diag(blocks):
        eye = jnp.eye(g, dtype=blocks.dtype)
        return (blocks[:, :, None, :] * eye[:, None, :, None]).reshape(g * blocks.shape[1], g * blocks.shape[2])

    return dict(
        toep=pair_diag(toep).astype(BF16), p_re=pair_diag(p_re).astype(BF16), p_im=pair_diag(p_im).astype(BF16),
        q_re=pair_diag(q_re).astype(BF16), q_im=pair_diag(q_im).astype(BF16), d_chunk=pair_cat(d_chunk),
        scan_re=pair_cat(scan_re), scan_im=pair_cat(scan_im),
        bb_re=dense_diag(bb_re.transpose(0, 2, 1)).astype(BF16), bb_im=dense_diag(bb_im.transpose(0, 2, 1)).astype(BF16),
        c_re=dense_diag(cr_t).astype(BF16), c_im=dense_diag(ci_t).astype(BF16),
        a_re=ab_re.reshape(1, g * p), a_im=ab_im.reshape(1, g * p), d_row=d.reshape(1, g * hch))


def _rec_levels(chunk):
    return [1 << p for p in range(chunk.bit_length() - 1)]


def _rec_tables(chunk):
    t = jnp.arange(chunk)[:, None]
    u = jnp.arange(chunk)[None, :]
    tabs = [u <= t]
    for half in _rec_levels(chunk):
        if half < 8:
            tabs.append(u <= (t // (2 * half)) * (2 * half) + half - 1)
    return jnp.concatenate(tabs, axis=0).astype(BF16)


def _linrec_kernel(*refs, mode, chunk, rows, heads, dk, dv, has_s0):
    it = iter(refs)
    q_ref, k_ref, v_ref, g_ref = next(it), next(it), next(it), next(it)
    if mode == "hgrn":
        loglb_ref, log1mlb_ref, onemlb_ref = next(it), next(it), next(it)
    else:
        r_ref, wgk_ref, bgk_ref = next(it), next(it), next(it)
    normw_ref = next(it)
    s0_ref = next(it) if has_s0 else None
    tstack_ref = next(it)
    o_ref, sfin_ref = next(it), next(it)
    st_ref = next(it)
    pad_refs = [next(it) for _ in range(5)] if rows != chunk else None

    c = pl.program_id(1)

    @pl.when(c == 0)
    def _():
        for h in range(heads):
            if has_s0:
                st_ref[h] = s0_ref[h].T
            else:
                st_ref[h] = jnp.zeros((dv, dk), F32)

    def load(ref, slot):
        if pad_refs is None:
            return ref[...]
        buf = pad_refs[slot]
        buf[...] = jnp.zeros(buf.shape, F32)
        buf[0:rows, 0:ref.shape[-1]] = ref[...]
        return buf[:, 0:ref.shape[-1]]

    q_all, k_all, v_all, g_all = load(q_ref, 0), load(k_ref, 1), load(v_ref, 2), load(g_ref, 3)
    trow = lax.broadcasted_iota(jnp.int32, (chunk, chunk), 0)
    tcol = lax.broadcasted_iota(jnp.int32, (chunk, chunk), 1)
    dxor = trow ^ tcol
    rowi = lax.broadcasted_iota(jnp.int32, (chunk, 1), 0)
    valid = rowi < rows
    if mode == "gla":
        r_all = load(r_ref, 4)
        gk_all = _dot(r_all.astype(BF16), wgk_ref[...]) + bgk_ref[...]
    levels = _rec_levels(chunk)
    tstack = tstack_ref[...]

    for h in range(heads):
        ksl = slice(h * dk, (h + 1) * dk)
        vsl = slice(h * dv, (h + 1) * dv)
        if mode == "hgrn":
            q = _silu(q_all[:, ksl])
            z = k_all[:, ksl]
            la = loglb_ref[:, ksl]
            lc = log1mlb_ref[:, ksl] + _log_sigmoid(z)
            lf = jnp.maximum(la, lc) + jnp.log1p(jnp.exp(-jnp.abs(la - lc)))
            k = onemlb_ref[:, ksl] * jax.nn.sigmoid(-z)
        else:
            q = q_all[:, ksl] * (dk ** -0.5)
            k = k_all[:, ksl]
            lf = _log_sigmoid(gk_all[:, ksl]) / GLA_TAU
        v = v_all[:, vsl]
        if rows != chunk:
            lf = jnp.where(valid, lf, 0.0)
        hi = lf.astype(BF16)
        rest = lf - hi.astype(F32)
        mid = rest.astype(BF16)
        lo = (rest - mid.astype(F32)).astype(BF16)
        sums = _dot(tstack, hi) + _dot(tstack, mid) + _dot(tstack, lo)
        b = sums[0:chunk]
        st = st_ref[h]
        o = _dot_nt((q * jnp.exp(b)).astype(BF16), st.astype(BF16))
        b_last = b[chunk - 1:chunk, :]
        k_dec = k * jnp.exp(b_last - b)
        st_new = st * jnp.exp(b_last) + _dot_tn(v.astype(BF16), k_dec.astype(BF16))
        st_ref[h] = st_new

        att = jnp.broadcast_to(jnp.sum(q * k, axis=-1, keepdims=True), (chunk, chunk))
        for li, half in enumerate(levels):
            if half < 8:
                beta = sums[(li + 1) * chunk:(li + 2) * chunk]
            else:
                nblk = chunk // (2 * half)
                b3 = b.reshape(nblk, 2 * half, dk)
                beta = jnp.broadcast_to(b3[:, half - 1:half, :], (nblk, 2 * half, dk)).reshape(chunk, dk)
            e = jnp.exp(-jnp.abs(b - beta))
            upper = (rowi & half) != 0
            qh = jnp.where(upper, q * e, 0.0)
            kh = jnp.where(upper, 0.0, k * e)
            att = jnp.where(dxor >= half, _dot_nt(qh.astype(BF16), kh.astype(BF16)), att)
        att = jnp.where(trow >= tcol, att, 0.0)
        o = o + _dot(att.astype(BF16), v.astype(BF16))
        o = o * lax.rsqrt(jnp.mean(o * o, axis=-1, keepdims=True) + NORM_EPS) * normw_ref[...]
        o = o * _silu(g_all[:, vsl])
        o_ref[:, vsl] = o[0:rows, :]

    @pl.when(c == pl.num_programs(1) - 1)
    def _():
        for h in range(heads):
            sfin_ref[h] = st_ref[h].T


def _lin_rec(mode, proj3, cols, extras, normw, s0, layer, chunk):
    bsz, t, _ = proj3.shape
    heads, dk, dv = (HGRN_HEADS, HGRN_DK, HGRN_DV) if mode == "hgrn" else (GLA_HEADS, GLA_DK, GLA_DV)
    rows = min(chunk, t)
    nchunk = max(t // chunk, 1)
    wk, wv = heads * dk, heads * dv

    def col_spec(width, col):
        return pl.BlockSpec((None, rows, width), lambda b, c: (b, c, col // width))

    in_specs = [col_spec(wk, cols[0]), col_spec(wk, cols[1]), col_spec(wv, cols[2]), col_spec(wv, cols[3])]
    args = [proj3, proj3, proj3, proj3]
    if mode == "hgrn":
        in_specs += [pl.BlockSpec((1, wk), lambda b, c: (0, 0))] * 3
        args += list(extras)
    else:
        in_specs += [col_spec(LANES, COL_RD), pl.BlockSpec((LANES, wk), lambda b, c: (0, 0)),
                     pl.BlockSpec((1, wk), lambda b, c: (0, 0))]
        args += [proj3] + list(extras)
    in_specs.append(pl.BlockSpec((1, dv), lambda b, c: (0, 0)))
    args.append(normw.reshape(1, dv))
    has_s0 = s0 is not None
    if has_s0:
        in_specs.append(pl.BlockSpec((None, None, heads, dk, dv), lambda b, c: (layer, b, 0, 0, 0)))
        args.append(s0)
    tables = _rec_tables(chunk)
    in_specs.append(pl.BlockSpec(tables.shape, lambda b, c: (0, 0)))
    args.append(tables)
    scratch = [pltpu.VMEM((heads, dv, dk), F32)]
    if rows != chunk:
        scratch += [pltpu.VMEM((chunk, wk), F32), pltpu.VMEM((chunk, wk), F32), pltpu.VMEM((chunk, wv), F32),
                    pltpu.VMEM((chunk, wv), F32), pltpu.VMEM((chunk, LANES), F32)]
    return pl.pallas_call(
        functools.partial(_linrec_kernel, mode=mode, chunk=chunk, rows=rows, heads=heads, dk=dk, dv=dv,
                          has_s0=has_s0),
        grid=(bsz, nchunk),
        in_specs=in_specs,
        out_specs=[pl.BlockSpec((None, rows, wv), lambda b, c: (b, c, 0)),
                   pl.BlockSpec((None, heads, dk, dv), lambda b, c: (b, 0, 0, 0))],
        out_shape=[jax.ShapeDtypeStruct((bsz, t, wv), F32), jax.ShapeDtypeStruct((bsz, heads, dk, dv), F32)],
        scratch_shapes=scratch,
        compiler_params=_cparams("parallel", "arbitrary"),
        name=f"{mode}_rec",
    )(*args)


def _kmean_kernel(k_ref, o_ref):
    o_ref[...] = jnp.mean(k_ref[...], axis=0, keepdims=True)


def _block_key_means(proj3):
    bsz, t, _ = proj3.shape
    nb = t // MOBA_BLOCK
    out = pl.pallas_call(
        _kmean_kernel,
        grid=(bsz, nb),
        in_specs=[pl.BlockSpec((None, MOBA_BLOCK, MOBA_WIDTH), lambda b, n: (b, n, COL_KC // MOBA_WIDTH))],
        out_specs=pl.BlockSpec((None, None, 1, MOBA_WIDTH), lambda b, n: (b, n, 0, 0)),
        out_shape=jax.ShapeDtypeStruct((bsz, nb, 1, MOBA_WIDTH), F32),
        compiler_params=_cparams("parallel", "parallel"),
    )(proj3)
    return out.reshape(bsz, nb, MOBA_WIDTH)


def _select_top_blocks(gscore, n_valid, idx, axis):
    g = jnp.where(idx < n_valid, gscore, -jnp.inf)
    sel = jnp.zeros(gscore.shape, F32)
    width = gscore.shape[axis]
    for _ in range(MOBA_TOPK):
        m = jnp.max(g, axis=axis, keepdims=True)
        first = jnp.min(jnp.where(g == m, idx, width), axis=axis, keepdims=True)
        pick = idx == jnp.where(m > -jnp.inf, first, -1)
        sel = jnp.where(pick, 1.0, sel)
        g = jnp.where(pick, -jnp.inf, g)
    return sel


def _moba_prompt_kernel(q_ref, k_ref, v_ref, km_ref, o_ref, kb_ref, vt_ref, sel_ref, s_ref, p_ref):
    i = pl.program_id(2)
    blk = MOBA_BLOCK

    @pl.when(i == 0)
    def _():
        kb_ref[...] = k_ref[...].astype(BF16)

        def transpose_block(n, carry):
            r0 = pl.multiple_of(n * blk, blk)
            vt_ref[:, pl.ds(r0, blk)] = v_ref[pl.ds(r0, blk), :].T.astype(BF16)
            return carry

        lax.fori_loop(0, k_ref.shape[0] // blk, transpose_block, 0)

    q = q_ref[...]
    gscore = _dot_nt(km_ref[...], q, precision=HIGHEST)
    blk_idx = lax.broadcasted_iota(jnp.int32, gscore.shape, 0)
    sel_ref[...] = _select_top_blocks(gscore, i, blk_idx, 0)
    qb = (q * (MOBA_DH ** -0.5)).astype(BF16)
    def scores(n):
        return _dot_nt(kb_ref[pl.ds(pl.multiple_of(n * blk, blk), blk), :], qb)

    def values(n, p):
        return _dot(vt_ref[:, pl.ds(pl.multiple_of(n * blk, blk), blk)], p)

    def softmax_step(s, m, l, acc, pv):
        m_new = jnp.maximum(m, jnp.max(s, axis=0, keepdims=True))
        alpha = jnp.exp(m - m_new)
        p = jnp.exp(s - m_new)
        p_ref[...] = p.astype(BF16)
        l = alpha * l + jnp.sum(p, axis=0, keepdims=True)
        return m_new, l, alpha * (acc + pv)

    s_ref[0] = scores(0)
    p_ref[...] = jnp.zeros(p_ref.shape, BF16)

    def past_block(n, carry):
        m, l, acc = carry
        pv = values(jnp.maximum(n - 1, 0), p_ref[...])
        s = s_ref[n % 2]
        s_ref[(n + 1) % 2] = scores(n + 1)
        bias = jnp.where(sel_ref[pl.ds(n, 1), :] > 0.5, 0.0, NEG)
        return softmax_step(s + bias, m, l, acc, pv)

    init = (jnp.full((1, blk), NEG, F32), jnp.zeros((1, blk), F32), jnp.zeros((MOBA_DH, blk), F32))
    m, l, acc = lax.fori_loop(0, i, past_block, init)
    pv = values(jnp.maximum(i - 1, 0), p_ref[...])
    kpos = lax.broadcasted_iota(jnp.int32, (blk, blk), 0)
    qpos = lax.broadcasted_iota(jnp.int32, (blk, blk), 1)
    m, l, acc = softmax_step(jnp.where(kpos <= qpos, s_ref[i % 2], NEG), m, l, acc, pv)
    acc = acc + values(i, p_ref[...])
    o_ref[...] = (acc / l).T


def _moba_prompt(proj3):
    bsz, t, _ = proj3.shape
    nb = t // MOBA_BLOCK
    km = _block_key_means(proj3)
    hq, hk, hv = COL_QC // MOBA_DH, COL_KC // MOBA_DH, COL_VC // MOBA_DH
    return pl.pallas_call(
        _moba_prompt_kernel,
        grid=(bsz, MOBA_HEADS, nb),
        in_specs=[pl.BlockSpec((None, MOBA_BLOCK, MOBA_DH), lambda b, h, i: (b, i, hq + h)),
                  pl.BlockSpec((None, t, MOBA_DH), lambda b, h, i: (b, 0, hk + h)),
                  pl.BlockSpec((None, t, MOBA_DH), lambda b, h, i: (b, 0, hv + h)),
                  pl.BlockSpec((None, nb, MOBA_DH), lambda b, h, i: (b, 0, h))],
        out_specs=pl.BlockSpec((None, MOBA_BLOCK, MOBA_DH), lambda b, h, i: (b, i, h)),
        out_shape=jax.ShapeDtypeStruct((bsz, t, MOBA_WIDTH), F32),
        scratch_shapes=[pltpu.VMEM((t, MOBA_DH), BF16), pltpu.VMEM((MOBA_DH, t), BF16),
                        pltpu.VMEM((nb, MOBA_BLOCK), F32), pltpu.VMEM((2, MOBA_BLOCK, MOBA_BLOCK), F32),
                        pltpu.VMEM((MOBA_BLOCK, MOBA_BLOCK), BF16)],
        compiler_params=_cparams("parallel", "parallel", "arbitrary"),
        name="moba_prompt",
    )(proj3, proj3, proj3, km)


def _moba_sample_kernel(pt_ref, q_ref, kn_ref, vn_ref, seg_ref, *refs, n_pages):
    k_pages = refs[:n_pages]
    v_pages = refs[n_pages:2 * n_pages]
    o_ref = refs[2 * n_pages]
    s_ref, km_ref = refs[2 * n_pages + 1], refs[2 * n_pages + 2]
    nh = MOBA_HEADS
    prow = PAGE_SIZE * nh
    pages_per_block = MOBA_BLOCK // PAGE_SIZE
    n_blocks = n_pages // pages_per_block
    sub = 8
    q = q_ref[...]
    nq = q.shape[0]
    q_head = lax.broadcasted_iota(jnp.int32, (nq, 1), 0) % nh
    q_tok = lax.broadcasted_iota(jnp.int32, (nq, 1), 0) // nh
    for n in range(n_blocks):
        tot = jnp.sum(k_pages[n * pages_per_block][...].reshape(prow // sub, sub, MOBA_DH), axis=0)
        for j in range(1, pages_per_block):
            tot = tot + jnp.sum(k_pages[n * pages_per_block + j][...].reshape(prow // sub, sub, MOBA_DH), axis=0)
        km_ref[n * sub:(n + 1) * sub, :] = tot
    g_all = _dot_nt(q, km_ref[...], precision=HIGHEST)
    g_lane = lax.broadcasted_iota(jnp.int32, g_all.shape, 1)
    g_own = jnp.where(g_lane % nh == q_head, g_all, 0.0)
    gscore = _dot(g_own, seg_ref[...], precision=HIGHEST) * (1.0 / MOBA_BLOCK)
    blk_idx = lax.broadcasted_iota(jnp.int32, gscore.shape, 1)
    sel = _select_top_blocks(gscore, n_blocks, blk_idx, 1)
    qb = (q * (MOBA_DH ** -0.5)).astype(BF16)
    own_head = lax.broadcasted_iota(jnp.int32, (nq, prow), 1) % nh == q_head
    for j in range(n_pages):
        s = _dot_nt(qb, k_pages[j][...].astype(BF16))
        nblk = j // pages_per_block
        keep = jnp.where(own_head, sel[:, nblk:nblk + 1], 0.0) > 0.5
        s_ref[:, j * prow:(j + 1) * prow] = jnp.where(keep, s, NEG)
    s_new = _dot_nt(qb, kn_ref[...].astype(BF16))
    coln = lax.broadcasted_iota(jnp.int32, s_new.shape, 1)
    ok_new = jnp.where(coln % nh == q_head, coln // nh, nq) <= q_tok
    s_new = jnp.where(ok_new, s_new, NEG)
    s_past = s_ref[...]
    m = jnp.maximum(jnp.max(s_past, axis=-1, keepdims=True), jnp.max(s_new, axis=-1, keepdims=True))
    p_past = jnp.exp(s_past - m)
    p_new = jnp.exp(s_new - m)
    l = jnp.sum(p_past, axis=-1, keepdims=True) + jnp.sum(p_new, axis=-1, keepdims=True)
    o = _dot(p_new.astype(BF16), vn_ref[...].astype(BF16))
    for j in range(n_pages):
        o = o + _dot(p_past[:, j * prow:(j + 1) * prow].astype(BF16), v_pages[j][...].astype(BF16))
    o_ref[...] = o / l


def _moba_sample(q, k_new, v_new, cache_k, cache_v, page_table, layer):
    bsz, nq, dh = q.shape
    depth, n_pool = cache_k.shape[:2]
    n_pages = page_table.shape[1]
    prow = PAGE_SIZE * MOBA_HEADS
    ck = cache_k.reshape(depth, n_pool, prow, dh)
    cv = cache_v.reshape(depth, n_pool, prow, dh)
    n_blocks = n_pages * PAGE_SIZE // MOBA_BLOCK
    seg = (jnp.arange(n_blocks * 8)[:, None] // 8 == jnp.arange(n_blocks)[None, :]).astype(F32)

    def page_spec(j):
        return pl.BlockSpec((None, None, prow, dh), lambda b, pt: (layer, pt[b * n_pages + j], 0, 0))

    grid_spec = pltpu.PrefetchScalarGridSpec(
        num_scalar_prefetch=1,
        grid=(bsz,),
        in_specs=([pl.BlockSpec((None, nq, dh), lambda b, pt: (b, 0, 0))] * 3
                  + [pl.BlockSpec(seg.shape, lambda b, pt: (0, 0))]
                  + [page_spec(j) for j in range(n_pages)] * 2),
        out_specs=pl.BlockSpec((None, nq, dh), lambda b, pt: (b, 0, 0)),
        scratch_shapes=[pltpu.VMEM((nq, n_pages * prow), F32), pltpu.VMEM((n_blocks * 8, dh), F32)],
    )
    return pl.pallas_call(
        functools.partial(_moba_sample_kernel, n_pages=n_pages),
        grid_spec=grid_spec,
        out_shape=jax.ShapeDtypeStruct((bsz, nq, dh), F32),
        compiler_params=_cparams("arbitrary"),
        name="moba_sample",
    )(page_table.reshape(-1), q, k_new, v_new, seg, *([ck] * n_pages), *([cv] * n_pages))


def _mix_kernel(ya_ref, ob_ref, oc_ref, od_ref, gl_ref, x_ref, g1_ref, wglu_ref, wb_ref, wc_ref, wd_ref, wo_ref,
                lng_ref, lnb_ref, o_ref):
    d = D_MODEL
    z = _dot(ya_ref[...].astype(BF16), wglu_ref[...])
    merged = jax.nn.sigmoid(gl_ref[:, 0:d]) * (z[:, 0:d] * jax.nn.sigmoid(z[:, d:2 * d]))
    merged += jax.nn.sigmoid(gl_ref[:, d:2 * d]) * _dot(ob_ref[...].astype(BF16), wb_ref[...])
    merged += jax.nn.sigmoid(gl_ref[:, 2 * d:3 * d]) * _dot(oc_ref[...].astype(BF16), wc_ref[...])
    merged += jax.nn.sigmoid(gl_ref[:, 3 * d:4 * d]) * _dot(od_ref[...].astype(BF16), wd_ref[...])
    mix = _dot(merged.astype(BF16), wo_ref[...])
    o_ref[...] = _layer_norm(DEEPNORM_ALPHA * x_ref[...] + g1_ref[...] * mix, lng_ref[...], lnb_ref[...])


def _mix(ya, ob, oc, od, proj, x, mod, rows_per_seq, wl, tm):
    n, d = x.shape
    w512 = SSM_WIDTH
    tok = lambda width, col=0: pl.BlockSpec((tm, width), lambda i: (i, col))
    const = lambda r, c: pl.BlockSpec((r, c), lambda i: (0, 0))
    return pl.pallas_call(
        _mix_kernel,
        grid=(n // tm,),
        in_specs=[tok(w512), tok(w512), tok(w512), tok(w512), tok(N_BRANCH * d, COL_GATES), tok(d),
                  _mod_spec(mod, tm, rows_per_seq, 2),
                  const(w512, 2 * d), const(w512, d), const(w512, d), const(w512, d), const(d, d),
                  const(1, d), const(1, d)],
        out_specs=tok(d),
        out_shape=jax.ShapeDtypeStruct((n, d), F32),
        compiler_params=_cparams("parallel"),
    )(ya, ob, oc, od, proj, x, mod, wl["w_glu"], wl["w_hgrn"], wl["w_moba"], wl["w_gla"], wl["w_out"],
      wl["ln1_g"], wl["ln1_b"])


def _first_max(vals, lane, width):
    m = jnp.max(vals, axis=-1, keepdims=True)
    first = jnp.min(jnp.where(vals == m, lane, width), axis=-1, keepdims=True)
    return m, first


def _route(h2, wr_ref, br_ref):
    logits = _dot(h2, wr_ref[...], precision=HIGHEST)
    lane = lax.broadcasted_iota(jnp.int32, logits.shape, 1)
    width = logits.shape[-1]
    scores = jax.nn.sigmoid(logits)
    biased = jnp.where(lane < N_EXPERTS, scores + br_ref[...], -jnp.inf)
    group = lane // EXPERTS_PER_GROUP
    best = jnp.zeros((logits.shape[0], 1), jnp.int32)
    best_score = None
    for gidx in range(N_EXPERT_GROUPS):
        vals = jnp.where(group == gidx, biased, -jnp.inf)
        m1, i1 = _first_max(vals, lane, width)
        m2 = jnp.max(jnp.where(lane == i1, -jnp.inf, vals), axis=-1, keepdims=True)
        gs = m1 + m2
        if best_score is None:
            best_score = gs
        else:
            better = gs > best_score
            best = jnp.where(better, gidx, best)
            best_score = jnp.where(better, gs, best_score)
    masked = jnp.where(group == best, biased, -jnp.inf)
    _, i1 = _first_max(masked, lane, width)
    _, i2 = _first_max(jnp.where(lane == i1, -jnp.inf, masked), lane, width)
    w1 = jnp.sum(jnp.where(lane == i1, scores, 0.0), axis=-1, keepdims=True)
    w2 = jnp.sum(jnp.where(lane == i2, scores, 0.0), axis=-1, keepdims=True)
    tot = w1 + w2
    return jnp.where(lane == i1, w1 / tot, 0.0) + jnp.where(lane == i2, w2 / tot, 0.0)


def _moe_kernel(x_ref, sh_ref, sc_ref, g2_ref, wr_ref, br_ref, wg_ref, wu_ref, wd_ref, lng_ref, lnb_ref, o_ref,
                h_ref, comb_ref, acc_ref, *, experts_per_step):
    e = pl.program_id(1)

    @pl.when(e == 0)
    def _():
        h2 = x_ref[...] * (1.0 + sc_ref[...]) + sh_ref[...]
        h_ref[...] = h2.astype(BF16)
        comb_ref[...] = _route(h2, wr_ref, br_ref)
        acc_ref[...] = jnp.zeros(acc_ref.shape, F32)

    hb = h_ref[...]
    lane = lax.broadcasted_iota(jnp.int32, comb_ref.shape, 1)
    for j in range(experts_per_step):
        eid = e * experts_per_step + j
        cw = jnp.sum(jnp.where(lane == eid, comb_ref[...], 0.0), axis=-1, keepdims=True)
        hid = _silu(_dot(hb, wg_ref[j])) * _dot(hb, wu_ref[j]) * cw
        acc_ref[...] += _dot(hid.astype(BF16), wd_ref[j])

    @pl.when(e == pl.num_programs(1) - 1)
    def _():
        o_ref[...] = _layer_norm(DEEPNORM_ALPHA * x_ref[...] + g2_ref[...] * acc_ref[...], lng_ref[...], lnb_ref[...])


def _moe(x, mod, rows_per_seq, wl, w_router, b_router, tm, experts_per_step=4):
    n, d = x.shape
    nh = EXPERT_HIDDEN
    tok = pl.BlockSpec((tm, d), lambda i, e: (i, 0))
    const = lambda r, c: pl.BlockSpec((r, c), lambda i, e: (0, 0))
    return pl.pallas_call(
        functools.partial(_moe_kernel, experts_per_step=experts_per_step),
        grid=(n // tm, N_EXPERTS // experts_per_step),
        in_specs=[tok, _mod_spec(mod, tm, rows_per_seq, 3), _mod_spec(mod, tm, rows_per_seq, 4),
                  _mod_spec(mod, tm, rows_per_seq, 5), const(d, LANES), const(1, LANES),
                  pl.BlockSpec((experts_per_step, d, nh), lambda i, e: (e, 0, 0)),
                  pl.BlockSpec((experts_per_step, d, nh), lambda i, e: (e, 0, 0)),
                  pl.BlockSpec((experts_per_step, nh, d), lambda i, e: (e, 0, 0)),
                  const(1, d), const(1, d)],
        out_specs=tok,
        out_shape=jax.ShapeDtypeStruct((n, d), F32),
        scratch_shapes=[pltpu.VMEM((tm, d), BF16), pltpu.VMEM((tm, LANES), F32), pltpu.VMEM((tm, d), F32)],
        compiler_params=_cparams("parallel", "arbitrary"),
    )(x, mod, mod, mod, w_router, b_router, wl["moe_gate"], wl["moe_up"], wl["moe_down"], wl["ln2_g"], wl["ln2_b"])


def _token_mixers_prompt(proj, k_rows, v_rows, bsz, t, wl):
    proj3 = proj.reshape(bsz, t, IN_COLS_PAD)
    u = proj3[:, :, COL_UA:COL_UA + SSM_WIDTH]
    ya, ssm_re, ssm_im = _ssm_prompt(u, wl["ssm"])
    ob, hgrn_s = _lin_rec("hgrn", proj3, (COL_QB, COL_FB, COL_IB, COL_GB), wl["hgrn_extras"], wl["hgrn_norm"],
                          None, 0, REC_CHUNK_PROMPT)
    od, gla_s = _lin_rec("gla", proj3, (COL_QD, COL_KD, COL_VD, COL_GD), wl["gla_extras"], wl["gla_norm"],
                         None, 0, REC_CHUNK_PROMPT)
    oc = _moba_prompt(proj3)
    n = bsz * t
    k_new = k_rows.reshape(bsz, t, MOBA_HEADS, MOBA_DH)
    v_new = v_rows.reshape(bsz, t, MOBA_HEADS, MOBA_DH)
    return (ya.reshape(n, -1), ob.reshape(n, -1), oc.reshape(n, -1), od.reshape(n, -1),
            (k_new, v_new, ssm_re, ssm_im, hgrn_s, gla_s))


def _token_mixers_sample(proj, k_rows, v_rows, bsz, t, wl, layer, cache_k, cache_v, page_table, st_re, st_im,
                         st_hgrn, st_gla):
    proj3 = proj.reshape(bsz, t, IN_COLS_PAD)
    u_tm = proj3[:, :, COL_UA:COL_UA + SSM_WIDTH].transpose(1, 0, 2)
    ns = SSM_GROUPS * SSM_STATE
    ya_tm, xr, xi = _ssm_sample(u_tm, st_re[layer].reshape(bsz, ns), st_im[layer].reshape(bsz, ns), wl["ssm"])
    ya = ya_tm.transpose(1, 0, 2)
    ob, hgrn_s = _lin_rec("hgrn", proj3, (COL_QB, COL_FB, COL_IB, COL_GB), wl["hgrn_extras"], wl["hgrn_norm"],
                          st_hgrn, layer, SUB_CHUNK)
    od, gla_s = _lin_rec("gla", proj3, (COL_QD, COL_KD, COL_VD, COL_GD), wl["gla_extras"], wl["gla_norm"],
                         st_gla, layer, SUB_CHUNK)
    nq = t * MOBA_HEADS
    q = proj3[:, :, COL_QC:COL_QC + MOBA_WIDTH].reshape(bsz, nq, MOBA_DH)
    oc = _moba_sample(q, k_rows.reshape(bsz, nq, MOBA_DH), v_rows.reshape(bsz, nq, MOBA_DH), cache_k, cache_v,
                      page_table, layer)
    n = bsz * t
    return (ya.reshape(n, -1), ob.reshape(n, -1), oc.reshape(n, -1), od.reshape(n, -1),
            (k_rows.reshape(bsz, t, MOBA_HEADS, MOBA_DH), v_rows.reshape(bsz, t, MOBA_HEADS, MOBA_DH),
             xr.reshape(bsz, SSM_GROUPS, SSM_STATE), xi.reshape(bsz, SSM_GROUPS, SSM_STATE), hgrn_s, gla_s))


def _layer_weights(l, w_in_p, ssm_tabs, lower_bounds, gla_w_gk2, gla_b_gk, hgrn_norm, gla_norm, ssm_w_glu,
                   hgrn_w_proj, moba_w_proj, gla_w_proj, w_out, ln1_g, ln1_b, ln2_g, ln2_b, moe_w_gate, moe_w_up,
                   moe_w_down):
    lb = lower_bounds[l].reshape(1, -1)
    wgk = jnp.pad(gla_w_gk2[l], ((0, LANES - GLA_RANK), (0, 0))).astype(BF16)
    row = lambda a: a.reshape(1, -1)
    return dict(
        w_in=w_in_p[l], ssm={k: v[l] for k, v in ssm_tabs.items()},
        hgrn_extras=(jnp.log(lb), jnp.log1p(-lb), 1.0 - lb), hgrn_norm=hgrn_norm[l],
        gla_extras=(wgk, row(gla_b_gk[l])), gla_norm=gla_norm[l],
        w_glu=ssm_w_glu[l].astype(BF16), w_hgrn=hgrn_w_proj[l].astype(BF16), w_moba=moba_w_proj[l].astype(BF16),
        w_gla=gla_w_proj[l].astype(BF16), w_out=w_out[l].astype(BF16),
        ln1_g=row(ln1_g[l]), ln1_b=row(ln1_b[l]), ln2_g=row(ln2_g[l]), ln2_b=row(ln2_b[l]),
        moe_gate=moe_w_gate[l].astype(BF16), moe_up=moe_w_up[l].astype(BF16), moe_down=moe_w_down[l].astype(BF16))


def kernel(x_prompt, x_sample, cache_k, cache_v, state_ssm_re, state_ssm_im, state_hgrn, state_gla, page_table,
           c_prompt, c_sample, ln_in_g, ln_in_b, w_ada, b_ada, w_in, ssm_lam_re, ssm_lam_im, ssm_log_step,
           ssm_b_re, ssm_b_im, ssm_c_re, ssm_c_im, ssm_d, ssm_w_glu, hgrn_lb, hgrn_norm, hgrn_w_proj, moba_w_proj,
           gla_w_gk2, gla_b_gk, gla_norm, gla_w_proj, w_out, ln1_g, ln1_b, ln2_g, ln2_b, w_router, b_router,
           moe_w_gate, moe_w_up, moe_w_down):
    bp, tp, d = x_prompt.shape
    bs, ts, _ = x_sample.shape
    depth = w_in.shape[0]
    n_pool = cache_k.shape[1]
    np_tok, ns_tok = bp * tp, bs * ts
    tm_p = min(1024, tp)
    tm_s = min(512, ns_tok)

    lb_cum = jnp.cumsum(jax.nn.softmax(hgrn_lb.astype(F32), axis=0), axis=0)
    lower_bounds = lb_cum - lb_cum[0:1]
    w_in_p = jnp.concatenate(
        [w_in[:, :, REF_GATE_START:IN_COLS_REF], w_in[:, :, :REF_RD_START], w_in[:, :, REF_RD_START:REF_GATE_START],
         jnp.zeros((depth, d, IN_COLS_PAD - IN_COLS_REF), w_in.dtype)], axis=-1).astype(BF16)
    n_scan = max(tp // SSM_CHUNK - 1, 0).bit_length()
    ssm_tabs = jax.vmap(functools.partial(_ssm_tables, n_scan=n_scan))(
        ssm_lam_re, ssm_lam_im, ssm_log_step, ssm_b_re, ssm_b_im, ssm_c_re, ssm_c_im, ssm_d)
    w_router_p = jnp.pad(w_router, ((0, 0), (0, LANES - N_EXPERTS)))
    b_router_p = jnp.pad(b_router, (0, LANES - N_EXPERTS)).reshape(1, LANES)

    nc = bs + bp
    nc_pad = -(-nc // 8) * 8
    c_all = jnp.pad(jnp.concatenate([c_sample, c_prompt], axis=0), ((0, nc_pad - nc), (0, 0)))
    mod_all = _ada_all(c_all, w_ada, b_ada)

    xp = _ln_rows(x_prompt.reshape(np_tok, d), ln_in_g, ln_in_b, tm_p)
    xs = _ln_rows(x_sample.reshape(ns_tok, d), ln_in_g, ln_in_b, tm_s)

    outs_p, outs_s = [], []
    for l in range(depth):
        wl = _layer_weights(l, w_in_p, ssm_tabs, lower_bounds, gla_w_gk2, gla_b_gk, hgrn_norm, gla_norm, ssm_w_glu,
                            hgrn_w_proj, moba_w_proj, gla_w_proj, w_out, ln1_g, ln1_b, ln2_g, ln2_b, moe_w_gate,
                            moe_w_up, moe_w_down)
        mod_p = mod_all[l, bs:bs + bp].reshape(bp, 1, 6 * d)
        mod_s = jnp.repeat(mod_all[l, :bs], ts, axis=0)

        proj_p, k_rows, v_rows = _in_proj(xp, mod_p, tp, wl["w_in"], tm_p)
        ya, ob, oc, od, st_p = _token_mixers_prompt(proj_p, k_rows, v_rows, bp, tp, wl)
        xp = _mix(ya, ob, oc, od, proj_p, xp, mod_p, tp, wl, min(256, tp))
        xp = _moe(xp, mod_p, tp, wl, w_router_p, b_router_p, tm_p)
        outs_p.append(st_p)

        proj_s, k_rows, v_rows = _in_proj(xs, mod_s, ts, wl["w_in"], tm_s)
        ya, ob, oc, od, st_s = _token_mixers_sample(proj_s, k_rows, v_rows, bs, ts, wl, l, cache_k, cache_v,
                                                    page_table, state_ssm_re, state_ssm_im, state_hgrn, state_gla)
        xs = _mix(ya, ob, oc, od, proj_s, xs, mod_s, ts, wl, min(256, ns_tok))
        xs = _moe(xs, mod_s, ts, wl, w_router_p, b_router_p, tm_s)
        outs_s.append(st_s)

    stack = lambda outs, idx: jnp.stack([o[idx] for o in outs])
    return (xp.reshape(bp, tp, d), xs.reshape(bs, ts, d),
            stack(outs_p, 0), stack(outs_p, 1), stack(outs_s, 0), stack(outs_s, 1),
            stack(outs_p, 2), stack(outs_p, 3), stack(outs_s, 2), stack(outs_s, 3),
            stack(outs_p, 4), stack(outs_s, 4), stack(outs_p, 5), stack(outs_s, 5))
```

```python
import functools
import math

import jax
import jax.numpy as jnp
from jax import lax
from jax.experimental import pallas as pl
from jax.experimental.pallas import tpu as pltpu

F32 = jnp.float32
BF16 = jnp.bfloat16
HIGHEST = lax.Precision.HIGHEST

D_MODEL = 1024
DEPTH = 4
PAGE_SIZE = 128
SSM_WIDTH = 512
SSM_GROUP = 16
SSM_GROUPS = 32
SSM_STATE = 64
SSM_MAX_RE = -1e-4
SSM_CHUNK = 16
SSM_QUAD_GROUPS = 8
HGRN_HEADS = 4
HGRN_DK = 128
HGRN_DV = 128
MOBA_HEADS = 4
MOBA_DH = 128
MOBA_WIDTH = 512
MOBA_BLOCK = 256
MOBA_TOPK = 3
GLA_HEADS = 4
GLA_DK = 64
GLA_DV = 128
GLA_RANK = 16
GLA_TAU = 16.0
N_BRANCH = 4
N_EXPERTS = 16
N_EXPERT_GROUPS = 4
EXPERTS_PER_GROUP = 4
EXPERT_HIDDEN = 256
DEEPNORM_ALPHA = (2 * DEPTH) ** 0.25
NORM_EPS = 1e-5

LANES = 128
SUB_CHUNK = 16
REC_CHUNK_PROMPT = 128
REC_SAMPLE_SEQS = 4
VMEM_LIMIT = 56 * 1024 * 1024
NEG = -1e30

COL_GATES = 0
COL_UA = 4096
COL_QB = 4608
COL_FB = 5120
COL_IB = 5632
COL_GB = 6144
COL_QC = 6656
COL_KC = 7168
COL_VC = 7680
COL_QD = 8192
COL_KD = 8448
COL_VD = 8704
COL_GD = 9216
COL_RD = 9728
IN_COLS_PAD = 10240
IN_COLS_REF = 9744
REF_GATE_START = 5648
REF_RD_START = 5632


def _cparams(*sem):
    return pltpu.CompilerParams(dimension_semantics=sem, vmem_limit_bytes=VMEM_LIMIT)


def _silu(x):
    return x * jax.nn.sigmoid(x)


def _log_sigmoid(x):
    return jnp.minimum(x, 0.0) - jnp.log1p(jnp.exp(-jnp.abs(x)))


def _dot(a, b, **kw):
    return jnp.dot(a, b, preferred_element_type=F32, **kw)


def _dot_nt(a, b, **kw):
    return lax.dot_general(a, b, (((1,), (1,)), ((), ())), preferred_element_type=F32, **kw)


def _dot_tn(a, b, **kw):
    return lax.dot_general(a, b, (((0,), (0,)), ((), ())), preferred_element_type=F32, **kw)


def _layer_norm(x, g, b):
    mu = jnp.mean(x, axis=-1, keepdims=True)
    xc = x - mu
    var = jnp.mean(xc * xc, axis=-1, keepdims=True)
    return xc * lax.rsqrt(var + NORM_EPS) * g + b


def _ln_kernel(x_ref, g_ref, b_ref, o_ref):
    o_ref[...] = _layer_norm(x_ref[...], g_ref[...], b_ref[...])


def _ln_rows(x, g, b, tm):
    n, d = x.shape
    return pl.pallas_call(
        _ln_kernel,
        grid=(n // tm,),
        in_specs=[pl.BlockSpec((tm, d), lambda i: (i, 0)),
                  pl.BlockSpec((1, d), lambda i: (0, 0)),
                  pl.BlockSpec((1, d), lambda i: (0, 0))],
        out_specs=pl.BlockSpec((tm, d), lambda i: (i, 0)),
        out_shape=jax.ShapeDtypeStruct((n, d), F32),
        compiler_params=_cparams("parallel"),
    )(x, g.reshape(1, d), b.reshape(1, d))


def _ada_kernel(c_ref, w_ref, b_ref, o_ref):
    h = _silu(c_ref[...]).astype(BF16)
    o_ref[...] = _dot(h, w_ref[...].astype(BF16)) + b_ref[...]


def _ada_all(c, w_ada, b_ada, tn=1536):
    nb, d = c.shape
    depth, _, n6 = w_ada.shape
    return pl.pallas_call(
        _ada_kernel,
        grid=(depth, n6 // tn),
        in_specs=[pl.BlockSpec((nb, d), lambda l, j: (0, 0)),
                  pl.BlockSpec((None, d, tn), lambda l, j: (l, 0, j)),
                  pl.BlockSpec((None, 1, tn), lambda l, j: (l, 0, j))],
        out_specs=pl.BlockSpec((None, nb, tn), lambda l, j: (l, 0, j)),
        out_shape=jax.ShapeDtypeStruct((depth, nb, n6), F32),
        compiler_params=_cparams("parallel", "parallel"),
    )(c, w_ada, b_ada.reshape(depth, 1, n6))


def _mod_spec(mod, tm, rows_per_seq, chunk):
    if mod.ndim == 3:
        tiles_per_seq = rows_per_seq // tm
        return pl.BlockSpec((None, 1, D_MODEL), lambda i, *_: (i // tiles_per_seq, 0, chunk))
    return pl.BlockSpec((tm, D_MODEL), lambda i, *_: (i, chunk))


def _inproj_kernel(x_ref, sh_ref, sc_ref, w_ref, o_ref, k_ref, v_ref, h_ref, *, kv_tile):
    j = pl.program_id(1)

    @pl.when(j == 0)
    def _():
        h_ref[...] = (x_ref[...] * (1.0 + sc_ref[...]) + sh_ref[...]).astype(BF16)

    acc = _dot(h_ref[...], w_ref[...])
    o_ref[...] = acc

    @pl.when(j == kv_tile)
    def _():
        tm = acc.shape[0]
        for h in range(MOBA_HEADS):
            k_ref[pl.ds(h, tm, stride=MOBA_HEADS), :] = acc[:, h * MOBA_DH:(h + 1) * MOBA_DH]
            v_ref[pl.ds(h, tm, stride=MOBA_HEADS), :] = acc[:, MOBA_WIDTH + h * MOBA_DH:MOBA_WIDTH + (h + 1) * MOBA_DH]


def _in_proj(x, mod, rows_per_seq, w, tm, tn=1024):
    n, d = x.shape
    ncol = w.shape[1]
    assert COL_KC % tn == 0 and COL_VC == COL_KC + MOBA_WIDTH and tn == 2 * MOBA_WIDTH
    kv_spec = pl.BlockSpec((tm * MOBA_HEADS, MOBA_DH), lambda i, j: (i, 0))
    kv_shape = jax.ShapeDtypeStruct((n * MOBA_HEADS, MOBA_DH), F32)
    return pl.pallas_call(
        functools.partial(_inproj_kernel, kv_tile=COL_KC // tn),
        grid=(n // tm, ncol // tn),
        in_specs=[pl.BlockSpec((tm, d), lambda i, j: (i, 0)),
                  _mod_spec(mod, tm, rows_per_seq, 0),
                  _mod_spec(mod, tm, rows_per_seq, 1),
                  pl.BlockSpec((d, tn), lambda i, j: (0, j))],
        out_specs=[pl.BlockSpec((tm, tn), lambda i, j: (i, j)), kv_spec, kv_spec],
        out_shape=[jax.ShapeDtypeStruct((n, ncol), F32), kv_shape, kv_shape],
        scratch_shapes=[pltpu.VMEM((tm, d), BF16)],
        compiler_params=_cparams("parallel", "arbitrary"),
        name="in_proj",
    )(x, mod, mod, w)


def _ssm_prompt_kernel(u_ref, toep_ref, wp_ref, wq_ref, d_ref, are_ref, aim_ref, y_ref, sre_ref, sim_ref, ucat_ref,
                       *, n_steps):
    nrow = ucat_ref.shape[0]
    ns = SSM_QUAD_GROUPS * SSM_STATE
    for tau in range(SSM_CHUNK):
        ucat_ref[:, tau * LANES:(tau + 1) * LANES] = u_ref[pl.ds(tau, nrow, stride=SSM_CHUNK), :].astype(BF16)
    ucat = ucat_ref[...]
    w = _dot(ucat, wp_ref[...])
    xr = w[:, 0:ns]
    xi = w[:, ns:2 * ns]
    row = lax.broadcasted_iota(jnp.int32, xr.shape, 0)
    for k in range(n_steps):
        dist = 1 << k
        ar = are_ref[k:k + 1, :]
        ai = aim_ref[k:k + 1, :]
        keep = row >= dist
        sr = jnp.where(keep, pltpu.roll(xr, dist, 0), 0.0)
        si = jnp.where(keep, pltpu.roll(xi, dist, 0), 0.0)
        xr, xi = xr + (ar * sr - ai * si), xi + (ar * si + ai * sr)
    keep = row >= 1
    pr = jnp.where(keep, pltpu.roll(xr, 1, 0), 0.0)
    pi = jnp.where(keep, pltpu.roll(xi, 1, 0), 0.0)
    out = (_dot(ucat, toep_ref[...]) + _dot(pr.astype(BF16), wq_ref[0:ns, :])
           + _dot(pi.astype(BF16), wq_ref[ns:2 * ns, :]))
    for tau in range(SSM_CHUNK):
        u_tau = u_ref[pl.ds(tau, nrow, stride=SSM_CHUNK), :]
        y_ref[pl.ds(tau, nrow, stride=SSM_CHUNK), :] = jax.nn.gelu(out[:, tau * LANES:(tau + 1) * LANES]
                                                                   + d_ref[...] * u_tau)
    sre_ref[...] = xr[nrow - 1:nrow, :]
    sim_ref[...] = xi[nrow - 1:nrow, :]


def _ssm_prompt(proj3, sp):
    bsz, t, _ = proj3.shape
    nrow = t // SSM_CHUNK
    n_steps = max(nrow - 1, 0).bit_length()
    nquad = SSM_WIDTH // LANES
    kw = SSM_CHUNK * LANES
    ns = SSM_QUAD_GROUPS * SSM_STATE
    once = pl.Buffered(1)
    y, sre, sim = pl.pallas_call(
        functools.partial(_ssm_prompt_kernel, n_steps=n_steps),
        grid=(nquad, bsz),
        in_specs=[pl.BlockSpec((None, t, LANES), lambda q, b: (b, 0, COL_UA // LANES + q)),
                  pl.BlockSpec((None, kw, kw), lambda q, b: (q, 0, 0), pipeline_mode=once),
                  pl.BlockSpec((None, kw, 2 * ns), lambda q, b: (q, 0, 0), pipeline_mode=once),
                  pl.BlockSpec((None, 2 * ns, kw), lambda q, b: (q, 0, 0), pipeline_mode=once),
                  pl.BlockSpec((1, LANES), lambda q, b: (0, q)),
                  pl.BlockSpec((None, sp["scan_re"].shape[1], ns), lambda q, b: (q, 0, 0)),
                  pl.BlockSpec((None, sp["scan_im"].shape[1], ns), lambda q, b: (q, 0, 0))],
        out_specs=[pl.BlockSpec((None, t, LANES), lambda q, b: (b, 0, q)),
                   pl.BlockSpec((None, None, 1, ns), lambda q, b: (b, q, 0, 0)),
                   pl.BlockSpec((None, None, 1, ns), lambda q, b: (b, q, 0, 0))],
        out_shape=[jax.ShapeDtypeStruct((bsz, t, SSM_WIDTH), F32),
                   jax.ShapeDtypeStruct((bsz, nquad, 1, ns), F32),
                   jax.ShapeDtypeStruct((bsz, nquad, 1, ns), F32)],
        scratch_shapes=[pltpu.VMEM((nrow, kw), BF16)],
        compiler_params=_cparams("arbitrary", "arbitrary"),
        name="ssm_prompt",
    )(proj3, sp["toep"], sp["w_in_state"], sp["w_state_out"], sp["d_row"], sp["scan_re"], sp["scan_im"])
    return y, sre.reshape(bsz, SSM_GROUPS, SSM_STATE), sim.reshape(bsz, SSM_GROUPS, SSM_STATE)


def _ssm_sample_kernel(u_ref, x0r_ref, x0i_ref, bbr_ref, bbi_ref, cr_ref, ci_ref, ar_ref, ai_ref, d_ref,
                       y_ref, xr_ref, xi_ref):
    xr = x0r_ref[...]
    xi = x0i_ref[...]
    ar = ar_ref[...]
    ai = ai_ref[...]
    for t in range(u_ref.shape[0]):
        u = u_ref[t]
        ub = u.astype(BF16)
        xr, xi = (ar * xr - ai * xi + _dot(ub, bbr_ref[...]),
                  ar * xi + ai * xr + _dot(ub, bbi_ref[...]))
        y = _dot(xr.astype(BF16), cr_ref[...]) - _dot(xi.astype(BF16), ci_ref[...]) + d_ref[...] * u
        y_ref[t] = jax.nn.gelu(y)
    xr_ref[...] = xr
    xi_ref[...] = xi


def _ssm_sample(u_tm, x0r, x0i, sp):
    t, bsz, w = u_tm.shape
    ns = SSM_GROUPS * SSM_STATE
    full = lambda *shape: pl.BlockSpec(shape, lambda i: (0,) * len(shape))
    return pl.pallas_call(
        _ssm_sample_kernel,
        grid=(1,),
        in_specs=[full(t, bsz, w), full(bsz, ns), full(bsz, ns), full(w, ns), full(w, ns), full(ns, w), full(ns, w),
                  full(1, ns), full(1, ns), full(1, w)],
        out_specs=[full(t, bsz, w), full(bsz, ns), full(bsz, ns)],
        out_shape=[jax.ShapeDtypeStruct((t, bsz, w), F32), jax.ShapeDtypeStruct((bsz, ns), F32),
                   jax.ShapeDtypeStruct((bsz, ns), F32)],
        compiler_params=_cparams("arbitrary"),
    )(u_tm, x0r, x0i, sp["bb_re"], sp["bb_im"], sp["c_re"], sp["c_im"], sp["a_re"], sp["a_im"], sp["d_row"])


def _ssm_tables(lam_re, lam_im, log_step, b_re, b_im, c_re, c_im, d, n_scan):
    hp = dict(precision=HIGHEST)
    g, p, hch = SSM_GROUPS, SSM_STATE, SSM_GROUP
    step = jnp.exp(log_step)[:, None]
    lr = jnp.minimum(lam_re, SSM_MAX_RE)
    li = lam_im
    mag = jnp.exp(lr * step)
    ab_re = mag * jnp.cos(li * step)
    ab_im = mag * jnp.sin(li * step)
    den = lr * lr + li * li
    coef_re = ((ab_re - 1.0) * lr + ab_im * li) / den
    coef_im = (ab_im * lr - (ab_re - 1.0) * li) / den
    bb_re = coef_re[..., None] * b_re - coef_im[..., None] * b_im
    bb_im = coef_re[..., None] * b_im + coef_im[..., None] * b_re

    def cmul(xr, xi, yr, yi):
        return xr * yr - xi * yi, xr * yi + xi * yr

    pows_re, pows_im = [jnp.ones_like(ab_re)], [jnp.zeros_like(ab_im)]
    for _ in range(SSM_CHUNK):
        nr, ni = cmul(pows_re[-1], pows_im[-1], ab_re, ab_im)
        pows_re.append(nr)
        pows_im.append(ni)
    pw_re = jnp.stack(pows_re)
    pw_im = jnp.stack(pows_im)
    ajb_re = pw_re[:SSM_CHUNK, :, :, None] * bb_re - pw_im[:SSM_CHUNK, :, :, None] * bb_im
    ajb_im = pw_re[:SSM_CHUNK, :, :, None] * bb_im + pw_im[:SSM_CHUNK, :, :, None] * bb_re
    klag = (jnp.einsum("jgpi,gop->jgio", ajb_re, c_re, **hp) - jnp.einsum("jgpi,gop->jgio", ajb_im, c_im, **hp))
    ts = jnp.arange(SSM_CHUNK)
    lag = ts[None, :] - ts[:, None]
    toep = jnp.where((lag >= 0)[:, :, None, None, None], klag[jnp.clip(lag, 0, SSM_CHUNK - 1)], 0.0)
    toep = toep.transpose(2, 0, 3, 1, 4).reshape(g, SSM_CHUNK * hch, SSM_CHUNK * hch)
    rev = jnp.arange(SSM_CHUNK - 1, -1, -1)
    rev_re = pw_re[rev]
    rev_im = pw_im[rev]
    p_re = (rev_re[..., None] * bb_re - rev_im[..., None] * bb_im).transpose(1, 0, 3, 2).reshape(g, SSM_CHUNK * hch, p)
    p_im = (rev_re[..., None] * bb_im + rev_im[..., None] * bb_re).transpose(1, 0, 3, 2).reshape(g, SSM_CHUNK * hch, p)
    a1_re = pw_re[1:]
    a1_im = pw_im[1:]
    cr_t = c_re.transpose(0, 2, 1)
    ci_t = c_im.transpose(0, 2, 1)
    q_re = (cr_t[None] * a1_re[..., None] - ci_t[None] * a1_im[..., None]).transpose(1, 2, 0, 3).reshape(g, p, SSM_CHUNK * hch)
    q_im = (-(cr_t[None] * a1_im[..., None] + ci_t[None] * a1_re[..., None])).transpose(1, 2, 0, 3).reshape(g, p, SSM_CHUNK * hch)
    d_chunk = jnp.broadcast_to(d[:, None, :], (g, SSM_CHUNK, hch)).reshape(g, 1, SSM_CHUNK * hch)
    sc_re, sc_im = [pw_re[SSM_CHUNK]], [pw_im[SSM_CHUNK]]
    for _ in range(max(n_scan, 1) - 1):
        nr, ni = cmul(sc_re[-1], sc_im[-1], sc_re[-1], sc_im[-1])
        sc_re.append(nr)
        sc_im.append(ni)
    scan_re = jnp.stack(sc_re, axis=1)
    scan_im = jnp.stack(sc_im, axis=1)

    nq, gq = g // SSM_QUAD_GROUPS, SSM_QUAD_GROUPS
    eye = jnp.eye(gq, dtype=F32)
    kw = SSM_CHUNK * gq * hch
    toep_q = (toep.reshape(nq, gq, SSM_CHUNK, hch, SSM_CHUNK, 1, hch) * eye[None, :, None, None, None, :, None])
    toep_q = toep_q.transpose(0, 2, 1, 3, 4, 5, 6).reshape(nq, kw, kw)

    def in_state(m):
        m = m.reshape(nq, gq, SSM_CHUNK, hch, 1, p) * eye[None, :, None, None, :, None]
        return m.transpose(0, 2, 1, 3, 4, 5).reshape(nq, kw, gq * p)

    def state_out(m):
        m = m.reshape(nq, gq, p, SSM_CHUNK, 1, hch) * eye[None, :, None, None, :, None]
        return m.reshape(nq, gq * p, kw)

    def scan_rows(m):
        return m.reshape(nq, gq, m.shape[1], p).transpose(0, 2, 1, 3).reshape(nq, m.shape[1], gq * p)

    def dense_diag(blocks):
        eye_g = jnp.eye(g, dtype=blocks.dtype)
        return (blocks[:, :, None, :] * eye_g[:, None, :, None]).reshape(g * blocks.shape[1], g * blocks.shape[2])

    return dict(
        toep=toep_q.astype(BF16),
        w_in_state=jnp.concatenate([in_state(p_re), in_state(p_im)], axis=-1).astype(BF16),
        w_state_out=jnp.concatenate([state_out(q_re), state_out(q_im)], axis=1).astype(BF16),
        scan_re=scan_rows(scan_re), scan_im=scan_rows(scan_im),
        bb_re=dense_diag(bb_re.transpose(0, 2, 1)).astype(BF16), bb_im=dense_diag(bb_im.transpose(0, 2, 1)).astype(BF16),
        c_re=dense_diag(cr_t).astype(BF16), c_im=dense_diag(ci_t).astype(BF16),
        a_re=ab_re.reshape(1, g * p), a_im=ab_im.reshape(1, g * p), d_row=d.reshape(1, g * hch))


def _rec_levels(chunk):
    return [1 << p for p in range(chunk.bit_length() - 1)]


def _rec_tables(chunk):
    t = jnp.arange(chunk)[:, None]
    u = jnp.arange(chunk)[None, :]
    tabs = [u <= t]
    for half in _rec_levels(chunk):
        if half < 8:
            tabs.append(u <= (t // (2 * half)) * (2 * half) + half - 1)
    return jnp.concatenate(tabs, axis=0).astype(BF16)


def _linrec_kernel(*refs, mode, chunk, rows, heads, dk, dv, has_s0, seqs):
    it = iter(refs)
    q_ref, k_ref, v_ref, g_ref = next(it), next(it), next(it), next(it)
    if mode == "hgrn":
        loglb_ref, log1mlb_ref, onemlb_ref = next(it), next(it), next(it)
    else:
        r_ref, wgk_ref, bgk_ref = next(it), next(it), next(it)
    normw_ref = next(it)
    s0_ref = next(it) if has_s0 else None
    tstack_ref = next(it)
    o_ref, sfin_ref = next(it), next(it)
    st_ref = next(it)
    pad_refs = [next(it) for _ in range(5)] if rows != chunk else None

    c = pl.program_id(1)

    @pl.when(c == 0)
    def _():
        for sh in range(seqs * heads):
            if has_s0:
                st_ref[sh] = s0_ref[sh // heads, sh % heads].T
            else:
                st_ref[sh] = jnp.zeros((dv, dk), F32)

    def load(ref, slot, sq):
        if pad_refs is None:
            return ref[sq]
        buf = pad_refs[slot]
        buf[sq] = jnp.zeros(buf.shape[1:], F32)
        buf[sq, 0:rows, :] = ref[sq]
        return buf[sq]

    trow = lax.broadcasted_iota(jnp.int32, (chunk, chunk), 0)
    tcol = lax.broadcasted_iota(jnp.int32, (chunk, chunk), 1)
    dxor = trow ^ tcol
    rowi = lax.broadcasted_iota(jnp.int32, (chunk, 1), 0)
    valid = rowi < rows
    levels = _rec_levels(chunk)
    tstack = tstack_ref[...]

    for sq, h in [(a, b) for a in range(seqs) for b in range(heads)]:
        if h == 0:
            q_all, k_all, v_all, g_all = load(q_ref, 0, sq), load(k_ref, 1, sq), load(v_ref, 2, sq), load(g_ref, 3, sq)
            if mode == "gla":
                gk_all = _dot(load(r_ref, 4, sq).astype(BF16), wgk_ref[...]) + bgk_ref[...]
        sh = sq * heads + h
        ksl = slice(h * dk, (h + 1) * dk)
        vsl = slice(h * dv, (h + 1) * dv)
        if mode == "hgrn":
            q = _silu(q_all[:, ksl])
            z = k_all[:, ksl]
            la = loglb_ref[:, ksl]
            lc = log1mlb_ref[:, ksl] + _log_sigmoid(z)
            lf = jnp.maximum(la, lc) + jnp.log1p(jnp.exp(-jnp.abs(la - lc)))
            k = onemlb_ref[:, ksl] * jax.nn.sigmoid(-z)
        else:
            q = q_all[:, ksl] * (dk ** -0.5)
            k = k_all[:, ksl]
            lf = _log_sigmoid(gk_all[:, ksl]) / GLA_TAU
        v = v_all[:, vsl]
        if rows != chunk:
            lf = jnp.where(valid, lf, 0.0)
        hi = lf.astype(BF16)
        rest = lf - hi.astype(F32)
        mid = rest.astype(BF16)
        lo = (rest - mid.astype(F32)).astype(BF16)
        sums = _dot(tstack, hi) + _dot(tstack, mid) + _dot(tstack, lo)
        b = sums[0:chunk]
        st = st_ref[sh]
        o = _dot_nt((q * jnp.exp(b)).astype(BF16), st.astype(BF16))
        b_last = b[chunk - 1:chunk, :]
        k_dec = k * jnp.exp(b_last - b)
        st_new = st * jnp.exp(b_last) + _dot_tn(v.astype(BF16), k_dec.astype(BF16))
        st_ref[sh] = st_new

        att = jnp.broadcast_to(jnp.sum(q * k, axis=-1, keepdims=True), (chunk, chunk))
        for li, half in enumerate(levels):
            if half < 8:
                beta = sums[(li + 1) * chunk:(li + 2) * chunk]
            else:
                nblk = chunk // (2 * half)
                b3 = b.reshape(nblk, 2 * half, dk)
                beta = jnp.broadcast_to(b3[:, half - 1:half, :], (nblk, 2 * half, dk)).reshape(chunk, dk)
            e = jnp.exp(-jnp.abs(b - beta))
            upper = (rowi & half) != 0
            qh = jnp.where(upper, q * e, 0.0)
            kh = jnp.where(upper, 0.0, k * e)
            att = jnp.where(dxor >= half, _dot_nt(qh.astype(BF16), kh.astype(BF16)), att)
        att = jnp.where(trow >= tcol, att, 0.0)
        o = o + _dot(att.astype(BF16), v.astype(BF16))
        o = o * lax.rsqrt(jnp.mean(o * o, axis=-1, keepdims=True) + NORM_EPS) * normw_ref[...]
        o = o * _silu(g_all[:, vsl])
        o_ref[sq, :, vsl] = o[0:rows, :]

    @pl.when(c == pl.num_programs(1) - 1)
    def _():
        for sh in range(seqs * heads):
            sfin_ref[sh // heads, sh % heads] = st_ref[sh].T


def _lin_rec(mode, proj3, cols, extras, normw, s0, layer, chunk, seqs=1):
    bsz, t, _ = proj3.shape
    heads, dk, dv = (HGRN_HEADS, HGRN_DK, HGRN_DV) if mode == "hgrn" else (GLA_HEADS, GLA_DK, GLA_DV)
    rows = min(chunk, t)
    nchunk = max(t // chunk, 1)
    wk, wv = heads * dk, heads * dv
    assert bsz % seqs == 0

    def col_spec(width, col):
        return pl.BlockSpec((seqs, rows, width), lambda b, c: (b, c, col // width))

    in_specs = [col_spec(wk, cols[0]), col_spec(wk, cols[1]), col_spec(wv, cols[2]), col_spec(wv, cols[3])]
    args = [proj3, proj3, proj3, proj3]
    if mode == "hgrn":
        in_specs += [pl.BlockSpec((1, wk), lambda b, c: (0, 0))] * 3
        args += list(extras)
    else:
        in_specs += [col_spec(LANES, COL_RD), pl.BlockSpec((LANES, wk), lambda b, c: (0, 0)),
                     pl.BlockSpec((1, wk), lambda b, c: (0, 0))]
        args += [proj3] + list(extras)
    in_specs.append(pl.BlockSpec((1, dv), lambda b, c: (0, 0)))
    args.append(normw.reshape(1, dv))
    has_s0 = s0 is not None
    if has_s0:
        in_specs.append(pl.BlockSpec((None, seqs, heads, dk, dv), lambda b, c: (layer, b, 0, 0, 0)))
        args.append(s0)
    tables = _rec_tables(chunk)
    in_specs.append(pl.BlockSpec(tables.shape, lambda b, c: (0, 0)))
    args.append(tables)
    scratch = [pltpu.VMEM((seqs * heads, dv, dk), F32)]
    if rows != chunk:
        scratch += [pltpu.VMEM((seqs, chunk, wk), F32), pltpu.VMEM((seqs, chunk, wk), F32),
                    pltpu.VMEM((seqs, chunk, wv), F32), pltpu.VMEM((seqs, chunk, wv), F32),
                    pltpu.VMEM((seqs, chunk, LANES), F32)]
    return pl.pallas_call(
        functools.partial(_linrec_kernel, mode=mode, chunk=chunk, rows=rows, heads=heads, dk=dk, dv=dv,
                          has_s0=has_s0, seqs=seqs),
        grid=(bsz // seqs, nchunk),
        in_specs=in_specs,
        out_specs=[pl.BlockSpec((seqs, rows, wv), lambda b, c: (b, c, 0)),
                   pl.BlockSpec((seqs, heads, dk, dv), lambda b, c: (b, 0, 0, 0))],
        out_shape=[jax.ShapeDtypeStruct((bsz, t, wv), F32), jax.ShapeDtypeStruct((bsz, heads, dk, dv), F32)],
        scratch_shapes=scratch,
        compiler_params=_cparams("parallel", "arbitrary"),
        name=f"{mode}_rec",
    )(*args)


def _kmean_kernel(k_ref, o_ref):
    o_ref[...] = jnp.mean(k_ref[...], axis=0, keepdims=True)


def _block_key_means(proj3):
    bsz, t, _ = proj3.shape
    nb = t // MOBA_BLOCK
    out = pl.pallas_call(
        _kmean_kernel,
        grid=(bsz, nb),
        in_specs=[pl.BlockSpec((None, MOBA_BLOCK, MOBA_WIDTH), lambda b, n: (b, n, COL_KC // MOBA_WIDTH))],
        out_specs=pl.BlockSpec((None, None, 1, MOBA_WIDTH), lambda b, n: (b, n, 0, 0)),
        out_shape=jax.ShapeDtypeStruct((bsz, nb, 1, MOBA_WIDTH), F32),
        compiler_params=_cparams("parallel", "parallel"),
    )(proj3)
    return out.reshape(bsz, nb, MOBA_WIDTH)


def _select_top_blocks(gscore, n_valid, idx, axis):
    g = jnp.where(idx < n_valid, gscore, -jnp.inf)
    sel = jnp.zeros(gscore.shape, F32)
    width = gscore.shape[axis]
    for _ in range(MOBA_TOPK):
        m = jnp.max(g, axis=axis, keepdims=True)
        first = jnp.min(jnp.where(g == m, idx, width), axis=axis, keepdims=True)
        pick = idx == jnp.where(m > -jnp.inf, first, -1)
        sel = jnp.where(pick, 1.0, sel)
        g = jnp.where(pick, -jnp.inf, g)
    return sel


def _moba_prompt_kernel(q_ref, k_ref, v_ref, km_ref, o_ref, kb_ref, vt_ref, sel_ref):
    i = pl.program_id(2)
    blk = MOBA_BLOCK
    dh = MOBA_DH
    heads = q_ref.shape[-1] // dh
    hs = [slice(j * dh, (j + 1) * dh) for j in range(heads)]

    @pl.when(i == 0)
    def _():
        kb_ref[...] = k_ref[...].astype(BF16)

        def transpose_block(n, carry):
            r0 = pl.multiple_of(n * blk, blk)
            vt_ref[:, pl.ds(r0, blk)] = v_ref[pl.ds(r0, blk), :].T.astype(BF16)
            return carry

        lax.fori_loop(0, k_ref.shape[0] // blk, transpose_block, 0)

    qbs = []
    for j in range(heads):
        q = q_ref[:, hs[j]]
        gscore = _dot_nt(km_ref[:, hs[j]], q, precision=HIGHEST)
        blk_idx = lax.broadcasted_iota(jnp.int32, gscore.shape, 0)
        sel_ref[j] = _select_top_blocks(gscore, i, blk_idx, 0)
        qbs.append((q * (dh ** -0.5)).astype(BF16))

    def scores(j, n):
        return _dot_nt(kb_ref[pl.ds(pl.multiple_of(n * blk, blk), blk), hs[j]], qbs[j])

    def values(j, n, p):
        return _dot(vt_ref[hs[j], pl.ds(pl.multiple_of(n * blk, blk), blk)], p)

    def softmax_step(s, m, l, acc, pv):
        m_new = jnp.maximum(m, jnp.max(s, axis=0, keepdims=True))
        alpha = jnp.exp(m - m_new)
        p = jnp.exp(s - m_new)
        l = alpha * l + jnp.sum(p, axis=0, keepdims=True)
        return m_new, l, alpha * (acc + pv), p.astype(BF16)

    def past_block(n, carry):
        out = []
        for j in range(heads):
            m, l, acc, s, p_prev = carry[j]
            pv = values(j, jnp.maximum(n - 1, 0), p_prev)
            s_next = scores(j, n + 1)
            bias = jnp.where(sel_ref[j, pl.ds(n, 1), :] > 0.5, 0.0, NEG)
            m, l, acc, p = softmax_step(s + bias, m, l, acc, pv)
            out.append((m, l, acc, s_next, p))
        return tuple(out)

    init = tuple((jnp.full((1, blk), NEG, F32), jnp.zeros((1, blk), F32), jnp.zeros((dh, blk), F32),
                  scores(j, 0), jnp.zeros((blk, blk), BF16)) for j in range(heads))
    carry = lax.fori_loop(0, i, past_block, init)
    kpos = lax.broadcasted_iota(jnp.int32, (blk, blk), 0)
    qpos = lax.broadcasted_iota(jnp.int32, (blk, blk), 1)
    for j in range(heads):
        m, l, acc, s, p_prev = carry[j]
        pv = values(j, jnp.maximum(i - 1, 0), p_prev)
        m, l, acc, p = softmax_step(jnp.where(kpos <= qpos, s, NEG), m, l, acc, pv)
        acc = acc + values(j, i, p)
        o_ref[:, hs[j]] = (acc / l).T


def _moba_prompt(proj3, heads_per_step=2):
    bsz, t, _ = proj3.shape
    nb = t // MOBA_BLOCK
    km = _block_key_means(proj3)
    w = heads_per_step * MOBA_DH
    hq, hk, hv = COL_QC // w, COL_KC // w, COL_VC // w
    once = pl.Buffered(1)
    return pl.pallas_call(
        _moba_prompt_kernel,
        grid=(bsz, MOBA_HEADS // heads_per_step, nb),
        in_specs=[pl.BlockSpec((None, MOBA_BLOCK, w), lambda b, h, i: (b, i, hq + h)),
                  pl.BlockSpec((None, t, w), lambda b, h, i: (b, 0, hk + h), pipeline_mode=once),
                  pl.BlockSpec((None, t, w), lambda b, h, i: (b, 0, hv + h), pipeline_mode=once),
                  pl.BlockSpec((None, nb, w), lambda b, h, i: (b, 0, h))],
        out_specs=pl.BlockSpec((None, MOBA_BLOCK, w), lambda b, h, i: (b, i, h)),
        out_shape=jax.ShapeDtypeStruct((bsz, t, MOBA_WIDTH), F32),
        scratch_shapes=[pltpu.VMEM((t, w), BF16), pltpu.VMEM((w, t), BF16),
                        pltpu.VMEM((heads_per_step, nb, MOBA_BLOCK), F32)],
        compiler_params=_cparams("parallel", "parallel", "arbitrary"),
        name="moba_prompt",
    )(proj3, proj3, proj3, km)


def _moba_sample_kernel(pt_ref, q_ref, kn_ref, vn_ref, seg_ref, *refs, n_pages):
    k_pages = refs[:n_pages]
    v_pages = refs[n_pages:2 * n_pages]
    o_ref = refs[2 * n_pages]
    s_ref, km_ref = refs[2 * n_pages + 1], refs[2 * n_pages + 2]
    nh = MOBA_HEADS
    prow = PAGE_SIZE * nh
    pages_per_block = MOBA_BLOCK // PAGE_SIZE
    n_blocks = n_pages // pages_per_block
    sub = 8
    q = q_ref[...]
    nq = q.shape[0]
    q_head = lax.broadcasted_iota(jnp.int32, (nq, 1), 0) % nh
    q_tok = lax.broadcasted_iota(jnp.int32, (nq, 1), 0) // nh
    for n in range(n_blocks):
        tot = jnp.sum(k_pages[n * pages_per_block][...].reshape(prow // sub, sub, MOBA_DH), axis=0)
        for j in range(1, pages_per_block):
            tot = tot + jnp.sum(k_pages[n * pages_per_block + j][...].reshape(prow // sub, sub, MOBA_DH), axis=0)
        km_ref[n * sub:(n + 1) * sub, :] = tot
    g_all = _dot_nt(q, km_ref[...], precision=HIGHEST)
    g_lane = lax.broadcasted_iota(jnp.int32, g_all.shape, 1)
    g_own = jnp.where(g_lane % nh == q_head, g_all, 0.0)
    gscore = _dot(g_own, seg_ref[...], precision=HIGHEST) * (1.0 / MOBA_BLOCK)
    blk_idx = lax.broadcasted_iota(jnp.int32, gscore.shape, 1)
    sel = _select_top_blocks(gscore, n_blocks, blk_idx, 1)
    qb = (q * (MOBA_DH ** -0.5)).astype(BF16)
    own_head = lax.broadcasted_iota(jnp.int32, (nq, prow), 1) % nh == q_head
    for j in range(n_pages):
        s = _dot_nt(qb, k_pages[j][...].astype(BF16))
        nblk = j // pages_per_block
        keep = jnp.where(own_head, sel[:, nblk:nblk + 1], 0.0) > 0.5
        s_ref[:, j * prow:(j + 1) * prow] = jnp.where(keep, s, NEG)
    s_new = _dot_nt(qb, kn_ref[...].astype(BF16))
    coln = lax.broadcasted_iota(jnp.int32, s_new.shape, 1)
    ok_new = jnp.where(coln % nh == q_head, coln // nh, nq) <= q_tok
    s_new = jnp.where(ok_new, s_new, NEG)
    s_past = s_ref[...]
    m = jnp.maximum(jnp.max(s_past, axis=-1, keepdims=True), jnp.max(s_new, axis=-1, keepdims=True))
    p_past = jnp.exp(s_past - m)
    p_new = jnp.exp(s_new - m)
    l = jnp.sum(p_past, axis=-1, keepdims=True) + jnp.sum(p_new, axis=-1, keepdims=True)
    o = _dot(p_new.astype(BF16), vn_ref[...].astype(BF16))
    for j in range(n_pages):
        o = o + _dot(p_past[:, j * prow:(j + 1) * prow].astype(BF16), v_pages[j][...].astype(BF16))
    o_ref[...] = o / l


def _moba_sample(q, k_new, v_new, cache_k, cache_v, page_table, layer):
    bsz, nq, dh = q.shape
    depth, n_pool = cache_k.shape[:2]
    n_pages = page_table.shape[1]
    prow = PAGE_SIZE * MOBA_HEADS
    ck = cache_k.reshape(depth, n_pool, prow, dh)
    cv = cache_v.reshape(depth, n_pool, prow, dh)
    n_blocks = n_pages * PAGE_SIZE // MOBA_BLOCK
    seg = (jnp.arange(n_blocks * 8)[:, None] // 8 == jnp.arange(n_blocks)[None, :]).astype(F32)

    def page_spec(j):
        return pl.BlockSpec((None, None, prow, dh), lambda b, pt: (layer, pt[b * n_pages + j], 0, 0))

    grid_spec = pltpu.PrefetchScalarGridSpec(
        num_scalar_prefetch=1,
        grid=(bsz,),
        in_specs=([pl.BlockSpec((None, nq, dh), lambda b, pt: (b, 0, 0))] * 3
                  + [pl.BlockSpec(seg.shape, lambda b, pt: (0, 0))]
                  + [page_spec(j) for j in range(n_pages)] * 2),
        out_specs=pl.BlockSpec((None, nq, dh), lambda b, pt: (b, 0, 0)),
        scratch_shapes=[pltpu.VMEM((nq, n_pages * prow), F32), pltpu.VMEM((n_blocks * 8, dh), F32)],
    )
    return pl.pallas_call(
        functools.partial(_moba_sample_kernel, n_pages=n_pages),
        grid_spec=grid_spec,
        out_shape=jax.ShapeDtypeStruct((bsz, nq, dh), F32),
        compiler_params=_cparams("arbitrary"),
        name="moba_sample",
    )(page_table.reshape(-1), q, k_new, v_new, seg, *([ck] * n_pages), *([cv] * n_pages))


def _mix_kernel(ya_ref, ob_ref, oc_ref, od_ref, gl_ref, x_ref, g1_ref, wglu_ref, wb_ref, wc_ref, wd_ref, wo_ref,
                lng_ref, lnb_ref, o_ref):
    d = D_MODEL
    z = _dot(ya_ref[...].astype(BF16), wglu_ref[...])
    merged = jax.nn.sigmoid(gl_ref[:, 0:d]) * (z[:, 0:d] * jax.nn.sigmoid(z[:, d:2 * d]))
    merged += jax.nn.sigmoid(gl_ref[:, d:2 * d]) * _dot(ob_ref[...].astype(BF16), wb_ref[...])
    merged += jax.nn.sigmoid(gl_ref[:, 2 * d:3 * d]) * _dot(oc_ref[...].astype(BF16), wc_ref[...])
    merged += jax.nn.sigmoid(gl_ref[:, 3 * d:4 * d]) * _dot(od_ref[...].astype(BF16), wd_ref[...])
    mix = _dot(merged.astype(BF16), wo_ref[...])
    o_ref[...] = _layer_norm(DEEPNORM_ALPHA * x_ref[...] + g1_ref[...] * mix, lng_ref[...], lnb_ref[...])


def _mix(ya, ob, oc, od, proj, x, mod, rows_per_seq, wl, tm):
    n, d = x.shape
    w512 = SSM_WIDTH
    tok = lambda width, col=0: pl.BlockSpec((tm, width), lambda i: (i, col))
    const = lambda r, c: pl.BlockSpec((r, c), lambda i: (0, 0))
    return pl.pallas_call(
        _mix_kernel,
        grid=(n // tm,),
        in_specs=[tok(w512), tok(w512), tok(w512), tok(w512), tok(N_BRANCH * d, COL_GATES), tok(d),
                  _mod_spec(mod, tm, rows_per_seq, 2),
                  const(w512, 2 * d), const(w512, d), const(w512, d), const(w512, d), const(d, d),
                  const(1, d), const(1, d)],
        out_specs=tok(d),
        out_shape=jax.ShapeDtypeStruct((n, d), F32),
        compiler_params=_cparams("parallel"),
    )(ya, ob, oc, od, proj, x, mod, wl["w_glu"], wl["w_hgrn"], wl["w_moba"], wl["w_gla"], wl["w_out"],
      wl["ln1_g"], wl["ln1_b"])


def _first_max(vals, lane, width):
    m = jnp.max(vals, axis=-1, keepdims=True)
    first = jnp.min(jnp.where(vals == m, lane, width), axis=-1, keepdims=True)
    return m, first


def _route(h2, wr_ref, br_ref):
    logits = _dot(h2, wr_ref[...], precision=HIGHEST)
    lane = lax.broadcasted_iota(jnp.int32, logits.shape, 1)
    width = logits.shape[-1]
    scores = jax.nn.sigmoid(logits)
    biased = jnp.where(lane < N_EXPERTS, scores + br_ref[...], -jnp.inf)
    group = lane // EXPERTS_PER_GROUP
    best = jnp.zeros((logits.shape[0], 1), jnp.int32)
    best_score = None
    for gidx in range(N_EXPERT_GROUPS):
        vals = jnp.where(group == gidx, biased, -jnp.inf)
        m1, i1 = _first_max(vals, lane, width)
        m2 = jnp.max(jnp.where(lane == i1, -jnp.inf, vals), axis=-1, keepdims=True)
        gs = m1 + m2
        if best_score is None:
            best_score = gs
        else:
            better = gs > best_score
            best = jnp.where(better, gidx, best)
            best_score = jnp.where(better, gs, best_score)
    masked = jnp.where(group == best, biased, -jnp.inf)
    _, i1 = _first_max(masked, lane, width)
    _, i2 = _first_max(jnp.where(lane == i1, -jnp.inf, masked), lane, width)
    w1 = jnp.sum(jnp.where(lane == i1, scores, 0.0), axis=-1, keepdims=True)
    w2 = jnp.sum(jnp.where(lane == i2, scores, 0.0), axis=-1, keepdims=True)
    tot = w1 + w2
    return jnp.where(lane == i1, w1 / tot, 0.0) + jnp.where(lane == i2, w2 / tot, 0.0)


def _moe_kernel(x_ref, sh_ref, sc_ref, g2_ref, wr_ref, br_ref, wg_ref, wu_ref, wd_ref, lng_ref, lnb_ref, o_ref,
                h_ref, comb_ref, acc_ref, *, experts_per_step):
    e = pl.program_id(1)

    @pl.when(e == 0)
    def _():
        h2 = x_ref[...] * (1.0 + sc_ref[...]) + sh_ref[...]
        h_ref[...] = h2.astype(BF16)
        comb_ref[...] = _route(h2, wr_ref, br_ref)
        acc_ref[...] = jnp.zeros(acc_ref.shape, F32)

    hb = h_ref[...]
    lane = lax.broadcasted_iota(jnp.int32, comb_ref.shape, 1)
    for j in range(experts_per_step):
        eid = e * experts_per_step + j
        cw = jnp.sum(jnp.where(lane == eid, comb_ref[...], 0.0), axis=-1, keepdims=True)
        hid = _silu(_dot(hb, wg_ref[j])) * _dot(hb, wu_ref[j]) * cw
        acc_ref[...] += _dot(hid.astype(BF16), wd_ref[j])

    @pl.when(e == pl.num_programs(1) - 1)
    def _():
        o_ref[...] = _layer_norm(DEEPNORM_ALPHA * x_ref[...] + g2_ref[...] * acc_ref[...], lng_ref[...], lnb_ref[...])


def _moe(x, mod, rows_per_seq, wl, w_router, b_router, tm, experts_per_step=4):
    n, d = x.shape
    nh = EXPERT_HIDDEN
    tok = pl.BlockSpec((tm, d), lambda i, e: (i, 0))
    const = lambda r, c: pl.BlockSpec((r, c), lambda i, e: (0, 0))
    return pl.pallas_call(
        functools.partial(_moe_kernel, experts_per_step=experts_per_step),
        grid=(n // tm, N_EXPERTS // experts_per_step),
        in_specs=[tok, _mod_spec(mod, tm, rows_per_seq, 3), _mod_spec(mod, tm, rows_per_seq, 4),
                  _mod_spec(mod, tm, rows_per_seq, 5), const(d, LANES), const(1, LANES),
                  pl.BlockSpec((experts_per_step, d, nh), lambda i, e: (e, 0, 0)),
                  pl.BlockSpec((experts_per_step, d, nh), lambda i, e: (e, 0, 0)),
                  pl.BlockSpec((experts_per_step, nh, d), lambda i, e: (e, 0, 0)),
                  const(1, d), const(1, d)],
        out_specs=tok,
        out_shape=jax.ShapeDtypeStruct((n, d), F32),
        scratch_shapes=[pltpu.VMEM((tm, d), BF16), pltpu.VMEM((tm, LANES), F32), pltpu.VMEM((tm, d), F32)],
        compiler_params=_cparams("parallel", "arbitrary"),
    )(x, mod, mod, mod, w_router, b_router, wl["moe_gate"], wl["moe_up"], wl["moe_down"], wl["ln2_g"], wl["ln2_b"])


def _token_mixers_prompt(proj, k_rows, v_rows, bsz, t, wl):
    proj3 = proj.reshape(bsz, t, IN_COLS_PAD)
    ya, ssm_re, ssm_im = _ssm_prompt(proj3, wl["ssm"])
    ob, hgrn_s = _lin_rec("hgrn", proj3, (COL_QB, COL_FB, COL_IB, COL_GB), wl["hgrn_extras"], wl["hgrn_norm"],
                          None, 0, REC_CHUNK_PROMPT)
    od, gla_s = _lin_rec("gla", proj3, (COL_QD, COL_KD, COL_VD, COL_GD), wl["gla_extras"], wl["gla_norm"],
                         None, 0, REC_CHUNK_PROMPT)
    oc = _moba_prompt(proj3)
    n = bsz * t
    k_new = k_rows.reshape(bsz, t, MOBA_HEADS, MOBA_DH)
    v_new = v_rows.reshape(bsz, t, MOBA_HEADS, MOBA_DH)
    return (ya.reshape(n, -1), ob.reshape(n, -1), oc.reshape(n, -1), od.reshape(n, -1),
            (k_new, v_new, ssm_re, ssm_im, hgrn_s, gla_s))


def _token_mixers_sample(proj, k_rows, v_rows, bsz, t, wl, layer, cache_k, cache_v, page_table, st_re, st_im,
                         st_hgrn, st_gla):
    proj3 = proj.reshape(bsz, t, IN_COLS_PAD)
    u_tm = proj3[:, :, COL_UA:COL_UA + SSM_WIDTH].transpose(1, 0, 2)
    ns = SSM_GROUPS * SSM_STATE
    ya_tm, xr, xi = _ssm_sample(u_tm, st_re[layer].reshape(bsz, ns), st_im[layer].reshape(bsz, ns), wl["ssm"])
    ya = ya_tm.transpose(1, 0, 2)
    seqs = math.gcd(bsz, REC_SAMPLE_SEQS)
    ob, hgrn_s = _lin_rec("hgrn", proj3, (COL_QB, COL_FB, COL_IB, COL_GB), wl["hgrn_extras"], wl["hgrn_norm"],
                          st_hgrn, layer, SUB_CHUNK, seqs)
    od, gla_s = _lin_rec("gla", proj3, (COL_QD, COL_KD, COL_VD, COL_GD), wl["gla_extras"], wl["gla_norm"],
                         st_gla, layer, SUB_CHUNK, seqs)
    nq = t * MOBA_HEADS
    q = proj3[:, :, COL_QC:COL_QC + MOBA_WIDTH].reshape(bsz, nq, MOBA_DH)
    oc = _moba_sample(q, k_rows.reshape(bsz, nq, MOBA_DH), v_rows.reshape(bsz, nq, MOBA_DH), cache_k, cache_v,
                      page_table, layer)
    n = bsz * t
    return (ya.reshape(n, -1), ob.reshape(n, -1), oc.reshape(n, -1), od.reshape(n, -1),
            (k_rows.reshape(bsz, t, MOBA_HEADS, MOBA_DH), v_rows.reshape(bsz, t, MOBA_HEADS, MOBA_DH),
             xr.reshape(bsz, SSM_GROUPS, SSM_STATE), xi.reshape(bsz, SSM_GROUPS, SSM_STATE), hgrn_s, gla_s))


def _layer_weights(l, w_in_p, ssm_tabs, lower_bounds, gla_w_gk2, gla_b_gk, hgrn_norm, gla_norm, ssm_w_glu,
                   hgrn_w_proj, moba_w_proj, gla_w_proj, w_out, ln1_g, ln1_b, ln2_g, ln2_b, moe_w_gate, moe_w_up,
                   moe_w_down):
    lb = lower_bounds[l].reshape(1, -1)
    wgk = jnp.pad(gla_w_gk2[l], ((0, LANES - GLA_RANK), (0, 0))).astype(BF16)
    row = lambda a: a.reshape(1, -1)
    return dict(
        w_in=w_in_p[l], ssm={k: v[l] for k, v in ssm_tabs.items()},
        hgrn_extras=(jnp.log(lb), jnp.log1p(-lb), 1.0 - lb), hgrn_norm=hgrn_norm[l],
        gla_extras=(wgk, row(gla_b_gk[l])), gla_norm=gla_norm[l],
        w_glu=ssm_w_glu[l].astype(BF16), w_hgrn=hgrn_w_proj[l].astype(BF16), w_moba=moba_w_proj[l].astype(BF16),
        w_gla=gla_w_proj[l].astype(BF16), w_out=w_out[l].astype(BF16),
        ln1_g=row(ln1_g[l]), ln1_b=row(ln1_b[l]), ln2_g=row(ln2_g[l]), ln2_b=row(ln2_b[l]),
        moe_gate=moe_w_gate[l].astype(BF16), moe_up=moe_w_up[l].astype(BF16), moe_down=moe_w_down[l].astype(BF16))


def kernel(x_prompt, x_sample, cache_k, cache_v, state_ssm_re, state_ssm_im, state_hgrn, state_gla, page_table,
           c_prompt, c_sample, ln_in_g, ln_in_b, w_ada, b_ada, w_in, ssm_lam_re, ssm_lam_im, ssm_log_step,
           ssm_b_re, ssm_b_im, ssm_c_re, ssm_c_im, ssm_d, ssm_w_glu, hgrn_lb, hgrn_norm, hgrn_w_proj, moba_w_proj,
           gla_w_gk2, gla_b_gk, gla_norm, gla_w_proj, w_out, ln1_g, ln1_b, ln2_g, ln2_b, w_router, b_router,
           moe_w_gate, moe_w_up, moe_w_down):
    bp, tp, d = x_prompt.shape
    bs, ts, _ = x_sample.shape
    depth = w_in.shape[0]
    n_pool = cache_k.shape[1]
    np_tok, ns_tok = bp * tp, bs * ts
    tm_p = min(1024, tp)
    tm_s = min(512, ns_tok)

    lb_cum = jnp.cumsum(jax.nn.softmax(hgrn_lb.astype(F32), axis=0), axis=0)
    lower_bounds = lb_cum - lb_cum[0:1]
    w_in_p = jnp.concatenate(
        [w_in[:, :, REF_GATE_START:IN_COLS_REF], w_in[:, :, :REF_RD_START], w_in[:, :, REF_RD_START:REF_GATE_START],
         jnp.zeros((depth, d, IN_COLS_PAD - IN_COLS_REF), w_in.dtype)], axis=-1).astype(BF16)
    n_scan = max(tp // SSM_CHUNK - 1, 0).bit_length()
    ssm_tabs = jax.vmap(functools.partial(_ssm_tables, n_scan=n_scan))(
        ssm_lam_re, ssm_lam_im, ssm_log_step, ssm_b_re, ssm_b_im, ssm_c_re, ssm_c_im, ssm_d)
    w_router_p = jnp.pad(w_router, ((0, 0), (0, LANES - N_EXPERTS)))
    b_router_p = jnp.pad(b_router, (0, LANES - N_EXPERTS)).reshape(1, LANES)

    nc = bs + bp
    nc_pad = -(-nc // 8) * 8
    c_all = jnp.pad(jnp.concatenate([c_sample, c_prompt], axis=0), ((0, nc_pad - nc), (0, 0)))
    mod_all = _ada_all(c_all, w_ada, b_ada)

    xp = _ln_rows(x_prompt.reshape(np_tok, d), ln_in_g, ln_in_b, tm_p)
    xs = _ln_rows(x_sample.reshape(ns_tok, d), ln_in_g, ln_in_b, tm_s)

    outs_p, outs_s = [], []
    for l in range(depth):
        wl = _layer_weights(l, w_in_p, ssm_tabs, lower_bounds, gla_w_gk2, gla_b_gk, hgrn_norm, gla_norm, ssm_w_glu,
                            hgrn_w_proj, moba_w_proj, gla_w_proj, w_out, ln1_g, ln1_b, ln2_g, ln2_b, moe_w_gate,
                            moe_w_up, moe_w_down)
        mod_p = mod_all[l, bs:bs + bp].reshape(bp, 1, 6 * d)
        mod_s = jnp.repeat(mod_all[l, :bs], ts, axis=0)

        proj_p, k_rows, v_rows = _in_proj(xp, mod_p, tp, wl["w_in"], tm_p)
        ya, ob, oc, od, st_p = _token_mixers_prompt(proj_p, k_rows, v_rows, bp, tp, wl)
        xp = _mix(ya, ob, oc, od, proj_p, xp, mod_p, tp, wl, min(256, tp))
        xp = _moe(xp, mod_p, tp, wl, w_router_p, b_router_p, tm_p)
        outs_p.append(st_p)

        proj_s, k_rows, v_rows = _in_proj(xs, mod_s, ts, wl["w_in"], tm_s)
        ya, ob, oc, od, st_s = _token_mixers_sample(proj_s, k_rows, v_rows, bs, ts, wl, l, cache_k, cache_v,
                                                    page_table, state_ssm_re, state_ssm_im, state_hgrn, state_gla)
        xs = _mix(ya, ob, oc, od, proj_s, xs, mod_s, ts, wl, min(256, ns_tok))
        xs = _moe(xs, mod_s, ts, wl, w_router_p, b_router_p, tm_s)
        outs_s.append(st_s)

    stack = lambda outs, idx: jnp.stack([o[idx] for o in outs])
    return (xp.reshape(bp, tp, d), xs.reshape(bs, ts, d),
            stack(outs_p, 0), stack(outs_p, 1), stack(outs_s, 0), stack(outs_s, 1),
            stack(outs_p, 2), stack(outs_p, 3), stack(outs_s, 2), stack(outs_s, 3),
            stack(outs_p, 4), stack(outs_s, 4), stack(outs_p, 5), stack(outs_s, 5))
```

```python
import functools
import math

import jax
import jax.numpy as jnp
from jax import lax
from jax.experimental import pallas as pl
from jax.experimental.pallas import tpu as pltpu

F32 = jnp.float32
BF16 = jnp.bfloat16
HIGHEST = lax.Precision.HIGHEST

D_MODEL = 1024
DEPTH = 4
PAGE_SIZE = 128
SSM_WIDTH = 512
SSM_GROUP = 16
SSM_GROUPS = 32
SSM_STATE = 64
SSM_MAX_RE = -1e-4
SSM_CHUNK = 16
SSM_QUAD_GROUPS = 8
HGRN_HEADS = 4
HGRN_DK = 128
HGRN_DV = 128
MOBA_HEADS = 4
MOBA_DH = 128
MOBA_WIDTH = 512
MOBA_BLOCK = 256
MOBA_TOPK = 3
GLA_HEADS = 4
GLA_DK = 64
GLA_DV = 128
GLA_RANK = 16
GLA_TAU = 16.0
N_BRANCH = 4
N_EXPERTS = 16
N_EXPERT_GROUPS = 4
EXPERTS_PER_GROUP = 4
EXPERT_HIDDEN = 256
DEEPNORM_ALPHA = (2 * DEPTH) ** 0.25
NORM_EPS = 1e-5

LANES = 128
SUB_CHUNK = 16
REC_CHUNK_PROMPT = 128
REC_SAMPLE_SEQS = 4
VMEM_LIMIT = 56 * 1024 * 1024
NEG = -1e30

COL_GATES = 0
COL_UA = 4096
COL_QB = 4608
COL_FB = 5120
COL_IB = 5632
COL_GB = 6144
COL_QC = 6656
COL_KC = 7168
COL_VC = 7680
COL_QD = 8192
COL_KD = 8448
COL_VD = 8704
COL_GD = 9216
COL_RD = 9728
IN_COLS_PAD = 10240
IN_COLS_REF = 9744
REF_GATE_START = 5648
REF_RD_START = 5632


def _cparams(*sem):
    return pltpu.CompilerParams(dimension_semantics=sem, vmem_limit_bytes=VMEM_LIMIT)


def _silu(x):
    return x * jax.nn.sigmoid(x)


def _log_sigmoid(x):
    return jnp.minimum(x, 0.0) - jnp.log1p(jnp.exp(-jnp.abs(x)))


def _dot(a, b, **kw):
    return jnp.dot(a, b, preferred_element_type=F32, **kw)


def _dot_nt(a, b, **kw):
    return lax.dot_general(a, b, (((1,), (1,)), ((), ())), preferred_element_type=F32, **kw)


def _dot_tn(a, b, **kw):
    return lax.dot_general(a, b, (((0,), (0,)), ((), ())), preferred_element_type=F32, **kw)


def _layer_norm(x, g, b):
    mu = jnp.mean(x, axis=-1, keepdims=True)
    xc = x - mu
    var = jnp.mean(xc * xc, axis=-1, keepdims=True)
    return xc * lax.rsqrt(var + NORM_EPS) * g + b


def _ln_kernel(x_ref, g_ref, b_ref, o_ref):
    o_ref[...] = _layer_norm(x_ref[...], g_ref[...], b_ref[...])


def _ln_rows(x, g, b, tm):
    n, d = x.shape
    return pl.pallas_call(
        _ln_kernel,
        grid=(n // tm,),
        in_specs=[pl.BlockSpec((tm, d), lambda i: (i, 0)),
                  pl.BlockSpec((1, d), lambda i: (0, 0)),
                  pl.BlockSpec((1, d), lambda i: (0, 0))],
        out_specs=pl.BlockSpec((tm, d), lambda i: (i, 0)),
        out_shape=jax.ShapeDtypeStruct((n, d), F32),
        compiler_params=_cparams("parallel"),
    )(x, g.reshape(1, d), b.reshape(1, d))


def _ada_kernel(c_ref, w_ref, b_ref, o_ref):
    h = _silu(c_ref[...]).astype(BF16)
    o_ref[...] = _dot(h, w_ref[...].astype(BF16)) + b_ref[...]


def _ada_all(c, w_ada, b_ada, tn=1536):
    nb, d = c.shape
    depth, _, n6 = w_ada.shape
    return pl.pallas_call(
        _ada_kernel,
        grid=(depth, n6 // tn),
        in_specs=[pl.BlockSpec((nb, d), lambda l, j: (0, 0)),
                  pl.BlockSpec((None, d, tn), lambda l, j: (l, 0, j)),
                  pl.BlockSpec((None, 1, tn), lambda l, j: (l, 0, j))],
        out_specs=pl.BlockSpec((None, nb, tn), lambda l, j: (l, 0, j)),
        out_shape=jax.ShapeDtypeStruct((depth, nb, n6), F32),
        compiler_params=_cparams("parallel", "parallel"),
    )(c, w_ada, b_ada.reshape(depth, 1, n6))


def _mod_spec(mod, tm, rows_per_seq, chunk):
    if mod.ndim == 3:
        tiles_per_seq = rows_per_seq // tm
        return pl.BlockSpec((None, 1, D_MODEL), lambda i, *_: (i // tiles_per_seq, 0, chunk))
    return pl.BlockSpec((tm, D_MODEL), lambda i, *_: (i, chunk))


def _inproj_kernel(x_ref, sh_ref, sc_ref, w_ref, o_ref, k_ref, v_ref, h_ref, *, kv_tile):
    j = pl.program_id(1)

    @pl.when(j == 0)
    def _():
        h_ref[...] = (x_ref[...] * (1.0 + sc_ref[...]) + sh_ref[...]).astype(BF16)

    acc = _dot(h_ref[...], w_ref[...])
    o_ref[...] = acc

    @pl.when(j == kv_tile)
    def _():
        tm = acc.shape[0]
        for h in range(MOBA_HEADS):
            k_ref[pl.ds(h, tm, stride=MOBA_HEADS), :] = acc[:, h * MOBA_DH:(h + 1) * MOBA_DH]
            v_ref[pl.ds(h, tm, stride=MOBA_HEADS), :] = acc[:, MOBA_WIDTH + h * MOBA_DH:MOBA_WIDTH + (h + 1) * MOBA_DH]


def _in_proj(x, mod, rows_per_seq, w, tm, tn=1024):
    n, d = x.shape
    ncol = w.shape[1]
    assert COL_KC % tn == 0 and COL_VC == COL_KC + MOBA_WIDTH and tn == 2 * MOBA_WIDTH
    kv_spec = pl.BlockSpec((tm * MOBA_HEADS, MOBA_DH), lambda i, j: (i, 0))
    kv_shape = jax.ShapeDtypeStruct((n * MOBA_HEADS, MOBA_DH), F32)
    return pl.pallas_call(
        functools.partial(_inproj_kernel, kv_tile=COL_KC // tn),
        grid=(n // tm, ncol // tn),
        in_specs=[pl.BlockSpec((tm, d), lambda i, j: (i, 0)),
                  _mod_spec(mod, tm, rows_per_seq, 0),
                  _mod_spec(mod, tm, rows_per_seq, 1),
                  pl.BlockSpec((d, tn), lambda i, j: (0, j))],
        out_specs=[pl.BlockSpec((tm, tn), lambda i, j: (i, j)), kv_spec, kv_spec],
        out_shape=[jax.ShapeDtypeStruct((n, ncol), F32), kv_shape, kv_shape],
        scratch_shapes=[pltpu.VMEM((tm, d), BF16)],
        compiler_params=_cparams("parallel", "arbitrary"),
        name="in_proj",
    )(x, mod, mod, w)


def _ssm_prompt_kernel(u_ref, toep_ref, inre_ref, inim_ref, outre_ref, outim_ref, d_ref, are_ref, aim_ref,
                       y_ref, sre_ref, sim_ref, ucat_ref, *, n_steps):
    nrow = ucat_ref.shape[0]
    for tau in range(SSM_CHUNK):
        ucat_ref[:, tau * LANES:(tau + 1) * LANES] = u_ref[pl.ds(tau, nrow, stride=SSM_CHUNK), :].astype(BF16)
    ucat = ucat_ref[...]
    xr = _dot(ucat, inre_ref[...])
    xi = _dot(ucat, inim_ref[...])
    row = lax.broadcasted_iota(jnp.int32, xr.shape, 0)
    for k in range(n_steps):
        dist = 1 << k
        ar = are_ref[k:k + 1, :]
        ai = aim_ref[k:k + 1, :]
        keep = row >= dist
        sr = jnp.where(keep, pltpu.roll(xr, dist, 0), 0.0)
        si = jnp.where(keep, pltpu.roll(xi, dist, 0), 0.0)
        xr, xi = xr + (ar * sr - ai * si), xi + (ar * si + ai * sr)
    keep = row >= 1
    pr = jnp.where(keep, pltpu.roll(xr, 1, 0), 0.0)
    pi = jnp.where(keep, pltpu.roll(xi, 1, 0), 0.0)
    out = (_dot(ucat, toep_ref[...]) + _dot(pr.astype(BF16), outre_ref[...])
           + _dot(pi.astype(BF16), outim_ref[...]))
    for tau in range(SSM_CHUNK):
        u_tau = u_ref[pl.ds(tau, nrow, stride=SSM_CHUNK), :]
        y_ref[pl.ds(tau, nrow, stride=SSM_CHUNK), :] = jax.nn.gelu(out[:, tau * LANES:(tau + 1) * LANES]
                                                                   + d_ref[...] * u_tau)
    sre_ref[...] = xr[nrow - 1:nrow, :]
    sim_ref[...] = xi[nrow - 1:nrow, :]


def _ssm_prompt(proj3, sp):
    bsz, t, _ = proj3.shape
    nrow = t // SSM_CHUNK
    n_steps = max(nrow - 1, 0).bit_length()
    nquad = SSM_WIDTH // LANES
    kw = SSM_CHUNK * LANES
    ns = SSM_QUAD_GROUPS * SSM_STATE
    once = pl.Buffered(1)
    y, sre, sim = pl.pallas_call(
        functools.partial(_ssm_prompt_kernel, n_steps=n_steps),
        grid=(nquad, bsz),
        in_specs=[pl.BlockSpec((None, t, LANES), lambda q, b: (b, 0, COL_UA // LANES + q)),
                  pl.BlockSpec((None, kw, kw), lambda q, b: (q, 0, 0), pipeline_mode=once),
                  pl.BlockSpec((None, kw, ns), lambda q, b: (q, 0, 0), pipeline_mode=once),
                  pl.BlockSpec((None, kw, ns), lambda q, b: (q, 0, 0), pipeline_mode=once),
                  pl.BlockSpec((None, ns, kw), lambda q, b: (q, 0, 0), pipeline_mode=once),
                  pl.BlockSpec((None, ns, kw), lambda q, b: (q, 0, 0), pipeline_mode=once),
                  pl.BlockSpec((1, LANES), lambda q, b: (0, q)),
                  pl.BlockSpec((None, sp["scan_re"].shape[1], ns), lambda q, b: (q, 0, 0)),
                  pl.BlockSpec((None, sp["scan_im"].shape[1], ns), lambda q, b: (q, 0, 0))],
        out_specs=[pl.BlockSpec((None, t, LANES), lambda q, b: (b, 0, q)),
                   pl.BlockSpec((None, None, 1, ns), lambda q, b: (b, q, 0, 0)),
                   pl.BlockSpec((None, None, 1, ns), lambda q, b: (b, q, 0, 0))],
        out_shape=[jax.ShapeDtypeStruct((bsz, t, SSM_WIDTH), F32),
                   jax.ShapeDtypeStruct((bsz, nquad, 1, ns), F32),
                   jax.ShapeDtypeStruct((bsz, nquad, 1, ns), F32)],
        scratch_shapes=[pltpu.VMEM((nrow, kw), BF16)],
        compiler_params=_cparams("arbitrary", "arbitrary"),
        name="ssm_prompt",
    )(proj3, sp["toep"], sp["in_re"], sp["in_im"], sp["out_re"], sp["out_im"], sp["d_row"], sp["scan_re"],
      sp["scan_im"])
    return y, sre.reshape(bsz, SSM_GROUPS, SSM_STATE), sim.reshape(bsz, SSM_GROUPS, SSM_STATE)


def _ssm_sample_kernel(u_ref, x0r_ref, x0i_ref, bbr_ref, bbi_ref, cr_ref, ci_ref, ar_ref, ai_ref, d_ref,
                       y_ref, xr_ref, xi_ref):
    xr = x0r_ref[...]
    xi = x0i_ref[...]
    ar = ar_ref[...]
    ai = ai_ref[...]
    for t in range(u_ref.shape[0]):
        u = u_ref[t]
        ub = u.astype(BF16)
        xr, xi = (ar * xr - ai * xi + _dot(ub, bbr_ref[...]),
                  ar * xi + ai * xr + _dot(ub, bbi_ref[...]))
        y = _dot(xr.astype(BF16), cr_ref[...]) - _dot(xi.astype(BF16), ci_ref[...]) + d_ref[...] * u
        y_ref[t] = jax.nn.gelu(y)
    xr_ref[...] = xr
    xi_ref[...] = xi


def _ssm_sample(u_tm, x0r, x0i, sp):
    t, bsz, w = u_tm.shape
    ns = SSM_GROUPS * SSM_STATE
    full = lambda *shape: pl.BlockSpec(shape, lambda i: (0,) * len(shape))
    return pl.pallas_call(
        _ssm_sample_kernel,
        grid=(1,),
        in_specs=[full(t, bsz, w), full(bsz, ns), full(bsz, ns), full(w, ns), full(w, ns), full(ns, w), full(ns, w),
                  full(1, ns), full(1, ns), full(1, w)],
        out_specs=[full(t, bsz, w), full(bsz, ns), full(bsz, ns)],
        out_shape=[jax.ShapeDtypeStruct((t, bsz, w), F32), jax.ShapeDtypeStruct((bsz, ns), F32),
                   jax.ShapeDtypeStruct((bsz, ns), F32)],
        compiler_params=_cparams("arbitrary"),
    )(u_tm, x0r, x0i, sp["bb_re"], sp["bb_im"], sp["c_re"], sp["c_im"], sp["a_re"], sp["a_im"], sp["d_row"])


def _ssm_tables(lam_re, lam_im, log_step, b_re, b_im, c_re, c_im, d, n_scan):
    hp = dict(precision=HIGHEST)
    g, p, hch = SSM_GROUPS, SSM_STATE, SSM_GROUP
    step = jnp.exp(log_step)[:, None]
    lr = jnp.minimum(lam_re, SSM_MAX_RE)
    li = lam_im
    mag = jnp.exp(lr * step)
    ab_re = mag * jnp.cos(li * step)
    ab_im = mag * jnp.sin(li * step)
    den = lr * lr + li * li
    coef_re = ((ab_re - 1.0) * lr + ab_im * li) / den
    coef_im = (ab_im * lr - (ab_re - 1.0) * li) / den
    bb_re = coef_re[..., None] * b_re - coef_im[..., None] * b_im
    bb_im = coef_re[..., None] * b_im + coef_im[..., None] * b_re

    def cmul(xr, xi, yr, yi):
        return xr * yr - xi * yi, xr * yi + xi * yr

    pows_re, pows_im = [jnp.ones_like(ab_re)], [jnp.zeros_like(ab_im)]
    for _ in range(SSM_CHUNK):
        nr, ni = cmul(pows_re[-1], pows_im[-1], ab_re, ab_im)
        pows_re.append(nr)
        pows_im.append(ni)
    pw_re = jnp.stack(pows_re)
    pw_im = jnp.stack(pows_im)
    ajb_re = pw_re[:SSM_CHUNK, :, :, None] * bb_re - pw_im[:SSM_CHUNK, :, :, None] * bb_im
    ajb_im = pw_re[:SSM_CHUNK, :, :, None] * bb_im + pw_im[:SSM_CHUNK, :, :, None] * bb_re
    klag = (jnp.einsum("jgpi,gop->jgio", ajb_re, c_re, **hp) - jnp.einsum("jgpi,gop->jgio", ajb_im, c_im, **hp))
    rev = jnp.arange(SSM_CHUNK - 1, -1, -1)
    rev_re = pw_re[rev]
    rev_im = pw_im[rev]
    p_re = (rev_re[..., None] * bb_re - rev_im[..., None] * bb_im).transpose(1, 0, 3, 2).reshape(g, SSM_CHUNK * hch, p)
    p_im = (rev_re[..., None] * bb_im + rev_im[..., None] * bb_re).transpose(1, 0, 3, 2).reshape(g, SSM_CHUNK * hch, p)
    a1_re = pw_re[1:]
    a1_im = pw_im[1:]
    cr_t = c_re.transpose(0, 2, 1)
    ci_t = c_im.transpose(0, 2, 1)
    q_re = (cr_t[None] * a1_re[..., None] - ci_t[None] * a1_im[..., None]).transpose(1, 2, 0, 3).reshape(g, p, SSM_CHUNK * hch)
    q_im = (-(cr_t[None] * a1_im[..., None] + ci_t[None] * a1_re[..., None])).transpose(1, 2, 0, 3).reshape(g, p, SSM_CHUNK * hch)
    d_chunk = jnp.broadcast_to(d[:, None, :], (g, SSM_CHUNK, hch)).reshape(g, 1, SSM_CHUNK * hch)
    sc_re, sc_im = [pw_re[SSM_CHUNK]], [pw_im[SSM_CHUNK]]
    for _ in range(max(n_scan, 1) - 1):
        nr, ni = cmul(sc_re[-1], sc_im[-1], sc_re[-1], sc_im[-1])
        sc_re.append(nr)
        sc_im.append(ni)
    scan_re = jnp.stack(sc_re, axis=1)
    scan_im = jnp.stack(sc_im, axis=1)

    nq, gq = g // SSM_QUAD_GROUPS, SSM_QUAD_GROUPS
    kw = SSM_CHUNK * gq * hch
    qw = gq * hch

    def group_mask(rows_per_group, cols_per_group):
        r = jnp.arange(gq * rows_per_group)[:, None] // rows_per_group
        c = jnp.arange(gq * cols_per_group)[None, :] // cols_per_group
        return (r == c).astype(F32)

    def replicate(width):
        return (jnp.arange(width)[:, None] == jnp.arange(gq * width)[None, :] % width).astype(F32)

    klag_r = klag.reshape(SSM_CHUNK, nq, gq * hch, hch).transpose(1, 0, 2, 3)
    lag_blk = (jnp.einsum("njro,oc->njrc", klag_r, replicate(hch), **hp) * group_mask(hch, hch)).astype(BF16)
    lag_row = lag_blk.transpose(0, 2, 1, 3).reshape(nq, qw, kw)
    lag_row = jnp.concatenate([jnp.zeros_like(lag_row), lag_row], axis=-1)
    toep_q = jnp.concatenate([lag_row[:, :, (SSM_CHUNK - ts) * qw:(2 * SSM_CHUNK - ts) * qw]
                              for ts in range(SSM_CHUNK)], axis=1)

    def in_state(m):
        m = m.reshape(nq, gq, SSM_CHUNK, hch, p).transpose(0, 2, 1, 3, 4).reshape(nq, SSM_CHUNK, qw, p)
        m = jnp.einsum("ntrp,pc->ntrc", m, replicate(p), **hp) * group_mask(hch, p)
        return m.reshape(nq, kw, gq * p)

    def state_out(m):
        m = m.reshape(nq, gq * p, SSM_CHUNK, hch)
        m = jnp.einsum("nrto,oc->nrtc", m, replicate(hch), **hp) * group_mask(p, hch)[:, None, :]
        return m.reshape(nq, gq * p, kw)

    def scan_rows(m):
        return m.reshape(nq, gq, m.shape[1], p).transpose(0, 2, 1, 3).reshape(nq, m.shape[1], gq * p)

    def dense_diag(blocks):
        eye_g = jnp.eye(g, dtype=blocks.dtype)
        return (blocks[:, :, None, :] * eye_g[:, None, :, None]).reshape(g * blocks.shape[1], g * blocks.shape[2])

    return dict(
        toep=toep_q.astype(BF16),
        in_re=in_state(p_re).astype(BF16), in_im=in_state(p_im).astype(BF16),
        out_re=state_out(q_re).astype(BF16), out_im=state_out(q_im).astype(BF16),
        scan_re=scan_rows(scan_re), scan_im=scan_rows(scan_im),
        bb_re=dense_diag(bb_re.transpose(0, 2, 1)).astype(BF16), bb_im=dense_diag(bb_im.transpose(0, 2, 1)).astype(BF16),
        c_re=dense_diag(cr_t).astype(BF16), c_im=dense_diag(ci_t).astype(BF16),
        a_re=ab_re.reshape(1, g * p), a_im=ab_im.reshape(1, g * p), d_row=d.reshape(1, g * hch))


def _rec_levels(chunk):
    return [1 << p for p in range(chunk.bit_length() - 1)]


def _rec_tables(chunk):
    t = jnp.arange(chunk)[:, None]
    u = jnp.arange(chunk)[None, :]
    tabs = [u <= t]
    for half in _rec_levels(chunk):
        if half < 8:
            tabs.append(u <= (t // (2 * half)) * (2 * half) + half - 1)
    return jnp.concatenate(tabs, axis=0).astype(BF16)


def _linrec_kernel(*refs, mode, chunk, rows, heads, dk, dv, has_s0, seqs):
    it = iter(refs)
    q_ref, k_ref, v_ref, g_ref = next(it), next(it), next(it), next(it)
    if mode == "hgrn":
        loglb_ref, log1mlb_ref, onemlb_ref = next(it), next(it), next(it)
    else:
        r_ref, wgk_ref, bgk_ref = next(it), next(it), next(it)
    normw_ref = next(it)
    s0_ref = next(it) if has_s0 else None
    tstack_ref = next(it)
    o_ref, sfin_ref = next(it), next(it)
    st_ref = next(it)
    pad_refs = [next(it) for _ in range(5)] if rows != chunk else None

    c = pl.program_id(1)
    pack = LANES // dk
    ngrp = heads // pack
    gw = pack * dk

    @pl.when(c == 0)
    def _():
        for sg in range(seqs * ngrp):
            if has_s0:
                sq0, h0 = sg // ngrp, (sg % ngrp) * pack
                st_ref[sg] = jnp.concatenate([s0_ref[sq0, h0 + hh].T for hh in range(pack)], axis=1)
            else:
                st_ref[sg] = jnp.zeros((dv, gw), F32)

    def load(ref, slot, sq):
        if pad_refs is None:
            return ref[sq]
        buf = pad_refs[slot]
        buf[sq] = jnp.zeros(buf.shape[1:], F32)
        buf[sq, 0:rows, :] = ref[sq]
        return buf[sq]

    trow = lax.broadcasted_iota(jnp.int32, (chunk, chunk), 0)
    tcol = lax.broadcasted_iota(jnp.int32, (chunk, chunk), 1)
    dxor = trow ^ tcol
    rowi = lax.broadcasted_iota(jnp.int32, (chunk, 1), 0)
    valid = rowi < rows
    levels = _rec_levels(chunk)
    tstack = tstack_ref[...]

    lane_head = lax.broadcasted_iota(jnp.int32, (1, gw), 1) // dk

    def own_lanes(x, hh):
        return x if pack == 1 else jnp.where(lane_head == hh, x, 0.0)

    for sq, grp in [(a, b) for a in range(seqs) for b in range(ngrp)]:
        if grp == 0:
            q_all, k_all, v_all, g_all = load(q_ref, 0, sq), load(k_ref, 1, sq), load(v_ref, 2, sq), load(g_ref, 3, sq)
            if mode == "gla":
                gk_all = _dot(load(r_ref, 4, sq).astype(BF16), wgk_ref[...]) + bgk_ref[...]
        sg = sq * ngrp + grp
        ksl = slice(grp * gw, (grp + 1) * gw)
        if mode == "hgrn":
            q = _silu(q_all[:, ksl])
            z = k_all[:, ksl]
            la = loglb_ref[:, ksl]
            lc = log1mlb_ref[:, ksl] + _log_sigmoid(z)
            lf = jnp.maximum(la, lc) + jnp.log1p(jnp.exp(-jnp.abs(la - lc)))
            k = onemlb_ref[:, ksl] * jax.nn.sigmoid(-z)
        else:
            q = q_all[:, ksl] * (dk ** -0.5)
            k = k_all[:, ksl]
            lf = _log_sigmoid(gk_all[:, ksl]) / GLA_TAU
        if rows != chunk:
            lf = jnp.where(valid, lf, 0.0)
        hi = lf.astype(BF16)
        rest = lf - hi.astype(F32)
        mid = rest.astype(BF16)
        lo = (rest - mid.astype(F32)).astype(BF16)
        sums = _dot(tstack, hi) + _dot(tstack, mid) + _dot(tstack, lo)
        b = sums[0:chunk]
        st = st_ref[sg]
        stb = st.astype(BF16)
        qe = q * jnp.exp(b)
        b_last = b[chunk - 1:chunk, :]
        k_dec = k * jnp.exp(b_last - b)
        qk = q * k

        atts = [jnp.broadcast_to(jnp.sum(own_lanes(qk, hh), axis=-1, keepdims=True), (chunk, chunk))
                for hh in range(pack)]
        for li, half in enumerate(levels):
            if half < 8:
                beta = sums[(li + 1) * chunk:(li + 2) * chunk]
            else:
                nblk = chunk // (2 * half)
                b3 = b.reshape(nblk, 2 * half, gw)
                beta = jnp.broadcast_to(b3[:, half - 1:half, :], (nblk, 2 * half, gw)).reshape(chunk, gw)
            e = jnp.exp(-jnp.abs(b - beta))
            upper = (rowi & half) != 0
            qh = jnp.where(upper, q * e, 0.0)
            kh = jnp.where(upper, 0.0, k * e).astype(BF16)
            for hh in range(pack):
                atts[hh] = jnp.where(dxor >= half, _dot_nt(own_lanes(qh, hh).astype(BF16), kh), atts[hh])
        st_new = st * jnp.exp(b_last)
        for hh in range(pack):
            h = grp * pack + hh
            vsl = slice(h * dv, (h + 1) * dv)
            vb = v_all[:, vsl].astype(BF16)
            att = jnp.where(trow >= tcol, atts[hh], 0.0)
            o = _dot_nt(own_lanes(qe, hh).astype(BF16), stb) + _dot(att.astype(BF16), vb)
            o = o * lax.rsqrt(jnp.mean(o * o, axis=-1, keepdims=True) + NORM_EPS) * normw_ref[...]
            o = o * _silu(g_all[:, vsl])
            o_ref[sq, :, vsl] = o[0:rows, :]
            st_new = st_new + _dot_tn(vb, own_lanes(k_dec, hh).astype(BF16))
        st_ref[sg] = st_new

    @pl.when(c == pl.num_programs(1) - 1)
    def _():
        for sg in range(seqs * ngrp):
            for hh in range(pack):
                sfin_ref[sg // ngrp, (sg % ngrp) * pack + hh] = st_ref[sg][:, hh * dk:(hh + 1) * dk].T


def _lin_rec(mode, proj3, cols, extras, normw, s0, layer, chunk, seqs=1):
    bsz, t, _ = proj3.shape
    heads, dk, dv = (HGRN_HEADS, HGRN_DK, HGRN_DV) if mode == "hgrn" else (GLA_HEADS, GLA_DK, GLA_DV)
    rows = min(chunk, t)
    nchunk = max(t // chunk, 1)
    wk, wv = heads * dk, heads * dv
    assert bsz % seqs == 0

    def col_spec(width, col):
        return pl.BlockSpec((seqs, rows, width), lambda b, c: (b, c, col // width))

    in_specs = [col_spec(wk, cols[0]), col_spec(wk, cols[1]), col_spec(wv, cols[2]), col_spec(wv, cols[3])]
    args = [proj3, proj3, proj3, proj3]
    if mode == "hgrn":
        in_specs += [pl.BlockSpec((1, wk), lambda b, c: (0, 0))] * 3
        args += list(extras)
    else:
        in_specs += [col_spec(LANES, COL_RD), pl.BlockSpec((LANES, wk), lambda b, c: (0, 0)),
                     pl.BlockSpec((1, wk), lambda b, c: (0, 0))]
        args += [proj3] + list(extras)
    in_specs.append(pl.BlockSpec((1, dv), lambda b, c: (0, 0)))
    args.append(normw.reshape(1, dv))
    has_s0 = s0 is not None
    if has_s0:
        in_specs.append(pl.BlockSpec((None, seqs, heads, dk, dv), lambda b, c: (layer, b, 0, 0, 0)))
        args.append(s0)
    tables = _rec_tables(chunk)
    in_specs.append(pl.BlockSpec(tables.shape, lambda b, c: (0, 0)))
    args.append(tables)
    scratch = [pltpu.VMEM((seqs * heads * dk // LANES, dv, LANES), F32)]
    if rows != chunk:
        scratch += [pltpu.VMEM((seqs, chunk, wk), F32), pltpu.VMEM((seqs, chunk, wk), F32),
                    pltpu.VMEM((seqs, chunk, wv), F32), pltpu.VMEM((seqs, chunk, wv), F32),
                    pltpu.VMEM((seqs, chunk, LANES), F32)]
    return pl.pallas_call(
        functools.partial(_linrec_kernel, mode=mode, chunk=chunk, rows=rows, heads=heads, dk=dk, dv=dv,
                          has_s0=has_s0, seqs=seqs),
        grid=(bsz // seqs, nchunk),
        in_specs=in_specs,
        out_specs=[pl.BlockSpec((seqs, rows, wv), lambda b, c: (b, c, 0)),
                   pl.BlockSpec((seqs, heads, dk, dv), lambda b, c: (b, 0, 0, 0))],
        out_shape=[jax.ShapeDtypeStruct((bsz, t, wv), F32), jax.ShapeDtypeStruct((bsz, heads, dk, dv), F32)],
        scratch_shapes=scratch,
        compiler_params=_cparams("parallel", "arbitrary"),
        name=f"{mode}_rec",
    )(*args)


def _kmean_kernel(k_ref, o_ref):
    o_ref[...] = jnp.mean(k_ref[...], axis=0, keepdims=True)


def _block_key_means(proj3):
    bsz, t, _ = proj3.shape
    nb = t // MOBA_BLOCK
    out = pl.pallas_call(
        _kmean_kernel,
        grid=(bsz, nb),
        in_specs=[pl.BlockSpec((None, MOBA_BLOCK, MOBA_WIDTH), lambda b, n: (b, n, COL_KC // MOBA_WIDTH))],
        out_specs=pl.BlockSpec((None, None, 1, MOBA_WIDTH), lambda b, n: (b, n, 0, 0)),
        out_shape=jax.ShapeDtypeStruct((bsz, nb, 1, MOBA_WIDTH), F32),
        compiler_params=_cparams("parallel", "parallel"),
    )(proj3)
    return out.reshape(bsz, nb, MOBA_WIDTH)


def _select_top_blocks(gscore, n_valid, idx, axis):
    g = jnp.where(idx < n_valid, gscore, -jnp.inf)
    sel = jnp.zeros(gscore.shape, F32)
    width = gscore.shape[axis]
    for _ in range(MOBA_TOPK):
        m = jnp.max(g, axis=axis, keepdims=True)
        first = jnp.min(jnp.where(g == m, idx, width), axis=axis, keepdims=True)
        pick = idx == jnp.where(m > -jnp.inf, first, -1)
        sel = jnp.where(pick, 1.0, sel)
        g = jnp.where(pick, -jnp.inf, g)
    return sel


def _moba_prompt_kernel(q_ref, k_ref, v_ref, km_ref, o_ref, kb_ref, vt_ref, sel_ref):
    i = pl.program_id(2)
    blk = MOBA_BLOCK
    dh = MOBA_DH
    heads = q_ref.shape[-1] // dh
    hs = [slice(j * dh, (j + 1) * dh) for j in range(heads)]

    @pl.when(i == 0)
    def _():
        kb_ref[...] = k_ref[...].astype(BF16)

        def transpose_block(n, carry):
            r0 = pl.multiple_of(n * blk, blk)
            vt_ref[:, pl.ds(r0, blk)] = v_ref[pl.ds(r0, blk), :].T.astype(BF16)
            return carry

        lax.fori_loop(0, k_ref.shape[0] // blk, transpose_block, 0)

    qbs = []
    for j in range(heads):
        q = q_ref[:, hs[j]]
        gscore = _dot_nt(km_ref[:, hs[j]], q, precision=HIGHEST)
        blk_idx = lax.broadcasted_iota(jnp.int32, gscore.shape, 0)
        sel_ref[j] = _select_top_blocks(gscore, i, blk_idx, 0)
        qbs.append((q * (dh ** -0.5)).astype(BF16))

    def scores(j, n):
        return _dot_nt(kb_ref[pl.ds(pl.multiple_of(n * blk, blk), blk), hs[j]], qbs[j])

    def values(j, n, p):
        return _dot(vt_ref[hs[j], pl.ds(pl.multiple_of(n * blk, blk), blk)], p)

    def softmax_step(s, m, l, acc, pv):
        m_new = jnp.maximum(m, jnp.max(s, axis=0, keepdims=True))
        alpha = jnp.exp(m - m_new)
        p = jnp.exp(s - m_new)
        l = alpha * l + jnp.sum(p, axis=0, keepdims=True)
        return m_new, l, alpha * (acc + pv), p.astype(BF16)

    def past_block(n, carry):
        out = []
        for j in range(heads):
            m, l, acc, s, p_prev = carry[j]
            pv = values(j, jnp.maximum(n - 1, 0), p_prev)
            s_next = scores(j, n + 1)
            bias = jnp.where(sel_ref[j, pl.ds(n, 1), :] > 0.5, 0.0, NEG)
            m, l, acc, p = softmax_step(s + bias, m, l, acc, pv)
            out.append((m, l, acc, s_next, p))
        return tuple(out)

    init = tuple((jnp.full((1, blk), NEG, F32), jnp.zeros((1, blk), F32), jnp.zeros((dh, blk), F32),
                  scores(j, 0), jnp.zeros((blk, blk), BF16)) for j in range(heads))
    carry = lax.fori_loop(0, i, past_block, init)
    kpos = lax.broadcasted_iota(jnp.int32, (blk, blk), 0)
    qpos = lax.broadcasted_iota(jnp.int32, (blk, blk), 1)
    for j in range(heads):
        m, l, acc, s, p_prev = carry[j]
        pv = values(j, jnp.maximum(i - 1, 0), p_prev)
        m, l, acc, p = softmax_step(jnp.where(kpos <= qpos, s, NEG), m, l, acc, pv)
        acc = acc + values(j, i, p)
        o_ref[:, hs[j]] = (acc / l).T


def _moba_prompt(proj3, heads_per_step=2):
    bsz, t, _ = proj3.shape
    nb = t // MOBA_BLOCK
    km = _block_key_means(proj3)
    w = heads_per_step * MOBA_DH
    hq, hk, hv = COL_QC // w, COL_KC // w, COL_VC // w
    once = pl.Buffered(1)
    return pl.pallas_call(
        _moba_prompt_kernel,
        grid=(bsz, MOBA_HEADS // heads_per_step, nb),
        in_specs=[pl.BlockSpec((None, MOBA_BLOCK, w), lambda b, h, i: (b, i, hq + h)),
                  pl.BlockSpec((None, t, w), lambda b, h, i: (b, 0, hk + h), pipeline_mode=once),
                  pl.BlockSpec((None, t, w), lambda b, h, i: (b, 0, hv + h), pipeline_mode=once),
                  pl.BlockSpec((None, nb, w), lambda b, h, i: (b, 0, h))],
        out_specs=pl.BlockSpec((None, MOBA_BLOCK, w), lambda b, h, i: (b, i, h)),
        out_shape=jax.ShapeDtypeStruct((bsz, t, MOBA_WIDTH), F32),
        scratch_shapes=[pltpu.VMEM((t, w), BF16), pltpu.VMEM((w, t), BF16),
                        pltpu.VMEM((heads_per_step, nb, MOBA_BLOCK), F32)],
        compiler_params=_cparams("parallel", "parallel", "arbitrary"),
        name="moba_prompt",
    )(proj3, proj3, proj3, km)


def _moba_sample_kernel(pt_ref, q_ref, kn_ref, vn_ref, seg_ref, *refs, n_pages):
    k_pages = refs[:n_pages]
    v_pages = refs[n_pages:2 * n_pages]
    o_ref = refs[2 * n_pages]
    s_ref, km_ref = refs[2 * n_pages + 1], refs[2 * n_pages + 2]
    nh = MOBA_HEADS
    prow = PAGE_SIZE * nh
    pages_per_block = MOBA_BLOCK // PAGE_SIZE
    n_blocks = n_pages // pages_per_block
    sub = 8
    q = q_ref[...]
    nq = q.shape[0]
    q_head = lax.broadcasted_iota(jnp.int32, (nq, 1), 0) % nh
    q_tok = lax.broadcasted_iota(jnp.int32, (nq, 1), 0) // nh
    for n in range(n_blocks):
        tot = jnp.sum(k_pages[n * pages_per_block][...].reshape(prow // sub, sub, MOBA_DH), axis=0)
        for j in range(1, pages_per_block):
            tot = tot + jnp.sum(k_pages[n * pages_per_block + j][...].reshape(prow // sub, sub, MOBA_DH), axis=0)
        km_ref[n * sub:(n + 1) * sub, :] = tot
    g_all = _dot_nt(q, km_ref[...], precision=HIGHEST)
    g_lane = lax.broadcasted_iota(jnp.int32, g_all.shape, 1)
    g_own = jnp.where(g_lane % nh == q_head, g_all, 0.0)
    gscore = _dot(g_own, seg_ref[...], precision=HIGHEST) * (1.0 / MOBA_BLOCK)
    blk_idx = lax.broadcasted_iota(jnp.int32, gscore.shape, 1)
    sel = _select_top_blocks(gscore, n_blocks, blk_idx, 1)
    qb = (q * (MOBA_DH ** -0.5)).astype(BF16)
    own_head = lax.broadcasted_iota(jnp.int32, (nq, prow), 1) % nh == q_head
    for j in range(n_pages):
        s = _dot_nt(qb, k_pages[j][...].astype(BF16))
        nblk = j // pages_per_block
        keep = jnp.where(own_head, sel[:, nblk:nblk + 1], 0.0) > 0.5
        s_ref[:, j * prow:(j + 1) * prow] = jnp.where(keep, s, NEG)
    s_new = _dot_nt(qb, kn_ref[...].astype(BF16))
    coln = lax.broadcasted_iota(jnp.int32, s_new.shape, 1)
    ok_new = jnp.where(coln % nh == q_head, coln // nh, nq) <= q_tok
    s_new = jnp.where(ok_new, s_new, NEG)
    s_past = s_ref[...]
    m = jnp.maximum(jnp.max(s_past, axis=-1, keepdims=True), jnp.max(s_new, axis=-1, keepdims=True))
    p_past = jnp.exp(s_past - m)
    p_new = jnp.exp(s_new - m)
    l = jnp.sum(p_past, axis=-1, keepdims=True) + jnp.sum(p_new, axis=-1, keepdims=True)
    o = _dot(p_new.astype(BF16), vn_ref[...].astype(BF16))
    for j in range(n_pages):
        o = o + _dot(p_past[:, j * prow:(j + 1) * prow].astype(BF16), v_pages[j][...].astype(BF16))
    o_ref[...] = o / l


def _moba_sample(q, k_new, v_new, cache_k, cache_v, page_table, layer):
    bsz, nq, dh = q.shape
    depth, n_pool = cache_k.shape[:2]
    n_pages = page_table.shape[1]
    prow = PAGE_SIZE * MOBA_HEADS
    ck = cache_k.reshape(depth, n_pool, prow, dh)
    cv = cache_v.reshape(depth, n_pool, prow, dh)
    n_blocks = n_pages * PAGE_SIZE // MOBA_BLOCK
    seg = (jnp.arange(n_blocks * 8)[:, None] // 8 == jnp.arange(n_blocks)[None, :]).astype(F32)

    def page_spec(j):
        return pl.BlockSpec((None, None, prow, dh), lambda b, pt: (layer, pt[b * n_pages + j], 0, 0))

    grid_spec = pltpu.PrefetchScalarGridSpec(
        num_scalar_prefetch=1,
        grid=(bsz,),
        in_specs=([pl.BlockSpec((None, nq, dh), lambda b, pt: (b, 0, 0))] * 3
                  + [pl.BlockSpec(seg.shape, lambda b, pt: (0, 0))]
                  + [page_spec(j) for j in range(n_pages)] * 2),
        out_specs=pl.BlockSpec((None, nq, dh), lambda b, pt: (b, 0, 0)),
        scratch_shapes=[pltpu.VMEM((nq, n_pages * prow), F32), pltpu.VMEM((n_blocks * 8, dh), F32)],
    )
    return pl.pallas_call(
        functools.partial(_moba_sample_kernel, n_pages=n_pages),
        grid_spec=grid_spec,
        out_shape=jax.ShapeDtypeStruct((bsz, nq, dh), F32),
        compiler_params=_cparams("arbitrary"),
        name="moba_sample",
    )(page_table.reshape(-1), q, k_new, v_new, seg, *([ck] * n_pages), *([cv] * n_pages))


def _mix_kernel(ya_ref, ob_ref, oc_ref, od_ref, gl_ref, x_ref, g1_ref, wglu_ref, wb_ref, wc_ref, wd_ref, wo_ref,
                lng_ref, lnb_ref, o_ref):
    d = D_MODEL
    z = _dot(ya_ref[...].astype(BF16), wglu_ref[...])
    merged = jax.nn.sigmoid(gl_ref[:, 0:d]) * (z[:, 0:d] * jax.nn.sigmoid(z[:, d:2 * d]))
    merged += jax.nn.sigmoid(gl_ref[:, d:2 * d]) * _dot(ob_ref[...].astype(BF16), wb_ref[...])
    merged += jax.nn.sigmoid(gl_ref[:, 2 * d:3 * d]) * _dot(oc_ref[...].astype(BF16), wc_ref[...])
    merged += jax.nn.sigmoid(gl_ref[:, 3 * d:4 * d]) * _dot(od_ref[...].astype(BF16), wd_ref[...])
    mix = _dot(merged.astype(BF16), wo_ref[...])
    o_ref[...] = _layer_norm(DEEPNORM_ALPHA * x_ref[...] + g1_ref[...] * mix, lng_ref[...], lnb_ref[...])


def _mix(ya, ob, oc, od, proj, x, mod, rows_per_seq, wl, tm):
    n, d = x.shape
    w512 = SSM_WIDTH
    tok = lambda width, col=0: pl.BlockSpec((tm, width), lambda i: (i, col))
    const = lambda r, c: pl.BlockSpec((r, c), lambda i: (0, 0))
    return pl.pallas_call(
        _mix_kernel,
        grid=(n // tm,),
        in_specs=[tok(w512), tok(w512), tok(w512), tok(w512), tok(N_BRANCH * d, COL_GATES), tok(d),
                  _mod_spec(mod, tm, rows_per_seq, 2),
                  const(w512, 2 * d), const(w512, d), const(w512, d), const(w512, d), const(d, d),
                  const(1, d), const(1, d)],
        out_specs=tok(d),
        out_shape=jax.ShapeDtypeStruct((n, d), F32),
        compiler_params=_cparams("parallel"),
    )(ya, ob, oc, od, proj, x, mod, wl["w_glu"], wl["w_hgrn"], wl["w_moba"], wl["w_gla"], wl["w_out"],
      wl["ln1_g"], wl["ln1_b"])


def _first_max(vals, lane, width):
    m = jnp.max(vals, axis=-1, keepdims=True)
    first = jnp.min(jnp.where(vals == m, lane, width), axis=-1, keepdims=True)
    return m, first


def _route(h2, wr_ref, br_ref):
    logits = _dot(h2, wr_ref[...], precision=HIGHEST)
    lane = lax.broadcasted_iota(jnp.int32, logits.shape, 1)
    width = logits.shape[-1]
    scores = jax.nn.sigmoid(logits)
    biased = jnp.where(lane < N_EXPERTS, scores + br_ref[...], -jnp.inf)
    group = lane // EXPERTS_PER_GROUP
    best = jnp.zeros((logits.shape[0], 1), jnp.int32)
    best_score = None
    for gidx in range(N_EXPERT_GROUPS):
        vals = jnp.where(group == gidx, biased, -jnp.inf)
        m1, i1 = _first_max(vals, lane, width)
        m2 = jnp.max(jnp.where(lane == i1, -jnp.inf, vals), axis=-1, keepdims=True)
        gs = m1 + m2
        if best_score is None:
            best_score = gs
        else:
            better = gs > best_score
            best = jnp.where(better, gidx, best)
            best_score = jnp.where(better, gs, best_score)
    masked = jnp.where(group == best, biased, -jnp.inf)
    _, i1 = _first_max(masked, lane, width)
    _, i2 = _first_max(jnp.where(lane == i1, -jnp.inf, masked), lane, width)
    w1 = jnp.sum(jnp.where(lane == i1, scores, 0.0), axis=-1, keepdims=True)
    w2 = jnp.sum(jnp.where(lane == i2, scores, 0.0), axis=-1, keepdims=True)
    tot = w1 + w2
    return jnp.where(lane == i1, w1 / tot, 0.0) + jnp.where(lane == i2, w2 / tot, 0.0)


def _moe_kernel(x_ref, sh_ref, sc_ref, g2_ref, wr_ref, br_ref, wg_ref, wu_ref, wd_ref, lng_ref, lnb_ref, o_ref,
                h_ref, comb_ref, acc_ref, *, experts_per_step):
    e = pl.program_id(1)

    @pl.when(e == 0)
    def _():
        h2 = x_ref[...] * (1.0 + sc_ref[...]) + sh_ref[...]
        h_ref[...] = h2.astype(BF16)
        comb_ref[...] = _route(h2, wr_ref, br_ref)
        acc_ref[...] = jnp.zeros(acc_ref.shape, F32)

    hb = h_ref[...]
    lane = lax.broadcasted_iota(jnp.int32, comb_ref.shape, 1)
    for j in range(experts_per_step):
        eid = e * experts_per_step + j
        cw = jnp.sum(jnp.where(lane == eid, comb_ref[...], 0.0), axis=-1, keepdims=True)
        hid = _silu(_dot(hb, wg_ref[j])) * _dot(hb, wu_ref[j]) * cw
        acc_ref[...] += _dot(hid.astype(BF16), wd_ref[j])

    @pl.when(e == pl.num_programs(1) - 1)
    def _():
        o_ref[...] = _layer_norm(DEEPNORM_ALPHA * x_ref[...] + g2_ref[...] * acc_ref[...], lng_ref[...], lnb_ref[...])


def _moe(x, mod, rows_per_seq, wl, w_router, b_router, tm, experts_per_step=4):
    n, d = x.shape
    nh = EXPERT_HIDDEN
    tok = pl.BlockSpec((tm, d), lambda i, e: (i, 0))
    const = lambda r, c: pl.BlockSpec((r, c), lambda i, e: (0, 0))
    return pl.pallas_call(
        functools.partial(_moe_kernel, experts_per_step=experts_per_step),
        grid=(n // tm, N_EXPERTS // experts_per_step),
        in_specs=[tok, _mod_spec(mod, tm, rows_per_seq, 3), _mod_spec(mod, tm, rows_per_seq, 4),
                  _mod_spec(mod, tm, rows_per_seq, 5), const(d, LANES), const(1, LANES),
                  pl.BlockSpec((experts_per_step, d, nh), lambda i, e: (e, 0, 0)),
                  pl.BlockSpec((experts_per_step, d, nh), lambda i, e: (e, 0, 0)),
                  pl.BlockSpec((experts_per_step, nh, d), lambda i, e: (e, 0, 0)),
                  const(1, d), const(1, d)],
        out_specs=tok,
        out_shape=jax.ShapeDtypeStruct((n, d), F32),
        scratch_shapes=[pltpu.VMEM((tm, d), BF16), pltpu.VMEM((tm, LANES), F32), pltpu.VMEM((tm, d), F32)],
        compiler_params=_cparams("parallel", "arbitrary"),
    )(x, mod, mod, mod, w_router, b_router, wl["moe_gate"], wl["moe_up"], wl["moe_down"], wl["ln2_g"], wl["ln2_b"])


def _token_mixers_prompt(proj, k_rows, v_rows, bsz, t, wl):
    proj3 = proj.reshape(bsz, t, IN_COLS_PAD)
    ya, ssm_re, ssm_im = _ssm_prompt(proj3, wl["ssm"])
    ob, hgrn_s = _lin_rec("hgrn", proj3, (COL_QB, COL_FB, COL_IB, COL_GB), wl["hgrn_extras"], wl["hgrn_norm"],
                          None, 0, REC_CHUNK_PROMPT)
    od, gla_s = _lin_rec("gla", proj3, (COL_QD, COL_KD, COL_VD, COL_GD), wl["gla_extras"], wl["gla_norm"],
                         None, 0, REC_CHUNK_PROMPT)
    oc = _moba_prompt(proj3)
    n = bsz * t
    k_new = k_rows.reshape(bsz, t, MOBA_HEADS, MOBA_DH)
    v_new = v_rows.reshape(bsz, t, MOBA_HEADS, MOBA_DH)
    return (ya.reshape(n, -1), ob.reshape(n, -1), oc.reshape(n, -1), od.reshape(n, -1),
            (k_new, v_new, ssm_re, ssm_im, hgrn_s, gla_s))


def _token_mixers_sample(proj, k_rows, v_rows, bsz, t, wl, layer, cache_k, cache_v, page_table, st_re, st_im,
                         st_hgrn, st_gla):
    proj3 = proj.reshape(bsz, t, IN_COLS_PAD)
    u_tm = proj3[:, :, COL_UA:COL_UA + SSM_WIDTH].transpose(1, 0, 2)
    ns = SSM_GROUPS * SSM_STATE
    ya_tm, xr, xi = _ssm_sample(u_tm, st_re[layer].reshape(bsz, ns), st_im[layer].reshape(bsz, ns), wl["ssm"])
    ya = ya_tm.transpose(1, 0, 2)
    seqs = math.gcd(bsz, REC_SAMPLE_SEQS)
    ob, hgrn_s = _lin_rec("hgrn", proj3, (COL_QB, COL_FB, COL_IB, COL_GB), wl["hgrn_extras"], wl["hgrn_norm"],
                          st_hgrn, layer, SUB_CHUNK, seqs)
    od, gla_s = _lin_rec("gla", proj3, (COL_QD, COL_KD, COL_VD, COL_GD), wl["gla_extras"], wl["gla_norm"],
                         st_gla, layer, SUB_CHUNK, seqs)
    nq = t * MOBA_HEADS
    q = proj3[:, :, COL_QC:COL_QC + MOBA_WIDTH].reshape(bsz, nq, MOBA_DH)
    oc = _moba_sample(q, k_rows.reshape(bsz, nq, MOBA_DH), v_rows.reshape(bsz, nq, MOBA_DH), cache_k, cache_v,
                      page_table, layer)
    n = bsz * t
    return (ya.reshape(n, -1), ob.reshape(n, -1), oc.reshape(n, -1), od.reshape(n, -1),
            (k_rows.reshape(bsz, t, MOBA_HEADS, MOBA_DH), v_rows.reshape(bsz, t, MOBA_HEADS, MOBA_DH),
             xr.reshape(bsz, SSM_GROUPS, SSM_STATE), xi.reshape(bsz, SSM_GROUPS, SSM_STATE), hgrn_s, gla_s))


def _layer_weights(l, w_in_p, ssm_tabs, lower_bounds, gla_w_gk2, gla_b_gk, hgrn_norm, gla_norm, ssm_w_glu,
                   hgrn_w_proj, moba_w_proj, gla_w_proj, w_out, ln1_g, ln1_b, ln2_g, ln2_b, moe_w_gate, moe_w_up,
                   moe_w_down):
    lb = lower_bounds[l].reshape(1, -1)
    wgk = jnp.pad(gla_w_gk2[l], ((0, LANES - GLA_RANK), (0, 0))).astype(BF16)
    row = lambda a: a.reshape(1, -1)
    return dict(
        w_in=w_in_p[l], ssm={k: v[l] for k, v in ssm_tabs.items()},
        hgrn_extras=(jnp.log(lb), jnp.log1p(-lb), 1.0 - lb), hgrn_norm=hgrn_norm[l],
        gla_extras=(wgk, row(gla_b_gk[l])), gla_norm=gla_norm[l],
        w_glu=ssm_w_glu[l].astype(BF16), w_hgrn=hgrn_w_proj[l].astype(BF16), w_moba=moba_w_proj[l].astype(BF16),
        w_gla=gla_w_proj[l].astype(BF16), w_out=w_out[l].astype(BF16),
        ln1_g=row(ln1_g[l]), ln1_b=row(ln1_b[l]), ln2_g=row(ln2_g[l]), ln2_b=row(ln2_b[l]),
        moe_gate=moe_w_gate[l].astype(BF16), moe_up=moe_w_up[l].astype(BF16), moe_down=moe_w_down[l].astype(BF16))


def kernel(x_prompt, x_sample, cache_k, cache_v, state_ssm_re, state_ssm_im, state_hgrn, state_gla, page_table,
           c_prompt, c_sample, ln_in_g, ln_in_b, w_ada, b_ada, w_in, ssm_lam_re, ssm_lam_im, ssm_log_step,
           ssm_b_re, ssm_b_im, ssm_c_re, ssm_c_im, ssm_d, ssm_w_glu, hgrn_lb, hgrn_norm, hgrn_w_proj, moba_w_proj,
           gla_w_gk2, gla_b_gk, gla_norm, gla_w_proj, w_out, ln1_g, ln1_b, ln2_g, ln2_b, w_router, b_router,
           moe_w_gate, moe_w_up, moe_w_down):
    bp, tp, d = x_prompt.shape
    bs, ts, _ = x_sample.shape
    depth = w_in.shape[0]
    n_pool = cache_k.shape[1]
    np_tok, ns_tok = bp * tp, bs * ts
    tm_p = min(1024, tp)
    tm_s = min(512, ns_tok)

    lb_cum = jnp.cumsum(jax.nn.softmax(hgrn_lb.astype(F32), axis=0), axis=0)
    lower_bounds = lb_cum - lb_cum[0:1]
    w_in_p = jnp.concatenate(
        [w_in[:, :, REF_GATE_START:IN_COLS_REF], w_in[:, :, :REF_RD_START], w_in[:, :, REF_RD_START:REF_GATE_START],
         jnp.zeros((depth, d, IN_COLS_PAD - IN_COLS_REF), w_in.dtype)], axis=-1).astype(BF16)
    n_scan = max(tp // SSM_CHUNK - 1, 0).bit_length()
    ssm_tabs = jax.vmap(functools.partial(_ssm_tables, n_scan=n_scan))(
        ssm_lam_re, ssm_lam_im, ssm_log_step, ssm_b_re, ssm_b_im, ssm_c_re, ssm_c_im, ssm_d)
    w_router_p = jnp.pad(w_router, ((0, 0), (0, LANES - N_EXPERTS)))
    b_router_p = jnp.pad(b_router, (0, LANES - N_EXPERTS)).reshape(1, LANES)

    nc = bs + bp
    nc_pad = -(-nc // 8) * 8
    c_all = jnp.pad(jnp.concatenate([c_sample, c_prompt], axis=0), ((0, nc_pad - nc), (0, 0)))
    mod_all = _ada_all(c_all, w_ada, b_ada)

    xp = _ln_rows(x_prompt.reshape(np_tok, d), ln_in_g, ln_in_b, tm_p)
    xs = _ln_rows(x_sample.reshape(ns_tok, d), ln_in_g, ln_in_b, tm_s)

    outs_p, outs_s = [], []
    for l in range(depth):
        wl = _layer_weights(l, w_in_p, ssm_tabs, lower_bounds, gla_w_gk2, gla_b_gk, hgrn_norm, gla_norm, ssm_w_glu,
                            hgrn_w_proj, moba_w_proj, gla_w_proj, w_out, ln1_g, ln1_b, ln2_g, ln2_b, moe_w_gate,
                            moe_w_up, moe_w_down)
        mod_p = mod_all[l, bs:bs + bp].reshape(bp, 1, 6 * d)
        mod_s = jnp.repeat(mod_all[l, :bs], ts, axis=0)

        proj_p, k_rows, v_rows = _in_proj(xp, mod_p, tp, wl["w_in"], tm_p)
        ya, ob, oc, od, st_p = _token_mixers_prompt(proj_p, k_rows, v_rows, bp, tp, wl)
        xp = _mix(ya, ob, oc, od, proj_p, xp, mod_p, tp, wl, min(256, tp))
        xp = _moe(xp, mod_p, tp, wl, w_router_p, b_router_p, tm_p)
        outs_p.append(st_p)

        proj_s, k_rows, v_rows = _in_proj(xs, mod_s, ts, wl["w_in"], tm_s)
        ya, ob, oc, od, st_s = _token_mixers_sample(proj_s, k_rows, v_rows, bs, ts, wl, l, cache_k, cache_v,
                                                    page_table, state_ssm_re, state_ssm_im, state_hgrn, state_gla)
        xs = _mix(ya, ob, oc, od, proj_s, xs, mod_s, ts, wl, min(256, ns_tok))
        xs = _moe(xs, mod_s, ts, wl, w_router_p, b_router_p, tm_s)
        outs_s.append(st_s)

    stack = lambda outs, idx: jnp.stack([o[idx] for o in outs])
    return (xp.reshape(bp, tp, d), xs.reshape(bs, ts, d),
            stack(outs_p, 0), stack(outs_p, 1), stack(outs_s, 0), stack(outs_s, 1),
            stack(outs_p, 2), stack(outs_p, 3), stack(outs_s, 2), stack(outs_s, 3),
            stack(outs_p, 4), stack(outs_s, 4), stack(outs_p, 5), stack(outs_s, 5))
```

```python
import functools
import math

import jax
import jax.numpy as jnp
from jax import lax
from jax.experimental import pallas as pl
from jax.experimental.pallas import tpu as pltpu

F32 = jnp.float32
BF16 = jnp.bfloat16
HIGHEST = lax.Precision.HIGHEST

D_MODEL = 1024
DEPTH = 4
PAGE_SIZE = 128
SSM_WIDTH = 512
SSM_GROUP = 16
SSM_GROUPS = 32
SSM_STATE = 64
SSM_MAX_RE = -1e-4
SSM_CHUNK = 16
SSM_QUAD_GROUPS = 8
HGRN_HEADS = 4
HGRN_DK = 128
HGRN_DV = 128
MOBA_HEADS = 4
MOBA_DH = 128
MOBA_WIDTH = 512
MOBA_BLOCK = 256
MOBA_TOPK = 3
GLA_HEADS = 4
GLA_DK = 64
GLA_DV = 128
GLA_RANK = 16
GLA_TAU = 16.0
N_BRANCH = 4
N_EXPERTS = 16
N_EXPERT_GROUPS = 4
EXPERTS_PER_GROUP = 4
EXPERT_HIDDEN = 256
DEEPNORM_ALPHA = (2 * DEPTH) ** 0.25
NORM_EPS = 1e-5

LANES = 128
SUB_CHUNK = 16
REC_CHUNK_PROMPT = 128
REC_SAMPLE_SEQS = 4
MOBA_SAMPLE_SEQS = 1
VMEM_LIMIT = 56 * 1024 * 1024
NEG = -1e30

GATE_COLS = N_BRANCH * D_MODEL
COL_UA = 0
COL_QB = 512
COL_FB = 1024
COL_IB = 1536
COL_GB = 2048
COL_QC = 2560
COL_KC = 3072
COL_VC = 3584
COL_QD = 4096
COL_KD = 4352
COL_VD = 4608
COL_GD = 5120
COL_RD = 5632
IN_COLS_PAD = 6144
IN_COLS_REF = 9744
REF_GATE_START = 5648
REF_RD_START = 5632


def _cparams(*sem):
    return pltpu.CompilerParams(dimension_semantics=sem, vmem_limit_bytes=VMEM_LIMIT)


def _silu(x):
    return x * jax.nn.sigmoid(x)


def _log_sigmoid(x):
    return jnp.minimum(x, 0.0) - jnp.log1p(jnp.exp(-jnp.abs(x)))


def _dot(a, b, **kw):
    return jnp.dot(a, b, preferred_element_type=F32, **kw)


def _dot_nt(a, b, **kw):
    return lax.dot_general(a, b, (((1,), (1,)), ((), ())), preferred_element_type=F32, **kw)


def _dot_tn(a, b, **kw):
    return lax.dot_general(a, b, (((0,), (0,)), ((), ())), preferred_element_type=F32, **kw)


def _layer_norm(x, g, b):
    mu = jnp.mean(x, axis=-1, keepdims=True)
    xc = x - mu
    var = jnp.mean(xc * xc, axis=-1, keepdims=True)
    return xc * lax.rsqrt(var + NORM_EPS) * g + b


def _ln_kernel(x_ref, g_ref, b_ref, o_ref):
    o_ref[...] = _layer_norm(x_ref[...], g_ref[...], b_ref[...])


def _ln_rows(x, g, b, tm):
    n, d = x.shape
    return pl.pallas_call(
        _ln_kernel,
        grid=(n // tm,),
        in_specs=[pl.BlockSpec((tm, d), lambda i: (i, 0)),
                  pl.BlockSpec((1, d), lambda i: (0, 0)),
                  pl.BlockSpec((1, d), lambda i: (0, 0))],
        out_specs=pl.BlockSpec((tm, d), lambda i: (i, 0)),
        out_shape=jax.ShapeDtypeStruct((n, d), F32),
        compiler_params=_cparams("parallel"),
    )(x, g.reshape(1, d), b.reshape(1, d))


def _ada_kernel(c_ref, w_ref, b_ref, o_ref):
    h = _silu(c_ref[...]).astype(BF16)
    o_ref[...] = _dot(h, w_ref[...].astype(BF16)) + b_ref[...]


def _ada_all(c, w_ada, b_ada, tn=1536):
    nb, d = c.shape
    depth, _, n6 = w_ada.shape
    return pl.pallas_call(
        _ada_kernel,
        grid=(depth, n6 // tn),
        in_specs=[pl.BlockSpec((nb, d), lambda l, j: (0, 0)),
                  pl.BlockSpec((None, d, tn), lambda l, j: (l, 0, j)),
                  pl.BlockSpec((None, 1, tn), lambda l, j: (l, 0, j))],
        out_specs=pl.BlockSpec((None, nb, tn), lambda l, j: (l, 0, j)),
        out_shape=jax.ShapeDtypeStruct((depth, nb, n6), F32),
        compiler_params=_cparams("parallel", "parallel"),
    )(c, w_ada, b_ada.reshape(depth, 1, n6))


def _mod_spec(mod, tm, rows_per_seq, chunk):
    if mod.ndim == 3:
        tiles_per_seq = rows_per_seq // tm
        return pl.BlockSpec((None, 1, D_MODEL), lambda i, *_: (i // tiles_per_seq, 0, chunk))
    return pl.BlockSpec((tm, D_MODEL), lambda i, *_: (i, chunk))


def _inproj_kernel(x_ref, sh_ref, sc_ref, w_ref, o_ref, gl_ref, k_ref, v_ref, h_ref, *, gate_tiles, kv_tile):
    j = pl.program_id(1)

    @pl.when(j == 0)
    def _():
        h_ref[...] = (x_ref[...] * (1.0 + sc_ref[...]) + sh_ref[...]).astype(BF16)

    acc = _dot(h_ref[...], w_ref[...])

    @pl.when(j < gate_tiles)
    def _():
        gl_ref[...] = acc.astype(BF16)

    @pl.when(j >= gate_tiles)
    def _():
        o_ref[...] = acc

    @pl.when(j == kv_tile)
    def _():
        tm = acc.shape[0]
        for h in range(MOBA_HEADS):
            k_ref[pl.ds(h, tm, stride=MOBA_HEADS), :] = acc[:, h * MOBA_DH:(h + 1) * MOBA_DH]
            v_ref[pl.ds(h, tm, stride=MOBA_HEADS), :] = acc[:, MOBA_WIDTH + h * MOBA_DH:MOBA_WIDTH + (h + 1) * MOBA_DH]


def _in_proj(x, mod, rows_per_seq, w, tm, tn=1024):
    n, d = x.shape
    ncol = w.shape[1]
    assert ncol == GATE_COLS + IN_COLS_PAD and GATE_COLS % tn == 0
    assert COL_KC % tn == 0 and COL_VC == COL_KC + MOBA_WIDTH and tn == 2 * MOBA_WIDTH
    gate_tiles = GATE_COLS // tn
    kv_spec = pl.BlockSpec((tm * MOBA_HEADS, MOBA_DH), lambda i, j: (i, 0))
    kv_shape = jax.ShapeDtypeStruct((n * MOBA_HEADS, MOBA_DH), F32)
    return pl.pallas_call(
        functools.partial(_inproj_kernel, gate_tiles=gate_tiles, kv_tile=gate_tiles + COL_KC // tn),
        grid=(n // tm, ncol // tn),
        in_specs=[pl.BlockSpec((tm, d), lambda i, j: (i, 0)),
                  _mod_spec(mod, tm, rows_per_seq, 0),
                  _mod_spec(mod, tm, rows_per_seq, 1),
                  pl.BlockSpec((d, tn), lambda i, j: (0, j))],
        out_specs=[pl.BlockSpec((tm, tn), lambda i, j: (i, jnp.maximum(j - gate_tiles, 0))),
                   pl.BlockSpec((tm, tn), lambda i, j: (i, jnp.minimum(j, gate_tiles - 1))), kv_spec, kv_spec],
        out_shape=[jax.ShapeDtypeStruct((n, IN_COLS_PAD), F32), jax.ShapeDtypeStruct((n, GATE_COLS), BF16),
                   kv_shape, kv_shape],
        scratch_shapes=[pltpu.VMEM((tm, d), BF16)],
        compiler_params=_cparams("parallel", "arbitrary"),
        name="in_proj",
    )(x, mod, mod, w)


def _ssm_prompt_kernel(u_ref, toep_ref, inre_ref, inim_ref, outre_ref, outim_ref, d_ref, are_ref, aim_ref,
                       y_ref, sre_ref, sim_ref, ucat_ref, *, n_steps):
    nrow = ucat_ref.shape[0]
    for tau in range(SSM_CHUNK):
        ucat_ref[:, tau * LANES:(tau + 1) * LANES] = u_ref[pl.ds(tau, nrow, stride=SSM_CHUNK), :].astype(BF16)
    ucat = ucat_ref[...]
    xr = _dot(ucat, inre_ref[...])
    xi = _dot(ucat, inim_ref[...])
    row = lax.broadcasted_iota(jnp.int32, xr.shape, 0)
    for k in range(n_steps):
        dist = 1 << k
        ar = are_ref[k:k + 1, :]
        ai = aim_ref[k:k + 1, :]
        keep = row >= dist
        sr = jnp.where(keep, pltpu.roll(xr, dist, 0), 0.0)
        si = jnp.where(keep, pltpu.roll(xi, dist, 0), 0.0)
        xr, xi = xr + (ar * sr - ai * si), xi + (ar * si + ai * sr)
    keep = row >= 1
    pr = jnp.where(keep, pltpu.roll(xr, 1, 0), 0.0)
    pi = jnp.where(keep, pltpu.roll(xi, 1, 0), 0.0)
    out = (_dot(ucat, toep_ref[...]) + _dot(pr.astype(BF16), outre_ref[...])
           + _dot(pi.astype(BF16), outim_ref[...]))
    for tau in range(SSM_CHUNK):
        u_tau = u_ref[pl.ds(tau, nrow, stride=SSM_CHUNK), :]
        y_ref[pl.ds(tau, nrow, stride=SSM_CHUNK), :] = jax.nn.gelu(out[:, tau * LANES:(tau + 1) * LANES]
                                                                   + d_ref[...] * u_tau)
    sre_ref[...] = xr[nrow - 1:nrow, :]
    sim_ref[...] = xi[nrow - 1:nrow, :]


def _ssm_prompt(proj3, sp):
    bsz, t, _ = proj3.shape
    nrow = t // SSM_CHUNK
    n_steps = max(nrow - 1, 0).bit_length()
    nquad = SSM_WIDTH // LANES
    kw = SSM_CHUNK * LANES
    ns = SSM_QUAD_GROUPS * SSM_STATE
    once = pl.Buffered(1)
    y, sre, sim = pl.pallas_call(
        functools.partial(_ssm_prompt_kernel, n_steps=n_steps),
        grid=(nquad, bsz),
        in_specs=[pl.BlockSpec((None, t, LANES), lambda q, b: (b, 0, COL_UA // LANES + q)),
                  pl.BlockSpec((None, kw, kw), lambda q, b: (q, 0, 0), pipeline_mode=once),
                  pl.BlockSpec((None, kw, ns), lambda q, b: (q, 0, 0), pipeline_mode=once),
                  pl.BlockSpec((None, kw, ns), lambda q, b: (q, 0, 0), pipeline_mode=once),
                  pl.BlockSpec((None, ns, kw), lambda q, b: (q, 0, 0), pipeline_mode=once),
                  pl.BlockSpec((None, ns, kw), lambda q, b: (q, 0, 0), pipeline_mode=once),
                  pl.BlockSpec((1, LANES), lambda q, b: (0, q)),
                  pl.BlockSpec((None, sp["scan_re"].shape[1], ns), lambda q, b: (q, 0, 0)),
                  pl.BlockSpec((None, sp["scan_im"].shape[1], ns), lambda q, b: (q, 0, 0))],
        out_specs=[pl.BlockSpec((None, t, LANES), lambda q, b: (b, 0, q)),
                   pl.BlockSpec((None, None, 1, ns), lambda q, b: (b, q, 0, 0)),
                   pl.BlockSpec((None, None, 1, ns), lambda q, b: (b, q, 0, 0))],
        out_shape=[jax.ShapeDtypeStruct((bsz, t, SSM_WIDTH), F32),
                   jax.ShapeDtypeStruct((bsz, nquad, 1, ns), F32),
                   jax.ShapeDtypeStruct((bsz, nquad, 1, ns), F32)],
        scratch_shapes=[pltpu.VMEM((nrow, kw), BF16)],
        compiler_params=_cparams("arbitrary", "arbitrary"),
        name="ssm_prompt",
    )(proj3, sp["toep"], sp["in_re"], sp["in_im"], sp["out_re"], sp["out_im"], sp["d_row"], sp["scan_re"],
      sp["scan_im"])
    return y, sre.reshape(bsz, SSM_GROUPS, SSM_STATE), sim.reshape(bsz, SSM_GROUPS, SSM_STATE)


def _ssm_sample_kernel(u_ref, x0r_ref, x0i_ref, bbr_ref, bbi_ref, cr_ref, ci_ref, ar_ref, ai_ref, d_ref,
                       y_ref, xr_ref, xi_ref):
    xr = x0r_ref[...]
    xi = x0i_ref[...]
    ar = ar_ref[...]
    ai = ai_ref[...]
    for t in range(u_ref.shape[0]):
        u = u_ref[t]
        ub = u.astype(BF16)
        xr, xi = (ar * xr - ai * xi + _dot(ub, bbr_ref[...]),
                  ar * xi + ai * xr + _dot(ub, bbi_ref[...]))
        y = _dot(xr.astype(BF16), cr_ref[...]) - _dot(xi.astype(BF16), ci_ref[...]) + d_ref[...] * u
        y_ref[t] = jax.nn.gelu(y)
    xr_ref[...] = xr
    xi_ref[...] = xi


def _ssm_sample(u_tm, x0r, x0i, sp):
    t, bsz, w = u_tm.shape
    ns = SSM_GROUPS * SSM_STATE
    full = lambda *shape: pl.BlockSpec(shape, lambda i: (0,) * len(shape))
    return pl.pallas_call(
        _ssm_sample_kernel,
        grid=(1,),
        in_specs=[full(t, bsz, w), full(bsz, ns), full(bsz, ns), full(w, ns), full(w, ns), full(ns, w), full(ns, w),
                  full(1, ns), full(1, ns), full(1, w)],
        out_specs=[full(t, bsz, w), full(bsz, ns), full(bsz, ns)],
        out_shape=[jax.ShapeDtypeStruct((t, bsz, w), F32), jax.ShapeDtypeStruct((bsz, ns), F32),
                   jax.ShapeDtypeStruct((bsz, ns), F32)],
        compiler_params=_cparams("arbitrary"),
    )(u_tm, x0r, x0i, sp["bb_re"], sp["bb_im"], sp["c_re"], sp["c_im"], sp["a_re"], sp["a_im"], sp["d_row"])


def _ssm_tables(lam_re, lam_im, log_step, b_re, b_im, c_re, c_im, d, n_scan):
    hp = dict(precision=HIGHEST)
    g, p, hch = SSM_GROUPS, SSM_STATE, SSM_GROUP
    step = jnp.exp(log_step)[:, None]
    lr = jnp.minimum(lam_re, SSM_MAX_RE)
    li = lam_im
    mag = jnp.exp(lr * step)
    ab_re = mag * jnp.cos(li * step)
    ab_im = mag * jnp.sin(li * step)
    den = lr * lr + li * li
    coef_re = ((ab_re - 1.0) * lr + ab_im * li) / den
    coef_im = (ab_im * lr - (ab_re - 1.0) * li) / den
    bb_re = coef_re[..., None] * b_re - coef_im[..., None] * b_im
    bb_im = coef_re[..., None] * b_im + coef_im[..., None] * b_re

    def cmul(xr, xi, yr, yi):
        return xr * yr - xi * yi, xr * yi + xi * yr

    pows_re, pows_im = [jnp.ones_like(ab_re)], [jnp.zeros_like(ab_im)]
    for _ in range(SSM_CHUNK):
        nr, ni = cmul(pows_re[-1], pows_im[-1], ab_re, ab_im)
        pows_re.append(nr)
        pows_im.append(ni)
    pw_re = jnp.stack(pows_re)
    pw_im = jnp.stack(pows_im)
    ajb_re = pw_re[:SSM_CHUNK, :, :, None] * bb_re - pw_im[:SSM_CHUNK, :, :, None] * bb_im
    ajb_im = pw_re[:SSM_CHUNK, :, :, None] * bb_im + pw_im[:SSM_CHUNK, :, :, None] * bb_re
    klag = (jnp.einsum("jgpi,gop->jgio", ajb_re, c_re, **hp) - jnp.einsum("jgpi,gop->jgio", ajb_im, c_im, **hp))
    rev = jnp.arange(SSM_CHUNK - 1, -1, -1)
    rev_re = pw_re[rev]
    rev_im = pw_im[rev]
    p_re = (rev_re[..., None] * bb_re - rev_im[..., None] * bb_im).transpose(1, 0, 3, 2).reshape(g, SSM_CHUNK * hch, p)
    p_im = (rev_re[..., None] * bb_im + rev_im[..., None] * bb_re).transpose(1, 0, 3, 2).reshape(g, SSM_CHUNK * hch, p)
    a1_re = pw_re[1:]
    a1_im = pw_im[1:]
    cr_t = c_re.transpose(0, 2, 1)
    ci_t = c_im.transpose(0, 2, 1)
    q_re = (cr_t[None] * a1_re[..., None] - ci_t[None] * a1_im[..., None]).transpose(1, 2, 0, 3).reshape(g, p, SSM_CHUNK * hch)
    q_im = (-(cr_t[None] * a1_im[..., None] + ci_t[None] * a1_re[..., None])).transpose(1, 2, 0, 3).reshape(g, p, SSM_CHUNK * hch)
    d_chunk = jnp.broadcast_to(d[:, None, :], (g, SSM_CHUNK, hch)).reshape(g, 1, SSM_CHUNK * hch)
    sc_re, sc_im = [pw_re[SSM_CHUNK]], [pw_im[SSM_CHUNK]]
    for _ in range(max(n_scan, 1) - 1):
        nr, ni = cmul(sc_re[-1], sc_im[-1], sc_re[-1], sc_im[-1])
        sc_re.append(nr)
        sc_im.append(ni)
    scan_re = jnp.stack(sc_re, axis=1)
    scan_im = jnp.stack(sc_im, axis=1)

    nq, gq = g // SSM_QUAD_GROUPS, SSM_QUAD_GROUPS
    kw = SSM_CHUNK * gq * hch
    qw = gq * hch

    def group_mask(rows_per_group, cols_per_group):
        r = jnp.arange(gq * rows_per_group)[:, None] // rows_per_group
        c = jnp.arange(gq * cols_per_group)[None, :] // cols_per_group
        return (r == c).astype(F32)

    def replicate(width):
        return (jnp.arange(width)[:, None] == jnp.arange(gq * width)[None, :] % width).astype(F32)

    klag_r = klag.reshape(SSM_CHUNK, nq, gq * hch, hch).transpose(1, 0, 2, 3)
    lag_blk = (jnp.einsum("njro,oc->njrc", klag_r, replicate(hch), **hp) * group_mask(hch, hch)).astype(BF16)
    lag_row = lag_blk.transpose(0, 2, 1, 3).reshape(nq, qw, kw)
    lag_row = jnp.concatenate([jnp.zeros_like(lag_row), lag_row], axis=-1)
    toep_q = jnp.concatenate([lag_row[:, :, (SSM_CHUNK - ts) * qw:(2 * SSM_CHUNK - ts) * qw]
                              for ts in range(SSM_CHUNK)], axis=1)

    def in_state(m):
        m = m.reshape(nq, gq, SSM_CHUNK, hch, p).transpose(0, 2, 1, 3, 4).reshape(nq, SSM_CHUNK, qw, p)
        m = jnp.einsum("ntrp,pc->ntrc", m, replicate(p), **hp) * group_mask(hch, p)
        return m.reshape(nq, kw, gq * p)

    def state_out(m):
        m = m.reshape(nq, gq * p, SSM_CHUNK, hch)
        m = jnp.einsum("nrto,oc->nrtc", m, replicate(hch), **hp) * group_mask(p, hch)[:, None, :]
        return m.reshape(nq, gq * p, kw)

    def scan_rows(m):
        return m.reshape(nq, gq, m.shape[1], p).transpose(0, 2, 1, 3).reshape(nq, m.shape[1], gq * p)

    def dense_diag(blocks):
        eye_g = jnp.eye(g, dtype=blocks.dtype)
        return (blocks[:, :, None, :] * eye_g[:, None, :, None]).reshape(g * blocks.shape[1], g * blocks.shape[2])

    return dict(
        toep=toep_q.astype(BF16),
        in_re=in_state(p_re).astype(BF16), in_im=in_state(p_im).astype(BF16),
        out_re=state_out(q_re).astype(BF16), out_im=state_out(q_im).astype(BF16),
        scan_re=scan_rows(scan_re), scan_im=scan_rows(scan_im),
        bb_re=dense_diag(bb_re.transpose(0, 2, 1)).astype(BF16), bb_im=dense_diag(bb_im.transpose(0, 2, 1)).astype(BF16),
        c_re=dense_diag(cr_t).astype(BF16), c_im=dense_diag(ci_t).astype(BF16),
        a_re=ab_re.reshape(1, g * p), a_im=ab_im.reshape(1, g * p), d_row=d.reshape(1, g * hch))


def _rec_levels(chunk):
    return [1 << p for p in range(chunk.bit_length() - 1)]


def _rec_tables(chunk):
    t = jnp.arange(chunk)[:, None]
    u = jnp.arange(chunk)[None, :]
    tabs = [u <= t]
    for half in _rec_levels(chunk):
        if half < 8:
            tabs.append(u <= (t // (2 * half)) * (2 * half) + half - 1)
    return jnp.concatenate(tabs, axis=0).astype(BF16)


def _linrec_kernel(*refs, mode, chunk, rows, heads, dk, dv, has_s0, seqs):
    it = iter(refs)
    q_ref, k_ref, v_ref, g_ref = next(it), next(it), next(it), next(it)
    if mode == "hgrn":
        loglb_ref, log1mlb_ref, onemlb_ref = next(it), next(it), next(it)
    else:
        r_ref, wgk_ref, bgk_ref = next(it), next(it), next(it)
    normw_ref = next(it)
    s0_ref = next(it) if has_s0 else None
    tstack_ref = next(it)
    o_ref, sfin_ref = next(it), next(it)
    st_ref = next(it)
    pad_refs = [next(it) for _ in range(5)] if rows != chunk else None

    c = pl.program_id(1)
    pack = LANES // dk
    ngrp = heads // pack
    gw = pack * dk

    @pl.when(c == 0)
    def _():
        for sg in range(seqs * ngrp):
            if has_s0:
                sq0, h0 = sg // ngrp, (sg % ngrp) * pack
                st_ref[sg] = jnp.concatenate([s0_ref[sq0, h0 + hh].T for hh in range(pack)], axis=1)
            else:
                st_ref[sg] = jnp.zeros((dv, gw), F32)

    def load(ref, slot, sq):
        if pad_refs is None:
            return ref[sq]
        buf = pad_refs[slot]
        buf[sq] = jnp.zeros(buf.shape[1:], F32)
        buf[sq, 0:rows, :] = ref[sq]
        return buf[sq]

    trow = lax.broadcasted_iota(jnp.int32, (chunk, chunk), 0)
    tcol = lax.broadcasted_iota(jnp.int32, (chunk, chunk), 1)
    dxor = trow ^ tcol
    rowi = lax.broadcasted_iota(jnp.int32, (chunk, 1), 0)
    valid = rowi < rows
    levels = _rec_levels(chunk)
    tstack = tstack_ref[...]

    lane_head = lax.broadcasted_iota(jnp.int32, (1, gw), 1) // dk

    def own_lanes(x, hh):
        return x if pack == 1 else jnp.where(lane_head == hh, x, 0.0)

    for sq, grp in [(a, b) for a in range(seqs) for b in range(ngrp)]:
        if grp == 0:
            q_all, k_all, v_all, g_all = load(q_ref, 0, sq), load(k_ref, 1, sq), load(v_ref, 2, sq), load(g_ref, 3, sq)
            if mode == "gla":
                gk_all = _dot(load(r_ref, 4, sq).astype(BF16), wgk_ref[...]) + bgk_ref[...]
        sg = sq * ngrp + grp
        ksl = slice(grp * gw, (grp + 1) * gw)
        if mode == "hgrn":
            q = _silu(q_all[:, ksl])
            z = k_all[:, ksl]
            la = loglb_ref[:, ksl]
            lc = log1mlb_ref[:, ksl] + _log_sigmoid(z)
            lf = jnp.maximum(la, lc) + jnp.log1p(jnp.exp(-jnp.abs(la - lc)))
            k = onemlb_ref[:, ksl] * jax.nn.sigmoid(-z)
        else:
            q = q_all[:, ksl] * (dk ** -0.5)
            k = k_all[:, ksl]
            lf = _log_sigmoid(gk_all[:, ksl]) / GLA_TAU
        if rows != chunk:
            lf = jnp.where(valid, lf, 0.0)
        hi = lf.astype(BF16)
        rest = lf - hi.astype(F32)
        mid = rest.astype(BF16)
        lo = (rest - mid.astype(F32)).astype(BF16)
        sums = _dot(tstack, hi) + _dot(tstack, mid) + _dot(tstack, lo)
        b = sums[0:chunk]
        st = st_ref[sg]
        stb = st.astype(BF16)
        qe = q * jnp.exp(b)
        b_last = b[chunk - 1:chunk, :]
        k_dec = k * jnp.exp(b_last - b)
        qk = q * k

        atts = [jnp.broadcast_to(jnp.sum(own_lanes(qk, hh), axis=-1, keepdims=True), (chunk, chunk))
                for hh in range(pack)]
        for li, half in enumerate(levels):
            if half < 8:
                beta = sums[(li + 1) * chunk:(li + 2) * chunk]
            else:
                nblk = chunk // (2 * half)
                b3 = b.reshape(nblk, 2 * half, gw)
                beta = jnp.broadcast_to(b3[:, half - 1:half, :], (nblk, 2 * half, gw)).reshape(chunk, gw)
            e = jnp.exp(-jnp.abs(b - beta))
            upper = (rowi & half) != 0
            qh = jnp.where(upper, q * e, 0.0)
            kh = jnp.where(upper, 0.0, k * e).astype(BF16)
            for hh in range(pack):
                atts[hh] = jnp.where(dxor >= half, _dot_nt(own_lanes(qh, hh).astype(BF16), kh), atts[hh])
        st_new = st * jnp.exp(b_last)
        for hh in range(pack):
            h = grp * pack + hh
            vsl = slice(h * dv, (h + 1) * dv)
            vb = v_all[:, vsl].astype(BF16)
            att = jnp.where(trow >= tcol, atts[hh], 0.0)
            o = _dot_nt(own_lanes(qe, hh).astype(BF16), stb) + _dot(att.astype(BF16), vb)
            o = o * lax.rsqrt(jnp.mean(o * o, axis=-1, keepdims=True) + NORM_EPS) * normw_ref[...]
            o = o * _silu(g_all[:, vsl])
            o_ref[sq, :, vsl] = o[0:rows, :]
            st_new = st_new + _dot_tn(vb, own_lanes(k_dec, hh).astype(BF16))
        st_ref[sg] = st_new

    @pl.when(c == pl.num_programs(1) - 1)
    def _():
        for sg in range(seqs * ngrp):
            for hh in range(pack):
                sfin_ref[sg // ngrp, (sg % ngrp) * pack + hh] = st_ref[sg][:, hh * dk:(hh + 1) * dk].T


def _lin_rec(mode, proj3, cols, extras, normw, s0, layer, chunk, seqs=1):
    bsz, t, _ = proj3.shape
    heads, dk, dv = (HGRN_HEADS, HGRN_DK, HGRN_DV) if mode == "hgrn" else (GLA_HEADS, GLA_DK, GLA_DV)
    rows = min(chunk, t)
    nchunk = max(t // chunk, 1)
    wk, wv = heads * dk, heads * dv
    assert bsz % seqs == 0

    def col_spec(width, col):
        return pl.BlockSpec((seqs, rows, width), lambda b, c: (b, c, col // width))

    in_specs = [col_spec(wk, cols[0]), col_spec(wk, cols[1]), col_spec(wv, cols[2]), col_spec(wv, cols[3])]
    args = [proj3, proj3, proj3, proj3]
    if mode == "hgrn":
        in_specs += [pl.BlockSpec((1, wk), lambda b, c: (0, 0))] * 3
        args += list(extras)
    else:
        in_specs += [col_spec(LANES, COL_RD), pl.BlockSpec((LANES, wk), lambda b, c: (0, 0)),
                     pl.BlockSpec((1, wk), lambda b, c: (0, 0))]
        args += [proj3] + list(extras)
    in_specs.append(pl.BlockSpec((1, dv), lambda b, c: (0, 0)))
    args.append(normw.reshape(1, dv))
    has_s0 = s0 is not None
    if has_s0:
        in_specs.append(pl.BlockSpec((None, seqs, heads, dk, dv), lambda b, c: (layer, b, 0, 0, 0)))
        args.append(s0)
    tables = _rec_tables(chunk)
    in_specs.append(pl.BlockSpec(tables.shape, lambda b, c: (0, 0)))
    args.append(tables)
    scratch = [pltpu.VMEM((seqs * heads * dk // LANES, dv, LANES), F32)]
    if rows != chunk:
        scratch += [pltpu.VMEM((seqs, chunk, wk), F32), pltpu.VMEM((seqs, chunk, wk), F32),
                    pltpu.VMEM((seqs, chunk, wv), F32), pltpu.VMEM((seqs, chunk, wv), F32),
                    pltpu.VMEM((seqs, chunk, LANES), F32)]
    return pl.pallas_call(
        functools.partial(_linrec_kernel, mode=mode, chunk=chunk, rows=rows, heads=heads, dk=dk, dv=dv,
                          has_s0=has_s0, seqs=seqs),
        grid=(bsz // seqs, nchunk),
        in_specs=in_specs,
        out_specs=[pl.BlockSpec((seqs, rows, wv), lambda b, c: (b, c, 0)),
                   pl.BlockSpec((seqs, heads, dk, dv), lambda b, c: (b, 0, 0, 0))],
        out_shape=[jax.ShapeDtypeStruct((bsz, t, wv), F32), jax.ShapeDtypeStruct((bsz, heads, dk, dv), F32)],
        scratch_shapes=scratch,
        compiler_params=_cparams("parallel", "arbitrary"),
        name=f"{mode}_rec",
    )(*args)


def _kmean_kernel(k_ref, o_ref):
    o_ref[...] = jnp.mean(k_ref[...], axis=0, keepdims=True)


def _block_key_means(proj3):
    bsz, t, _ = proj3.shape
    nb = t // MOBA_BLOCK
    out = pl.pallas_call(
        _kmean_kernel,
        grid=(bsz, nb),
        in_specs=[pl.BlockSpec((None, MOBA_BLOCK, MOBA_WIDTH), lambda b, n: (b, n, COL_KC // MOBA_WIDTH))],
        out_specs=pl.BlockSpec((None, None, 1, MOBA_WIDTH), lambda b, n: (b, n, 0, 0)),
        out_shape=jax.ShapeDtypeStruct((bsz, nb, 1, MOBA_WIDTH), F32),
        compiler_params=_cparams("parallel", "parallel"),
    )(proj3)
    return out.reshape(bsz, nb, MOBA_WIDTH)


def _select_top_blocks(gscore, n_valid, idx, axis):
    g = jnp.where(idx < n_valid, gscore, -jnp.inf)
    sel = jnp.zeros(gscore.shape, F32)
    width = gscore.shape[axis]
    for _ in range(MOBA_TOPK):
        m = jnp.max(g, axis=axis, keepdims=True)
        first = jnp.min(jnp.where(g == m, idx, width), axis=axis, keepdims=True)
        pick = idx == jnp.where(m > -jnp.inf, first, -1)
        sel = jnp.where(pick, 1.0, sel)
        g = jnp.where(pick, -jnp.inf, g)
    return sel


def _moba_prompt_kernel(q_ref, k_ref, v_ref, km_ref, o_ref, kb_ref, vt_ref, sel_ref):
    i = pl.program_id(2)
    blk = MOBA_BLOCK
    dh = MOBA_DH
    heads = q_ref.shape[-1] // dh
    hs = [slice(j * dh, (j + 1) * dh) for j in range(heads)]

    @pl.when(i == 0)
    def _():
        kb_ref[...] = k_ref[...].astype(BF16)

        def transpose_block(n, carry):
            r0 = pl.multiple_of(n * blk, blk)
            vt_ref[:, pl.ds(r0, blk)] = v_ref[pl.ds(r0, blk), :].T.astype(BF16)
            return carry

        lax.fori_loop(0, k_ref.shape[0] // blk, transpose_block, 0)

    qbs = []
    for j in range(heads):
        q = q_ref[:, hs[j]]
        gscore = _dot_nt(km_ref[:, hs[j]], q, precision=HIGHEST)
        blk_idx = lax.broadcasted_iota(jnp.int32, gscore.shape, 0)
        sel_ref[j] = _select_top_blocks(gscore, i, blk_idx, 0)
        qbs.append((q * (dh ** -0.5)).astype(BF16))

    def scores(j, n):
        return _dot_nt(kb_ref[pl.ds(pl.multiple_of(n * blk, blk), blk), hs[j]], qbs[j])

    def values(j, n, p):
        return _dot(vt_ref[hs[j], pl.ds(pl.multiple_of(n * blk, blk), blk)], p)

    def softmax_step(s, m, l, acc, pv):
        m_new = jnp.maximum(m, jnp.max(s, axis=0, keepdims=True))
        alpha = jnp.exp(m - m_new)
        p = jnp.exp(s - m_new)
        l = alpha * l + jnp.sum(p, axis=0, keepdims=True)
        return m_new, l, alpha * (acc + pv), p.astype(BF16)

    def past_block(n, carry):
        out = []
        for j in range(heads):
            m, l, acc, s, p_prev = carry[j]
            pv = values(j, jnp.maximum(n - 1, 0), p_prev)
            s_next = scores(j, n + 1)
            bias = jnp.where(sel_ref[j, pl.ds(n, 1), :] > 0.5, 0.0, NEG)
            m, l, acc, p = softmax_step(s + bias, m, l, acc, pv)
            out.append((m, l, acc, s_next, p))
        return tuple(out)

    init = tuple((jnp.full((1, blk), NEG, F32), jnp.zeros((1, blk), F32), jnp.zeros((dh, blk), F32),
                  scores(j, 0), jnp.zeros((blk, blk), BF16)) for j in range(heads))
    carry = lax.fori_loop(0, i, past_block, init)
    kpos = lax.broadcasted_iota(jnp.int32, (blk, blk), 0)
    qpos = lax.broadcasted_iota(jnp.int32, (blk, blk), 1)
    for j in range(heads):
        m, l, acc, s, p_prev = carry[j]
        pv = values(j, jnp.maximum(i - 1, 0), p_prev)
        m, l, acc, p = softmax_step(jnp.where(kpos <= qpos, s, NEG), m, l, acc, pv)
        acc = acc + values(j, i, p)
        o_ref[:, hs[j]] = (acc / l).T


def _moba_prompt(proj3, heads_per_step=2):
    bsz, t, _ = proj3.shape
    nb = t // MOBA_BLOCK
    km = _block_key_means(proj3)
    w = heads_per_step * MOBA_DH
    hq, hk, hv = COL_QC // w, COL_KC // w, COL_VC // w
    once = pl.Buffered(1)
    return pl.pallas_call(
        _moba_prompt_kernel,
        grid=(bsz, MOBA_HEADS // heads_per_step, nb),
        in_specs=[pl.BlockSpec((None, MOBA_BLOCK, w), lambda b, h, i: (b, i, hq + h)),
                  pl.BlockSpec((None, t, w), lambda b, h, i: (b, 0, hk + h), pipeline_mode=once),
                  pl.BlockSpec((None, t, w), lambda b, h, i: (b, 0, hv + h), pipeline_mode=once),
                  pl.BlockSpec((None, nb, w), lambda b, h, i: (b, 0, h))],
        out_specs=pl.BlockSpec((None, MOBA_BLOCK, w), lambda b, h, i: (b, i, h)),
        out_shape=jax.ShapeDtypeStruct((bsz, t, MOBA_WIDTH), F32),
        scratch_shapes=[pltpu.VMEM((t, w), BF16), pltpu.VMEM((w, t), BF16),
                        pltpu.VMEM((heads_per_step, nb, MOBA_BLOCK), F32)],
        compiler_params=_cparams("parallel", "parallel", "arbitrary"),
        name="moba_prompt",
    )(proj3, proj3, proj3, km)


def _moba_sample_kernel(pt_ref, q_ref, kn_ref, vn_ref, seg_ref, *refs, n_pages, seqs):
    npg = seqs * n_pages
    o_ref = refs[2 * npg]
    s_ref, km_ref = refs[2 * npg + 1], refs[2 * npg + 2]
    nh = MOBA_HEADS
    prow = PAGE_SIZE * nh
    pages_per_block = MOBA_BLOCK // PAGE_SIZE
    n_blocks = n_pages // pages_per_block
    sub = 8
    nq = q_ref.shape[1]
    q_head = lax.broadcasted_iota(jnp.int32, (nq, 1), 0) % nh
    q_tok = lax.broadcasted_iota(jnp.int32, (nq, 1), 0) // nh
    own_head = lax.broadcasted_iota(jnp.int32, (nq, prow), 1) % nh == q_head
    for sq in range(seqs):
        k_pages = refs[sq * n_pages:(sq + 1) * n_pages]
        v_pages = refs[npg + sq * n_pages:npg + (sq + 1) * n_pages]
        q = q_ref[sq]
        for n in range(n_blocks):
            tot = jnp.sum(k_pages[n * pages_per_block][...].reshape(prow // sub, sub, MOBA_DH), axis=0)
            for j in range(1, pages_per_block):
                tot = tot + jnp.sum(k_pages[n * pages_per_block + j][...].reshape(prow // sub, sub, MOBA_DH), axis=0)
            km_ref[sq, n * sub:(n + 1) * sub, :] = tot
        g_all = _dot_nt(q, km_ref[sq], precision=HIGHEST)
        g_lane = lax.broadcasted_iota(jnp.int32, g_all.shape, 1)
        g_own = jnp.where(g_lane % nh == q_head, g_all, 0.0)
        gscore = _dot(g_own, seg_ref[...], precision=HIGHEST) * (1.0 / MOBA_BLOCK)
        blk_idx = lax.broadcasted_iota(jnp.int32, gscore.shape, 1)
        sel = _select_top_blocks(gscore, n_blocks, blk_idx, 1)
        qb = (q * (MOBA_DH ** -0.5)).astype(BF16)
        k_all = jnp.concatenate([k_pages[j][...].astype(BF16) for j in range(n_pages)], axis=0)
        s_all = _dot_nt(qb, k_all)
        for j in range(n_pages):
            nblk = j // pages_per_block
            keep = jnp.where(own_head, sel[:, nblk:nblk + 1], 0.0) > 0.5
            s_ref[sq, :, j * prow:(j + 1) * prow] = jnp.where(keep, s_all[:, j * prow:(j + 1) * prow], NEG)
        s_new = _dot_nt(qb, kn_ref[sq].astype(BF16))
        coln = lax.broadcasted_iota(jnp.int32, s_new.shape, 1)
        ok_new = jnp.where(coln % nh == q_head, coln // nh, nq) <= q_tok
        s_new = jnp.where(ok_new, s_new, NEG)
        s_past = s_ref[sq]
        m = jnp.maximum(jnp.max(s_past, axis=-1, keepdims=True), jnp.max(s_new, axis=-1, keepdims=True))
        p_past = jnp.exp(s_past - m)
        p_new = jnp.exp(s_new - m)
        l = jnp.sum(p_past, axis=-1, keepdims=True) + jnp.sum(p_new, axis=-1, keepdims=True)
        v_all = jnp.concatenate([v_pages[j][...].astype(BF16) for j in range(n_pages)], axis=0)
        o = _dot(p_new.astype(BF16), vn_ref[sq].astype(BF16)) + _dot(p_past.astype(BF16), v_all)
        o_ref[sq] = o / l


def _moba_sample(q, k_new, v_new, cache_k, cache_v, page_table, layer, seqs):
    bsz, nq, dh = q.shape
    depth, n_pool = cache_k.shape[:2]
    n_pages = page_table.shape[1]
    prow = PAGE_SIZE * MOBA_HEADS
    ck = cache_k.reshape(depth, n_pool, prow, dh)
    cv = cache_v.reshape(depth, n_pool, prow, dh)
    n_blocks = n_pages * PAGE_SIZE // MOBA_BLOCK
    seg = (jnp.arange(n_blocks * 8)[:, None] // 8 == jnp.arange(n_blocks)[None, :]).astype(F32)
    assert bsz % seqs == 0

    def page_spec(sq, j):
        return pl.BlockSpec((None, None, prow, dh), lambda b, pt: (layer, pt[(b * seqs + sq) * n_pages + j], 0, 0))

    pages = [page_spec(sq, j) for sq in range(seqs) for j in range(n_pages)]
    grid_spec = pltpu.PrefetchScalarGridSpec(
        num_scalar_prefetch=1,
        grid=(bsz // seqs,),
        in_specs=([pl.BlockSpec((seqs, nq, dh), lambda b, pt: (b, 0, 0))] * 3
                  + [pl.BlockSpec(seg.shape, lambda b, pt: (0, 0))] + pages * 2),
        out_specs=pl.BlockSpec((seqs, nq, dh), lambda b, pt: (b, 0, 0)),
        scratch_shapes=[pltpu.VMEM((seqs, nq, n_pages * prow), F32), pltpu.VMEM((seqs, n_blocks * 8, dh), F32)],
    )
    return pl.pallas_call(
        functools.partial(_moba_sample_kernel, n_pages=n_pages, seqs=seqs),
        grid_spec=grid_spec,
        out_shape=jax.ShapeDtypeStruct((bsz, nq, dh), F32),
        compiler_params=_cparams("arbitrary"),
        name="moba_sample",
    )(page_table.reshape(-1), q, k_new, v_new, seg, *([ck] * (seqs * n_pages)), *([cv] * (seqs * n_pages)))


def _mix_kernel(ya_ref, ob_ref, oc_ref, od_ref, gl_ref, x_ref, g1_ref, wglu_ref, wb_ref, wc_ref, wd_ref, wo_ref,
                lng_ref, lnb_ref, o_ref):
    d = D_MODEL
    z = _dot(ya_ref[...].astype(BF16), wglu_ref[...])
    merged = jax.nn.sigmoid(gl_ref[:, 0:d]) * (z[:, 0:d] * jax.nn.sigmoid(z[:, d:2 * d]))
    merged += jax.nn.sigmoid(gl_ref[:, d:2 * d]) * _dot(ob_ref[...].astype(BF16), wb_ref[...])
    merged += jax.nn.sigmoid(gl_ref[:, 2 * d:3 * d]) * _dot(oc_ref[...].astype(BF16), wc_ref[...])
    merged += jax.nn.sigmoid(gl_ref[:, 3 * d:4 * d]) * _dot(od_ref[...].astype(BF16), wd_ref[...])
    mix = _dot(merged.astype(BF16), wo_ref[...])
    o_ref[...] = _layer_norm(DEEPNORM_ALPHA * x_ref[...] + g1_ref[...] * mix, lng_ref[...], lnb_ref[...])


def _mix(ya, ob, oc, od, gates, x, mod, rows_per_seq, wl, tm):
    n, d = x.shape
    w512 = SSM_WIDTH
    tok = lambda width, col=0: pl.BlockSpec((tm, width), lambda i: (i, col))
    const = lambda r, c: pl.BlockSpec((r, c), lambda i: (0, 0))
    return pl.pallas_call(
        _mix_kernel,
        grid=(n // tm,),
        in_specs=[tok(w512), tok(w512), tok(w512), tok(w512), tok(GATE_COLS), tok(d),
                  _mod_spec(mod, tm, rows_per_seq, 2),
                  const(w512, 2 * d), const(w512, d), const(w512, d), const(w512, d), const(d, d),
                  const(1, d), const(1, d)],
        out_specs=tok(d),
        out_shape=jax.ShapeDtypeStruct((n, d), F32),
        compiler_params=_cparams("parallel"),
    )(ya, ob, oc, od, gates, x, mod, wl["w_glu"], wl["w_hgrn"], wl["w_moba"], wl["w_gla"], wl["w_out"],
      wl["ln1_g"], wl["ln1_b"])


def _first_max(vals, lane, width):
    m = jnp.max(vals, axis=-1, keepdims=True)
    first = jnp.min(jnp.where(vals == m, lane, width), axis=-1, keepdims=True)
    return m, first


def _route(h2, wr_ref, br_ref):
    logits = _dot(h2, wr_ref[...], precision=HIGHEST)
    lane = lax.broadcasted_iota(jnp.int32, logits.shape, 1)
    width = logits.shape[-1]
    scores = jax.nn.sigmoid(logits)
    biased = jnp.where(lane < N_EXPERTS, scores + br_ref[...], -jnp.inf)
    group = lane // EXPERTS_PER_GROUP
    best = jnp.zeros((logits.shape[0], 1), jnp.int32)
    best_score = None
    for gidx in range(N_EXPERT_GROUPS):
        vals = jnp.where(group == gidx, biased, -jnp.inf)
        m1, i1 = _first_max(vals, lane, width)
        m2 = jnp.max(jnp.where(lane == i1, -jnp.inf, vals), axis=-1, keepdims=True)
        gs = m1 + m2
        if best_score is None:
            best_score = gs
        else:
            better = gs > best_score
            best = jnp.where(better, gidx, best)
            best_score = jnp.where(better, gs, best_score)
    masked = jnp.where(group == best, biased, -jnp.inf)
    _, i1 = _first_max(masked, lane, width)
    _, i2 = _first_max(jnp.where(lane == i1, -jnp.inf, masked), lane, width)
    w1 = jnp.sum(jnp.where(lane == i1, scores, 0.0), axis=-1, keepdims=True)
    w2 = jnp.sum(jnp.where(lane == i2, scores, 0.0), axis=-1, keepdims=True)
    tot = w1 + w2
    return jnp.where(lane == i1, w1 / tot, 0.0) + jnp.where(lane == i2, w2 / tot, 0.0)


def _moe_kernel(x_ref, sh_ref, sc_ref, g2_ref, wr_ref, br_ref, wg_ref, wu_ref, wd_ref, lng_ref, lnb_ref, o_ref,
                h_ref, comb_ref, acc_ref, *, experts_per_step):
    e = pl.program_id(1)

    @pl.when(e == 0)
    def _():
        h2 = x_ref[...] * (1.0 + sc_ref[...]) + sh_ref[...]
        h_ref[...] = h2.astype(BF16)
        comb_ref[...] = _route(h2, wr_ref, br_ref)
        acc_ref[...] = jnp.zeros(acc_ref.shape, F32)

    hb = h_ref[...]
    lane = lax.broadcasted_iota(jnp.int32, comb_ref.shape, 1)
    for j in range(experts_per_step):
        eid = e * experts_per_step + j
        cw = jnp.sum(jnp.where(lane == eid, comb_ref[...], 0.0), axis=-1, keepdims=True)
        hid = _silu(_dot(hb, wg_ref[j])) * _dot(hb, wu_ref[j]) * cw
        acc_ref[...] += _dot(hid.astype(BF16), wd_ref[j])

    @pl.when(e == pl.num_programs(1) - 1)
    def _():
        o_ref[...] = _layer_norm(DEEPNORM_ALPHA * x_ref[...] + g2_ref[...] * acc_ref[...], lng_ref[...], lnb_ref[...])


def _moe(x, mod, rows_per_seq, wl, w_router, b_router, tm, experts_per_step=4):
    n, d = x.shape
    nh = EXPERT_HIDDEN
    tok = pl.BlockSpec((tm, d), lambda i, e: (i, 0))
    const = lambda r, c: pl.BlockSpec((r, c), lambda i, e: (0, 0))
    return pl.pallas_call(
        functools.partial(_moe_kernel, experts_per_step=experts_per_step),
        grid=(n // tm, N_EXPERTS // experts_per_step),
        in_specs=[tok, _mod_spec(mod, tm, rows_per_seq, 3), _mod_spec(mod, tm, rows_per_seq, 4),
                  _mod_spec(mod, tm, rows_per_seq, 5), const(d, LANES), const(1, LANES),
                  pl.BlockSpec((experts_per_step, d, nh), lambda i, e: (e, 0, 0)),
                  pl.BlockSpec((experts_per_step, d, nh), lambda i, e: (e, 0, 0)),
                  pl.BlockSpec((experts_per_step, nh, d), lambda i, e: (e, 0, 0)),
                  const(1, d), const(1, d)],
        out_specs=tok,
        out_shape=jax.ShapeDtypeStruct((n, d), F32),
        scratch_shapes=[pltpu.VMEM((tm, d), BF16), pltpu.VMEM((tm, LANES), F32), pltpu.VMEM((tm, d), F32)],
        compiler_params=_cparams("parallel", "arbitrary"),
    )(x, mod, mod, mod, w_router, b_router, wl["moe_gate"], wl["moe_up"], wl["moe_down"], wl["ln2_g"], wl["ln2_b"])


def _token_mixers_prompt(proj, k_rows, v_rows, bsz, t, wl):
    proj3 = proj.reshape(bsz, t, IN_COLS_PAD)
    ya, ssm_re, ssm_im = _ssm_prompt(proj3, wl["ssm"])
    ob, hgrn_s = _lin_rec("hgrn", proj3, (COL_QB, COL_FB, COL_IB, COL_GB), wl["hgrn_extras"], wl["hgrn_norm"],
                          None, 0, REC_CHUNK_PROMPT)
    od, gla_s = _lin_rec("gla", proj3, (COL_QD, COL_KD, COL_VD, COL_GD), wl["gla_extras"], wl["gla_norm"],
                         None, 0, REC_CHUNK_PROMPT)
    oc = _moba_prompt(proj3)
    n = bsz * t
    k_new = k_rows.reshape(bsz, t, MOBA_HEADS, MOBA_DH)
    v_new = v_rows.reshape(bsz, t, MOBA_HEADS, MOBA_DH)
    return (ya.reshape(n, -1), ob.reshape(n, -1), oc.reshape(n, -1), od.reshape(n, -1),
            (k_new, v_new, ssm_re, ssm_im, hgrn_s, gla_s))


def _token_mixers_sample(proj, k_rows, v_rows, bsz, t, wl, layer, cache_k, cache_v, page_table, st_re, st_im,
                         st_hgrn, st_gla):
    proj3 = proj.reshape(bsz, t, IN_COLS_PAD)
    u_tm = proj3[:, :, COL_UA:COL_UA + SSM_WIDTH].transpose(1, 0, 2)
    ns = SSM_GROUPS * SSM_STATE
    ya_tm, xr, xi = _ssm_sample(u_tm, st_re[layer].reshape(bsz, ns), st_im[layer].reshape(bsz, ns), wl["ssm"])
    ya = ya_tm.transpose(1, 0, 2)
    seqs = math.gcd(bsz, REC_SAMPLE_SEQS)
    ob, hgrn_s = _lin_rec("hgrn", proj3, (COL_QB, COL_FB, COL_IB, COL_GB), wl["hgrn_extras"], wl["hgrn_norm"],
                          st_hgrn, layer, SUB_CHUNK, seqs)
    od, gla_s = _lin_rec("gla", proj3, (COL_QD, COL_KD, COL_VD, COL_GD), wl["gla_extras"], wl["gla_norm"],
                         st_gla, layer, SUB_CHUNK, seqs)
    nq = t * MOBA_HEADS
    q = proj3[:, :, COL_QC:COL_QC + MOBA_WIDTH].reshape(bsz, nq, MOBA_DH)
    oc = _moba_sample(q, k_rows.reshape(bsz, nq, MOBA_DH), v_rows.reshape(bsz, nq, MOBA_DH), cache_k, cache_v,
                      page_table, layer, math.gcd(bsz, MOBA_SAMPLE_SEQS))
    n = bsz * t
    return (ya.reshape(n, -1), ob.reshape(n, -1), oc.reshape(n, -1), od.reshape(n, -1),
            (k_rows.reshape(bsz, t, MOBA_HEADS, MOBA_DH), v_rows.reshape(bsz, t, MOBA_HEADS, MOBA_DH),
             xr.reshape(bsz, SSM_GROUPS, SSM_STATE), xi.reshape(bsz, SSM_GROUPS, SSM_STATE), hgrn_s, gla_s))


def _layer_weights(l, w_in_p, ssm_tabs, lower_bounds, gla_w_gk2, gla_b_gk, hgrn_norm, gla_norm, ssm_w_glu,
                   hgrn_w_proj, moba_w_proj, gla_w_proj, w_out, ln1_g, ln1_b, ln2_g, ln2_b, moe_w_gate, moe_w_up,
                   moe_w_down):
    lb = lower_bounds[l].reshape(1, -1)
    wgk = jnp.pad(gla_w_gk2[l], ((0, LANES - GLA_RANK), (0, 0))).astype(BF16)
    row = lambda a: a.reshape(1, -1)
    return dict(
        w_in=w_in_p[l], ssm={k: v[l] for k, v in ssm_tabs.items()},
        hgrn_extras=(jnp.log(lb), jnp.log1p(-lb), 1.0 - lb), hgrn_norm=hgrn_norm[l],
        gla_extras=(wgk, row(gla_b_gk[l])), gla_norm=gla_norm[l],
        w_glu=ssm_w_glu[l].astype(BF16), w_hgrn=hgrn_w_proj[l].astype(BF16), w_moba=moba_w_proj[l].astype(BF16),
        w_gla=gla_w_proj[l].astype(BF16), w_out=w_out[l].astype(BF16),
        ln1_g=row(ln1_g[l]), ln1_b=row(ln1_b[l]), ln2_g=row(ln2_g[l]), ln2_b=row(ln2_b[l]),
        moe_gate=moe_w_gate[l].astype(BF16), moe_up=moe_w_up[l].astype(BF16), moe_down=moe_w_down[l].astype(BF16))


def kernel(x_prompt, x_sample, cache_k, cache_v, state_ssm_re, state_ssm_im, state_hgrn, state_gla, page_table,
           c_prompt, c_sample, ln_in_g, ln_in_b, w_ada, b_ada, w_in, ssm_lam_re, ssm_lam_im, ssm_log_step,
           ssm_b_re, ssm_b_im, ssm_c_re, ssm_c_im, ssm_d, ssm_w_glu, hgrn_lb, hgrn_norm, hgrn_w_proj, moba_w_proj,
           gla_w_gk2, gla_b_gk, gla_norm, gla_w_proj, w_out, ln1_g, ln1_b, ln2_g, ln2_b, w_router, b_router,
           moe_w_gate, moe_w_up, moe_w_down):
    bp, tp, d = x_prompt.shape
    bs, ts, _ = x_sample.shape
    depth = w_in.shape[0]
    n_pool = cache_k.shape[1]
    np_tok, ns_tok = bp * tp, bs * ts
    tm_p = min(1024, tp)
    tm_s = min(512, ns_tok)

    lb_cum = jnp.cumsum(jax.nn.softmax(hgrn_lb.astype(F32), axis=0), axis=0)
    lower_bounds = lb_cum - lb_cum[0:1]
    w_in_p = jnp.concatenate(
        [w_in[:, :, REF_GATE_START:IN_COLS_REF], w_in[:, :, :REF_RD_START], w_in[:, :, REF_RD_START:REF_GATE_START],
         jnp.zeros((depth, d, GATE_COLS + IN_COLS_PAD - IN_COLS_REF), w_in.dtype)], axis=-1).astype(BF16)
    n_scan = max(tp // SSM_CHUNK - 1, 0).bit_length()
    ssm_tabs = jax.vmap(functools.partial(_ssm_tables, n_scan=n_scan))(
        ssm_lam_re, ssm_lam_im, ssm_log_step, ssm_b_re, ssm_b_im, ssm_c_re, ssm_c_im, ssm_d)
    w_router_p = jnp.pad(w_router, ((0, 0), (0, LANES - N_EXPERTS)))
    b_router_p = jnp.pad(b_router, (0, LANES - N_EXPERTS)).reshape(1, LANES)

    nc = bs + bp
    nc_pad = -(-nc // 8) * 8
    c_all = jnp.pad(jnp.concatenate([c_sample, c_prompt], axis=0), ((0, nc_pad - nc), (0, 0)))
    mod_all = _ada_all(c_all, w_ada, b_ada)

    xp = _ln_rows(x_prompt.reshape(np_tok, d), ln_in_g, ln_in_b, tm_p)
    xs = _ln_rows(x_sample.reshape(ns_tok, d), ln_in_g, ln_in_b, tm_s)

    outs_p, outs_s = [], []
    for l in range(depth):
        wl = _layer_weights(l, w_in_p, ssm_tabs, lower_bounds, gla_w_gk2, gla_b_gk, hgrn_norm, gla_norm, ssm_w_glu,
                            hgrn_w_proj, moba_w_proj, gla_w_proj, w_out, ln1_g, ln1_b, ln2_g, ln2_b, moe_w_gate,
                            moe_w_up, moe_w_down)
        mod_p = mod_all[l, bs:bs + bp].reshape(bp, 1, 6 * d)
        mod_s = jnp.repeat(mod_all[l, :bs], ts, axis=0)

        proj_p, gates, k_rows, v_rows = _in_proj(xp, mod_p, tp, wl["w_in"], tm_p)
        ya, ob, oc, od, st_p = _token_mixers_prompt(proj_p, k_rows, v_rows, bp, tp, wl)
        xp = _mix(ya, ob, oc, od, gates, xp, mod_p, tp, wl, min(256, tp))
        xp = _moe(xp, mod_p, tp, wl, w_router_p, b_router_p, tm_p)
        outs_p.append(st_p)

        proj_s, gates, k_rows, v_rows = _in_proj(xs, mod_s, ts, wl["w_in"], tm_s)
        ya, ob, oc, od, st_s = _token_mixers_sample(proj_s, k_rows, v_rows, bs, ts, wl, l, cache_k, cache_v,
                                                    page_table, state_ssm_re, state_ssm_im, state_hgrn, state_gla)
        xs = _mix(ya, ob, oc, od, gates, xs, mod_s, ts, wl, min(256, ns_tok))
        xs = _moe(xs, mod_s, ts, wl, w_router_p, b_router_p, tm_s)
        outs_s.append(st_s)

    stack = lambda outs, idx: jnp.stack([o[idx] for o in outs])
    return (xp.reshape(bp, tp, d), xs.reshape(bs, ts, d),
            stack(outs_p, 0), stack(outs_p, 1), stack(outs_s, 0), stack(outs_s, 1),
            stack(outs_p, 2), stack(outs_p, 3), stack(outs_s, 2), stack(outs_s, 3),
            stack(outs_p, 4), stack(outs_s, 4), stack(outs_p, 5), stack(outs_s, 5))
```

```python
import functools
import math

import jax
import jax.numpy as jnp
from jax import lax
from jax.experimental import pallas as pl
from jax.experimental.pallas import tpu as pltpu

F32 = jnp.float32
BF16 = jnp.bfloat16
HIGHEST = lax.Precision.HIGHEST

D_MODEL = 1024
DEPTH = 4
PAGE_SIZE = 128
SSM_WIDTH = 512
SSM_GROUP = 16
SSM_GROUPS = 32
SSM_STATE = 64
SSM_MAX_RE = -1e-4
SSM_CHUNK = 16
SSM_QUAD_GROUPS = 8
HGRN_HEADS = 4
HGRN_DK = 128
HGRN_DV = 128
MOBA_HEADS = 4
MOBA_DH = 128
MOBA_WIDTH = 512
MOBA_BLOCK = 256
MOBA_TOPK = 3
GLA_HEADS = 4
GLA_DK = 64
GLA_DV = 128
GLA_RANK = 16
GLA_TAU = 16.0
N_BRANCH = 4
N_EXPERTS = 16
N_EXPERT_GROUPS = 4
EXPERTS_PER_GROUP = 4
EXPERT_HIDDEN = 256
DEEPNORM_ALPHA = (2 * DEPTH) ** 0.25
NORM_EPS = 1e-5

LANES = 128
SUB_CHUNK = 16
REC_CHUNK_PROMPT = 128
REC_SAMPLE_SEQS = 4
MOBA_SAMPLE_SEQS = 1
VMEM_LIMIT = 56 * 1024 * 1024
NEG = -1e30

COL_GATES = 0
COL_UA = 4096
COL_QB = 4608
COL_FB = 5120
COL_IB = 5632
COL_GB = 6144
COL_QC = 6656
COL_KC = 7168
COL_VC = 7680
COL_QD = 8192
COL_KD = 8448
COL_VD = 8704
COL_GD = 9216
COL_RD = 9728
IN_COLS_PAD = 10240
IN_COLS_REF = 9744
REF_GATE_START = 5648
REF_RD_START = 5632


def _cparams(*sem):
    return pltpu.CompilerParams(dimension_semantics=sem, vmem_limit_bytes=VMEM_LIMIT)


def _silu(x):
    return x * jax.nn.sigmoid(x)


def _log_sigmoid(x):
    return jnp.minimum(x, 0.0) - jnp.log1p(jnp.exp(-jnp.abs(x)))


def _dot(a, b, **kw):
    return jnp.dot(a, b, preferred_element_type=F32, **kw)


def _dot_nt(a, b, **kw):
    return lax.dot_general(a, b, (((1,), (1,)), ((), ())), preferred_element_type=F32, **kw)


def _dot_tn(a, b, **kw):
    return lax.dot_general(a, b, (((0,), (0,)), ((), ())), preferred_element_type=F32, **kw)


def _layer_norm(x, g, b):
    mu = jnp.mean(x, axis=-1, keepdims=True)
    xc = x - mu
    var = jnp.mean(xc * xc, axis=-1, keepdims=True)
    return xc * lax.rsqrt(var + NORM_EPS) * g + b


def _ln_kernel(x_ref, g_ref, b_ref, o_ref):
    o_ref[...] = _layer_norm(x_ref[...], g_ref[...], b_ref[...])


def _ln_rows(x, g, b, tm):
    n, d = x.shape
    return pl.pallas_call(
        _ln_kernel,
        grid=(n // tm,),
        in_specs=[pl.BlockSpec((tm, d), lambda i: (i, 0)),
                  pl.BlockSpec((1, d), lambda i: (0, 0)),
                  pl.BlockSpec((1, d), lambda i: (0, 0))],
        out_specs=pl.BlockSpec((tm, d), lambda i: (i, 0)),
        out_shape=jax.ShapeDtypeStruct((n, d), F32),
        compiler_params=_cparams("parallel"),
    )(x, g.reshape(1, d), b.reshape(1, d))


def _ada_kernel(c_ref, w_ref, b_ref, o_ref):
    h = _silu(c_ref[...]).astype(BF16)
    o_ref[...] = _dot(h, w_ref[...].astype(BF16)) + b_ref[...]


def _ada_all(c, w_ada, b_ada, tn=1536):
    nb, d = c.shape
    depth, _, n6 = w_ada.shape
    return pl.pallas_call(
        _ada_kernel,
        grid=(depth, n6 // tn),
        in_specs=[pl.BlockSpec((nb, d), lambda l, j: (0, 0)),
                  pl.BlockSpec((None, d, tn), lambda l, j: (l, 0, j)),
                  pl.BlockSpec((None, 1, tn), lambda l, j: (l, 0, j))],
        out_specs=pl.BlockSpec((None, nb, tn), lambda l, j: (l, 0, j)),
        out_shape=jax.ShapeDtypeStruct((depth, nb, n6), F32),
        compiler_params=_cparams("parallel", "parallel"),
    )(c, w_ada, b_ada.reshape(depth, 1, n6))


def _mod_spec(mod, tm, rows_per_seq, chunk):
    if mod.ndim == 3:
        tiles_per_seq = rows_per_seq // tm
        return pl.BlockSpec((None, 1, D_MODEL), lambda i, *_: (i // tiles_per_seq, 0, chunk))
    return pl.BlockSpec((tm, D_MODEL), lambda i, *_: (i, chunk))


def _inproj_kernel(x_ref, sh_ref, sc_ref, w_ref, o_ref, k_ref, v_ref, h_ref, *, kv_tile):
    j = pl.program_id(1)

    @pl.when(j == 0)
    def _():
        h_ref[...] = (x_ref[...] * (1.0 + sc_ref[...]) + sh_ref[...]).astype(BF16)

    acc = _dot(h_ref[...], w_ref[...])
    o_ref[...] = acc

    @pl.when(j == kv_tile)
    def _():
        tm = acc.shape[0]
        for h in range(MOBA_HEADS):
            k_ref[pl.ds(h, tm, stride=MOBA_HEADS), :] = acc[:, h * MOBA_DH:(h + 1) * MOBA_DH]
            v_ref[pl.ds(h, tm, stride=MOBA_HEADS), :] = acc[:, MOBA_WIDTH + h * MOBA_DH:MOBA_WIDTH + (h + 1) * MOBA_DH]


def _in_proj(x, mod, rows_per_seq, w_all, layer, tm, tn=1024):
    n, d = x.shape
    ncol = w_all.shape[-1]
    assert COL_KC % tn == 0 and COL_VC == COL_KC + MOBA_WIDTH and tn == 2 * MOBA_WIDTH
    kv_spec = pl.BlockSpec((tm * MOBA_HEADS, MOBA_DH), lambda i, j: (i, 0))
    kv_shape = jax.ShapeDtypeStruct((n * MOBA_HEADS, MOBA_DH), F32)
    return pl.pallas_call(
        functools.partial(_inproj_kernel, kv_tile=COL_KC // tn),
        grid=(n // tm, ncol // tn),
        in_specs=[pl.BlockSpec((tm, d), lambda i, j: (i, 0)),
                  _mod_spec(mod, tm, rows_per_seq, 0),
                  _mod_spec(mod, tm, rows_per_seq, 1),
                  pl.BlockSpec((None, d, tn), lambda i, j: (layer, 0, j))],
        out_specs=[pl.BlockSpec((tm, tn), lambda i, j: (i, j)), kv_spec, kv_spec],
        out_shape=[jax.ShapeDtypeStruct((n, ncol), F32), kv_shape, kv_shape],
        scratch_shapes=[pltpu.VMEM((tm, d), BF16)],
        compiler_params=_cparams("parallel", "arbitrary"),
        name="in_proj",
    )(x, mod, mod, w_all)


def _ssm_prompt_kernel(u_ref, toep_ref, inre_ref, inim_ref, outre_ref, outim_ref, d_ref, are_ref, aim_ref,
                       y_ref, sre_ref, sim_ref, ucat_ref, *, n_steps):
    nrow = ucat_ref.shape[0]
    for tau in range(SSM_CHUNK):
        ucat_ref[:, tau * LANES:(tau + 1) * LANES] = u_ref[pl.ds(tau, nrow, stride=SSM_CHUNK), :].astype(BF16)
    ucat = ucat_ref[...]
    xr = _dot(ucat, inre_ref[...])
    xi = _dot(ucat, inim_ref[...])
    row = lax.broadcasted_iota(jnp.int32, xr.shape, 0)
    for k in range(n_steps):
        dist = 1 << k
        ar = are_ref[k:k + 1, :]
        ai = aim_ref[k:k + 1, :]
        keep = row >= dist
        sr = jnp.where(keep, pltpu.roll(xr, dist, 0), 0.0)
        si = jnp.where(keep, pltpu.roll(xi, dist, 0), 0.0)
        xr, xi = xr + (ar * sr - ai * si), xi + (ar * si + ai * sr)
    keep = row >= 1
    pr = jnp.where(keep, pltpu.roll(xr, 1, 0), 0.0)
    pi = jnp.where(keep, pltpu.roll(xi, 1, 0), 0.0)
    out = (_dot(ucat, toep_ref[...]) + _dot(pr.astype(BF16), outre_ref[...])
           + _dot(pi.astype(BF16), outim_ref[...]))
    for tau in range(SSM_CHUNK):
        u_tau = u_ref[pl.ds(tau, nrow, stride=SSM_CHUNK), :]
        y_ref[pl.ds(tau, nrow, stride=SSM_CHUNK), :] = jax.nn.gelu(out[:, tau * LANES:(tau + 1) * LANES]
                                                                   + d_ref[...] * u_tau)
    sre_ref[...] = xr[nrow - 1:nrow, :]
    sim_ref[...] = xi[nrow - 1:nrow, :]


def _ssm_prompt(proj3, sp, layer):
    bsz, t, _ = proj3.shape
    nrow = t // SSM_CHUNK
    n_steps = max(nrow - 1, 0).bit_length()
    nquad = SSM_WIDTH // LANES
    kw = SSM_CHUNK * LANES
    ns = SSM_QUAD_GROUPS * SSM_STATE
    once = pl.Buffered(1)
    table = lambda r, c, **kw_: pl.BlockSpec((None, None, r, c), lambda q, b: (layer, q, 0, 0), **kw_)
    y, sre, sim = pl.pallas_call(
        functools.partial(_ssm_prompt_kernel, n_steps=n_steps),
        grid=(nquad, bsz),
        in_specs=[pl.BlockSpec((None, t, LANES), lambda q, b: (b, 0, COL_UA // LANES + q)),
                  table(kw, kw, pipeline_mode=once),
                  table(kw, ns, pipeline_mode=once), table(kw, ns, pipeline_mode=once),
                  table(ns, kw, pipeline_mode=once), table(ns, kw, pipeline_mode=once),
                  pl.BlockSpec((None, 1, LANES), lambda q, b: (layer, 0, q)),
                  table(sp["scan_re"].shape[2], ns), table(sp["scan_im"].shape[2], ns)],
        out_specs=[pl.BlockSpec((None, t, LANES), lambda q, b: (b, 0, q)),
                   pl.BlockSpec((None, None, 1, ns), lambda q, b: (b, q, 0, 0)),
                   pl.BlockSpec((None, None, 1, ns), lambda q, b: (b, q, 0, 0))],
        out_shape=[jax.ShapeDtypeStruct((bsz, t, SSM_WIDTH), F32),
                   jax.ShapeDtypeStruct((bsz, nquad, 1, ns), F32),
                   jax.ShapeDtypeStruct((bsz, nquad, 1, ns), F32)],
        scratch_shapes=[pltpu.VMEM((nrow, kw), BF16)],
        compiler_params=_cparams("arbitrary", "arbitrary"),
        name="ssm_prompt",
    )(proj3, sp["toep"], sp["in_re"], sp["in_im"], sp["out_re"], sp["out_im"], sp["d_row"], sp["scan_re"],
      sp["scan_im"])
    return y, sre.reshape(bsz, SSM_GROUPS, SSM_STATE), sim.reshape(bsz, SSM_GROUPS, SSM_STATE)


def _ssm_sample_kernel(u_ref, x0r_ref, x0i_ref, bbr_ref, bbi_ref, cr_ref, ci_ref, ar_ref, ai_ref, d_ref,
                       y_ref, xr_ref, xi_ref):
    xr = x0r_ref[...]
    xi = x0i_ref[...]
    ar = ar_ref[...]
    ai = ai_ref[...]
    for t in range(u_ref.shape[0]):
        u = u_ref[t]
        ub = u.astype(BF16)
        xr, xi = (ar * xr - ai * xi + _dot(ub, bbr_ref[...]),
                  ar * xi + ai * xr + _dot(ub, bbi_ref[...]))
        y = _dot(xr.astype(BF16), cr_ref[...]) - _dot(xi.astype(BF16), ci_ref[...]) + d_ref[...] * u
        y_ref[t] = jax.nn.gelu(y)
    xr_ref[...] = xr
    xi_ref[...] = xi


def _ssm_sample(u_tm, x0r, x0i, sp):
    t, bsz, w = u_tm.shape
    ns = SSM_GROUPS * SSM_STATE
    full = lambda *shape: pl.BlockSpec(shape, lambda i: (0,) * len(shape))
    return pl.pallas_call(
        _ssm_sample_kernel,
        grid=(1,),
        in_specs=[full(t, bsz, w), full(bsz, ns), full(bsz, ns), full(w, ns), full(w, ns), full(ns, w), full(ns, w),
                  full(1, ns), full(1, ns), full(1, w)],
        out_specs=[full(t, bsz, w), full(bsz, ns), full(bsz, ns)],
        out_shape=[jax.ShapeDtypeStruct((t, bsz, w), F32), jax.ShapeDtypeStruct((bsz, ns), F32),
                   jax.ShapeDtypeStruct((bsz, ns), F32)],
        compiler_params=_cparams("arbitrary"),
    )(u_tm, x0r, x0i, sp["bb_re"], sp["bb_im"], sp["c_re"], sp["c_im"], sp["a_re"], sp["a_im"], sp["d_row"])


def _ssm_tables(lam_re, lam_im, log_step, b_re, b_im, c_re, c_im, d, n_scan):
    hp = dict(precision=HIGHEST)
    g, p, hch = SSM_GROUPS, SSM_STATE, SSM_GROUP
    step = jnp.exp(log_step)[:, None]
    lr = jnp.minimum(lam_re, SSM_MAX_RE)
    li = lam_im
    mag = jnp.exp(lr * step)
    ab_re = mag * jnp.cos(li * step)
    ab_im = mag * jnp.sin(li * step)
    den = lr * lr + li * li
    coef_re = ((ab_re - 1.0) * lr + ab_im * li) / den
    coef_im = (ab_im * lr - (ab_re - 1.0) * li) / den
    bb_re = coef_re[..., None] * b_re - coef_im[..., None] * b_im
    bb_im = coef_re[..., None] * b_im + coef_im[..., None] * b_re

    def cmul(xr, xi, yr, yi):
        return xr * yr - xi * yi, xr * yi + xi * yr

    pows_re, pows_im = [jnp.ones_like(ab_re)], [jnp.zeros_like(ab_im)]
    for _ in range(SSM_CHUNK):
        nr, ni = cmul(pows_re[-1], pows_im[-1], ab_re, ab_im)
        pows_re.append(nr)
        pows_im.append(ni)
    pw_re = jnp.stack(pows_re)
    pw_im = jnp.stack(pows_im)
    ajb_re = pw_re[:SSM_CHUNK, :, :, None] * bb_re - pw_im[:SSM_CHUNK, :, :, None] * bb_im
    ajb_im = pw_re[:SSM_CHUNK, :, :, None] * bb_im + pw_im[:SSM_CHUNK, :, :, None] * bb_re
    klag = (jnp.einsum("jgpi,gop->jgio", ajb_re, c_re, **hp) - jnp.einsum("jgpi,gop->jgio", ajb_im, c_im, **hp))
    rev = jnp.arange(SSM_CHUNK - 1, -1, -1)
    rev_re = pw_re[rev]
    rev_im = pw_im[rev]
    p_re = (rev_re[..., None] * bb_re - rev_im[..., None] * bb_im).transpose(1, 0, 3, 2).reshape(g, SSM_CHUNK * hch, p)
    p_im = (rev_re[..., None] * bb_im + rev_im[..., None] * bb_re).transpose(1, 0, 3, 2).reshape(g, SSM_CHUNK * hch, p)
    a1_re = pw_re[1:]
    a1_im = pw_im[1:]
    cr_t = c_re.transpose(0, 2, 1)
    ci_t = c_im.transpose(0, 2, 1)
    q_re = (cr_t[None] * a1_re[..., None] - ci_t[None] * a1_im[..., None]).transpose(1, 2, 0, 3).reshape(g, p, SSM_CHUNK * hch)
    q_im = (-(cr_t[None] * a1_im[..., None] + ci_t[None] * a1_re[..., None])).transpose(1, 2, 0, 3).reshape(g, p, SSM_CHUNK * hch)
    d_chunk = jnp.broadcast_to(d[:, None, :], (g, SSM_CHUNK, hch)).reshape(g, 1, SSM_CHUNK * hch)
    sc_re, sc_im = [pw_re[SSM_CHUNK]], [pw_im[SSM_CHUNK]]
    for _ in range(max(n_scan, 1) - 1):
        nr, ni = cmul(sc_re[-1], sc_im[-1], sc_re[-1], sc_im[-1])
        sc_re.append(nr)
        sc_im.append(ni)
    scan_re = jnp.stack(sc_re, axis=1)
    scan_im = jnp.stack(sc_im, axis=1)

    nq, gq = g // SSM_QUAD_GROUPS, SSM_QUAD_GROUPS
    kw = SSM_CHUNK * gq * hch
    qw = gq * hch

    def group_mask(rows_per_group, cols_per_group):
        r = jnp.arange(gq * rows_per_group)[:, None] // rows_per_group
        c = jnp.arange(gq * cols_per_group)[None, :] // cols_per_group
        return (r == c).astype(F32)

    def replicate(width):
        return (jnp.arange(width)[:, None] == jnp.arange(gq * width)[None, :] % width).astype(F32)

    klag_r = klag.reshape(SSM_CHUNK, nq, gq * hch, hch).transpose(1, 0, 2, 3)
    lag_blk = (jnp.einsum("njro,oc->njrc", klag_r, replicate(hch), **hp) * group_mask(hch, hch)).astype(BF16)
    lag_row = lag_blk.transpose(0, 2, 1, 3).reshape(nq, qw, kw)
    lag_row = jnp.concatenate([jnp.zeros_like(lag_row), lag_row], axis=-1)
    toep_q = jnp.concatenate([lag_row[:, :, (SSM_CHUNK - ts) * qw:(2 * SSM_CHUNK - ts) * qw]
                              for ts in range(SSM_CHUNK)], axis=1)

    def in_state(m):
        m = m.reshape(nq, gq, SSM_CHUNK, hch, p).transpose(0, 2, 1, 3, 4).reshape(nq, SSM_CHUNK, qw, p)
        m = jnp.einsum("ntrp,pc->ntrc", m, replicate(p), **hp) * group_mask(hch, p)
        return m.reshape(nq, kw, gq * p)

    def state_out(m):
        m = m.reshape(nq, gq * p, SSM_CHUNK, hch)
        m = jnp.einsum("nrto,oc->nrtc", m, replicate(hch), **hp) * group_mask(p, hch)[:, None, :]
        return m.reshape(nq, gq * p, kw)

    def scan_rows(m):
        return m.reshape(nq, gq, m.shape[1], p).transpose(0, 2, 1, 3).reshape(nq, m.shape[1], gq * p)

    def dense_diag(blocks):
        eye_g = jnp.eye(g, dtype=blocks.dtype)
        return (blocks[:, :, None, :] * eye_g[:, None, :, None]).reshape(g * blocks.shape[1], g * blocks.shape[2])

    return dict(
        toep=toep_q.astype(BF16),
        in_re=in_state(p_re).astype(BF16), in_im=in_state(p_im).astype(BF16),
        out_re=state_out(q_re).astype(BF16), out_im=state_out(q_im).astype(BF16),
        scan_re=scan_rows(scan_re), scan_im=scan_rows(scan_im),
        bb_re=dense_diag(bb_re.transpose(0, 2, 1)).astype(BF16), bb_im=dense_diag(bb_im.transpose(0, 2, 1)).astype(BF16),
        c_re=dense_diag(cr_t).astype(BF16), c_im=dense_diag(ci_t).astype(BF16),
        a_re=ab_re.reshape(1, g * p), a_im=ab_im.reshape(1, g * p), d_row=d.reshape(1, g * hch))


def _rec_levels(chunk):
    return [1 << p for p in range(chunk.bit_length() - 1)]


def _rec_tables(chunk):
    t = jnp.arange(chunk)[:, None]
    u = jnp.arange(chunk)[None, :]
    tabs = [u <= t]
    for half in _rec_levels(chunk):
        if half < 8:
            tabs.append(u <= (t // (2 * half)) * (2 * half) + half - 1)
    return jnp.concatenate(tabs, axis=0).astype(BF16)


def _linrec_kernel(*refs, mode, chunk, rows, heads, dk, dv, has_s0, seqs):
    it = iter(refs)
    q_ref, k_ref, v_ref, g_ref = next(it), next(it), next(it), next(it)
    if mode == "hgrn":
        loglb_ref, log1mlb_ref, onemlb_ref = next(it), next(it), next(it)
    else:
        r_ref, wgk_ref, bgk_ref = next(it), next(it), next(it)
    normw_ref = next(it)
    s0_ref = next(it) if has_s0 else None
    tstack_ref = next(it)
    o_ref, sfin_ref = next(it), next(it)
    st_ref = next(it)
    pad_refs = [next(it) for _ in range(5)] if rows != chunk else None

    c = pl.program_id(1)
    pack = LANES // dk
    ngrp = heads // pack
    gw = pack * dk

    @pl.when(c == 0)
    def _():
        for sg in range(seqs * ngrp):
            if has_s0:
                sq0, h0 = sg // ngrp, (sg % ngrp) * pack
                st_ref[sg] = jnp.concatenate([s0_ref[sq0, h0 + hh].T for hh in range(pack)], axis=1)
            else:
                st_ref[sg] = jnp.zeros((dv, gw), F32)

    def load(ref, slot, sq):
        if pad_refs is None:
            return ref[sq]
        buf = pad_refs[slot]
        buf[sq] = jnp.zeros(buf.shape[1:], F32)
        buf[sq, 0:rows, :] = ref[sq]
        return buf[sq]

    trow = lax.broadcasted_iota(jnp.int32, (chunk, chunk), 0)
    tcol = lax.broadcasted_iota(jnp.int32, (chunk, chunk), 1)
    dxor = trow ^ tcol
    rowi = lax.broadcasted_iota(jnp.int32, (chunk, 1), 0)
    valid = rowi < rows
    levels = _rec_levels(chunk)
    tstack = tstack_ref[...]

    lane_head = lax.broadcasted_iota(jnp.int32, (1, gw), 1) // dk

    def own_lanes(x, hh):
        return x if pack == 1 else jnp.where(lane_head == hh, x, 0.0)

    for sq, grp in [(a, b) for a in range(seqs) for b in range(ngrp)]:
        if grp == 0:
            q_all, k_all, v_all, g_all = load(q_ref, 0, sq), load(k_ref, 1, sq), load(v_ref, 2, sq), load(g_ref, 3, sq)
            if mode == "gla":
                gk_all = _dot(load(r_ref, 4, sq).astype(BF16), wgk_ref[...]) + bgk_ref[...]
        sg = sq * ngrp + grp
        ksl = slice(grp * gw, (grp + 1) * gw)
        if mode == "hgrn":
            q = _silu(q_all[:, ksl])
            z = k_all[:, ksl]
            la = loglb_ref[:, ksl]
            lc = log1mlb_ref[:, ksl] + _log_sigmoid(z)
            lf = jnp.maximum(la, lc) + jnp.log1p(jnp.exp(-jnp.abs(la - lc)))
            k = onemlb_ref[:, ksl] * jax.nn.sigmoid(-z)
        else:
            q = q_all[:, ksl] * (dk ** -0.5)
            k = k_all[:, ksl]
            lf = _log_sigmoid(gk_all[:, ksl]) / GLA_TAU
        if rows != chunk:
            lf = jnp.where(valid, lf, 0.0)
        hi = lf.astype(BF16)
        rest = lf - hi.astype(F32)
        mid = rest.astype(BF16)
        lo = (rest - mid.astype(F32)).astype(BF16)
        sums = _dot(tstack, hi) + _dot(tstack, mid) + _dot(tstack, lo)
        b = sums[0:chunk]
        st = st_ref[sg]
        stb = st.astype(BF16)
        qe = q * jnp.exp(b)
        b_last = b[chunk - 1:chunk, :]
        k_dec = k * jnp.exp(b_last - b)
        qk = q * k

        atts = [jnp.broadcast_to(jnp.sum(own_lanes(qk, hh), axis=-1, keepdims=True), (chunk, chunk))
                for hh in range(pack)]
        for li, half in enumerate(levels):
            if half < 8:
                beta = sums[(li + 1) * chunk:(li + 2) * chunk]
            else:
                nblk = chunk // (2 * half)
                b3 = b.reshape(nblk, 2 * half, gw)
                beta = jnp.broadcast_to(b3[:, half - 1:half, :], (nblk, 2 * half, gw)).reshape(chunk, gw)
            e = jnp.exp(-jnp.abs(b - beta))
            upper = (rowi & half) != 0
            qh = jnp.where(upper, q * e, 0.0)
            kh = jnp.where(upper, 0.0, k * e).astype(BF16)
            for hh in range(pack):
                atts[hh] = jnp.where(dxor >= half, _dot_nt(own_lanes(qh, hh).astype(BF16), kh), atts[hh])
        st_new = st * jnp.exp(b_last)
        for hh in range(pack):
            h = grp * pack + hh
            vsl = slice(h * dv, (h + 1) * dv)
            vb = v_all[:, vsl].astype(BF16)
            att = jnp.where(trow >= tcol, atts[hh], 0.0)
            o = _dot_nt(own_lanes(qe, hh).astype(BF16), stb) + _dot(att.astype(BF16), vb)
            o = o * lax.rsqrt(jnp.mean(o * o, axis=-1, keepdims=True) + NORM_EPS) * normw_ref[...]
            o = o * _silu(g_all[:, vsl])
            o_ref[sq, :, vsl] = o[0:rows, :]
            st_new = st_new + _dot_tn(vb, own_lanes(k_dec, hh).astype(BF16))
        st_ref[sg] = st_new

    @pl.when(c == pl.num_programs(1) - 1)
    def _():
        for sg in range(seqs * ngrp):
            for hh in range(pack):
                sfin_ref[sg // ngrp, (sg % ngrp) * pack + hh] = st_ref[sg][:, hh * dk:(hh + 1) * dk].T


def _lin_rec(mode, proj3, cols, extras, normw, s0, layer, chunk, seqs=1):
    bsz, t, _ = proj3.shape
    heads, dk, dv = (HGRN_HEADS, HGRN_DK, HGRN_DV) if mode == "hgrn" else (GLA_HEADS, GLA_DK, GLA_DV)
    rows = min(chunk, t)
    nchunk = max(t // chunk, 1)
    wk, wv = heads * dk, heads * dv
    assert bsz % seqs == 0

    def col_spec(width, col):
        return pl.BlockSpec((seqs, rows, width), lambda b, c: (b, c, col // width))

    in_specs = [col_spec(wk, cols[0]), col_spec(wk, cols[1]), col_spec(wv, cols[2]), col_spec(wv, cols[3])]
    args = [proj3, proj3, proj3, proj3]
    if mode == "hgrn":
        in_specs += [pl.BlockSpec((1, wk), lambda b, c: (0, 0))] * 3
        args += list(extras)
    else:
        in_specs += [col_spec(LANES, COL_RD), pl.BlockSpec((LANES, wk), lambda b, c: (0, 0)),
                     pl.BlockSpec((1, wk), lambda b, c: (0, 0))]
        args += [proj3] + list(extras)
    in_specs.append(pl.BlockSpec((1, dv), lambda b, c: (0, 0)))
    args.append(normw.reshape(1, dv))
    has_s0 = s0 is not None
    if has_s0:
        in_specs.append(pl.BlockSpec((None, seqs, heads, dk, dv), lambda b, c: (layer, b, 0, 0, 0)))
        args.append(s0)
    tables = _rec_tables(chunk)
    in_specs.append(pl.BlockSpec(tables.shape, lambda b, c: (0, 0)))
    args.append(tables)
    scratch = [pltpu.VMEM((seqs * heads * dk // LANES, dv, LANES), F32)]
    if rows != chunk:
        scratch += [pltpu.VMEM((seqs, chunk, wk), F32), pltpu.VMEM((seqs, chunk, wk), F32),
                    pltpu.VMEM((seqs, chunk, wv), F32), pltpu.VMEM((seqs, chunk, wv), F32),
                    pltpu.VMEM((seqs, chunk, LANES), F32)]
    return pl.pallas_call(
        functools.partial(_linrec_kernel, mode=mode, chunk=chunk, rows=rows, heads=heads, dk=dk, dv=dv,
                          has_s0=has_s0, seqs=seqs),
        grid=(bsz // seqs, nchunk),
        in_specs=in_specs,
        out_specs=[pl.BlockSpec((seqs, rows, wv), lambda b, c: (b, c, 0)),
                   pl.BlockSpec((seqs, heads, dk, dv), lambda b, c: (b, 0, 0, 0))],
        out_shape=[jax.ShapeDtypeStruct((bsz, t, wv), F32), jax.ShapeDtypeStruct((bsz, heads, dk, dv), F32)],
        scratch_shapes=scratch,
        compiler_params=_cparams("parallel", "arbitrary"),
        name=f"{mode}_rec",
    )(*args)


def _kmean_kernel(k_ref, o_ref):
    o_ref[...] = jnp.mean(k_ref[...], axis=0, keepdims=True)


def _block_key_means(proj3):
    bsz, t, _ = proj3.shape
    nb = t // MOBA_BLOCK
    out = pl.pallas_call(
        _kmean_kernel,
        grid=(bsz, nb),
        in_specs=[pl.BlockSpec((None, MOBA_BLOCK, MOBA_WIDTH), lambda b, n: (b, n, COL_KC // MOBA_WIDTH))],
        out_specs=pl.BlockSpec((None, None, 1, MOBA_WIDTH), lambda b, n: (b, n, 0, 0)),
        out_shape=jax.ShapeDtypeStruct((bsz, nb, 1, MOBA_WIDTH), F32),
        compiler_params=_cparams("parallel", "parallel"),
    )(proj3)
    return out.reshape(bsz, nb, MOBA_WIDTH)


def _select_top_blocks(gscore, n_valid, idx, axis):
    g = jnp.where(idx < n_valid, gscore, -jnp.inf)
    sel = jnp.zeros(gscore.shape, F32)
    width = gscore.shape[axis]
    for _ in range(MOBA_TOPK):
        m = jnp.max(g, axis=axis, keepdims=True)
        first = jnp.min(jnp.where(g == m, idx, width), axis=axis, keepdims=True)
        pick = idx == jnp.where(m > -jnp.inf, first, -1)
        sel = jnp.where(pick, 1.0, sel)
        g = jnp.where(pick, -jnp.inf, g)
    return sel


def _moba_prompt_kernel(q_ref, k_ref, v_ref, km_ref, o_ref, kb_ref, vt_ref, sel_ref):
    i = pl.program_id(2)
    blk = MOBA_BLOCK
    dh = MOBA_DH
    heads = q_ref.shape[-1] // dh
    hs = [slice(j * dh, (j + 1) * dh) for j in range(heads)]

    @pl.when(i == 0)
    def _():
        kb_ref[...] = k_ref[...].astype(BF16)

        def transpose_block(n, carry):
            r0 = pl.multiple_of(n * blk, blk)
            vt_ref[:, pl.ds(r0, blk)] = v_ref[pl.ds(r0, blk), :].T.astype(BF16)
            return carry

        lax.fori_loop(0, k_ref.shape[0] // blk, transpose_block, 0)

    qbs = []
    for j in range(heads):
        q = q_ref[:, hs[j]]
        gscore = _dot_nt(km_ref[:, hs[j]], q, precision=HIGHEST)
        blk_idx = lax.broadcasted_iota(jnp.int32, gscore.shape, 0)
        sel_ref[j] = _select_top_blocks(gscore, i, blk_idx, 0)
        qbs.append((q * (dh ** -0.5)).astype(BF16))

    def scores(j, n):
        return _dot_nt(kb_ref[pl.ds(pl.multiple_of(n * blk, blk), blk), hs[j]], qbs[j])

    def values(j, n, p):
        return _dot(vt_ref[hs[j], pl.ds(pl.multiple_of(n * blk, blk), blk)], p)

    def softmax_step(s, m, l, acc, pv):
        m_new = jnp.maximum(m, jnp.max(s, axis=0, keepdims=True))
        alpha = jnp.exp(m - m_new)
        p = jnp.exp(s - m_new)
        l = alpha * l + jnp.sum(p, axis=0, keepdims=True)
        return m_new, l, alpha * (acc + pv), p.astype(BF16)

    def past_block(n, carry):
        out = []
        for j in range(heads):
            m, l, acc, s, p_prev = carry[j]
            pv = values(j, jnp.maximum(n - 1, 0), p_prev)
            s_next = scores(j, n + 1)
            bias = jnp.where(sel_ref[j, pl.ds(n, 1), :] > 0.5, 0.0, NEG)
            m, l, acc, p = softmax_step(s + bias, m, l, acc, pv)
            out.append((m, l, acc, s_next, p))
        return tuple(out)

    init = tuple((jnp.full((1, blk), NEG, F32), jnp.zeros((1, blk), F32), jnp.zeros((dh, blk), F32),
                  scores(j, 0), jnp.zeros((blk, blk), BF16)) for j in range(heads))
    carry = lax.fori_loop(0, i, past_block, init)
    kpos = lax.broadcasted_iota(jnp.int32, (blk, blk), 0)
    qpos = lax.broadcasted_iota(jnp.int32, (blk, blk), 1)
    for j in range(heads):
        m, l, acc, s, p_prev = carry[j]
        pv = values(j, jnp.maximum(i - 1, 0), p_prev)
        m, l, acc, p = softmax_step(jnp.where(kpos <= qpos, s, NEG), m, l, acc, pv)
        acc = acc + values(j, i, p)
        o_ref[:, hs[j]] = (acc / l).T


def _moba_prompt(proj3, heads_per_step=2):
    bsz, t, _ = proj3.shape
    nb = t // MOBA_BLOCK
    km = _block_key_means(proj3)
    w = heads_per_step * MOBA_DH
    hq, hk, hv = COL_QC // w, COL_KC // w, COL_VC // w
    once = pl.Buffered(1)
    return pl.pallas_call(
        _moba_prompt_kernel,
        grid=(bsz, MOBA_HEADS // heads_per_step, nb),
        in_specs=[pl.BlockSpec((None, MOBA_BLOCK, w), lambda b, h, i: (b, i, hq + h)),
                  pl.BlockSpec((None, t, w), lambda b, h, i: (b, 0, hk + h), pipeline_mode=once),
                  pl.BlockSpec((None, t, w), lambda b, h, i: (b, 0, hv + h), pipeline_mode=once),
                  pl.BlockSpec((None, nb, w), lambda b, h, i: (b, 0, h))],
        out_specs=pl.BlockSpec((None, MOBA_BLOCK, w), lambda b, h, i: (b, i, h)),
        out_shape=jax.ShapeDtypeStruct((bsz, t, MOBA_WIDTH), F32),
        scratch_shapes=[pltpu.VMEM((t, w), BF16), pltpu.VMEM((w, t), BF16),
                        pltpu.VMEM((heads_per_step, nb, MOBA_BLOCK), F32)],
        compiler_params=_cparams("parallel", "parallel", "arbitrary"),
        name="moba_prompt",
    )(proj3, proj3, proj3, km)


def _moba_sample_kernel(pt_ref, q_ref, kn_ref, vn_ref, seg_ref, *refs, n_pages, seqs):
    npg = seqs * n_pages
    o_ref = refs[2 * npg]
    s_ref, km_ref = refs[2 * npg + 1], refs[2 * npg + 2]
    nh = MOBA_HEADS
    prow = PAGE_SIZE * nh
    pages_per_block = MOBA_BLOCK // PAGE_SIZE
    n_blocks = n_pages // pages_per_block
    sub = 8
    nq = q_ref.shape[1]
    q_head = lax.broadcasted_iota(jnp.int32, (nq, 1), 0) % nh
    q_tok = lax.broadcasted_iota(jnp.int32, (nq, 1), 0) // nh
    own_head = lax.broadcasted_iota(jnp.int32, (nq, prow), 1) % nh == q_head
    for sq in range(seqs):
        k_pages = refs[sq * n_pages:(sq + 1) * n_pages]
        v_pages = refs[npg + sq * n_pages:npg + (sq + 1) * n_pages]
        q = q_ref[sq]
        for n in range(n_blocks):
            tot = jnp.sum(k_pages[n * pages_per_block][...].reshape(prow // sub, sub, MOBA_DH), axis=0)
            for j in range(1, pages_per_block):
                tot = tot + jnp.sum(k_pages[n * pages_per_block + j][...].reshape(prow // sub, sub, MOBA_DH), axis=0)
            km_ref[sq, n * sub:(n + 1) * sub, :] = tot
        g_all = _dot_nt(q, km_ref[sq], precision=HIGHEST)
        g_lane = lax.broadcasted_iota(jnp.int32, g_all.shape, 1)
        g_own = jnp.where(g_lane % nh == q_head, g_all, 0.0)
        gscore = _dot(g_own, seg_ref[...], precision=HIGHEST) * (1.0 / MOBA_BLOCK)
        blk_idx = lax.broadcasted_iota(jnp.int32, gscore.shape, 1)
        sel = _select_top_blocks(gscore, n_blocks, blk_idx, 1)
        qb = (q * (MOBA_DH ** -0.5)).astype(BF16)
        k_all = jnp.concatenate([k_pages[j][...].astype(BF16) for j in range(n_pages)], axis=0)
        s_all = _dot_nt(qb, k_all)
        for j in range(n_pages):
            nblk = j // pages_per_block
            keep = jnp.where(own_head, sel[:, nblk:nblk + 1], 0.0) > 0.5
            s_ref[sq, :, j * prow:(j + 1) * prow] = jnp.where(keep, s_all[:, j * prow:(j + 1) * prow], NEG)
        s_new = _dot_nt(qb, kn_ref[sq].astype(BF16))
        coln = lax.broadcasted_iota(jnp.int32, s_new.shape, 1)
        ok_new = jnp.where(coln % nh == q_head, coln // nh, nq) <= q_tok
        s_new = jnp.where(ok_new, s_new, NEG)
        s_past = s_ref[sq]
        m = jnp.maximum(jnp.max(s_past, axis=-1, keepdims=True), jnp.max(s_new, axis=-1, keepdims=True))
        p_past = jnp.exp(s_past - m)
        p_new = jnp.exp(s_new - m)
        l = jnp.sum(p_past, axis=-1, keepdims=True) + jnp.sum(p_new, axis=-1, keepdims=True)
        v_all = jnp.concatenate([v_pages[j][...].astype(BF16) for j in range(n_pages)], axis=0)
        o = _dot(p_new.astype(BF16), vn_ref[sq].astype(BF16)) + _dot(p_past.astype(BF16), v_all)
        o_ref[sq] = o / l


def _moba_sample(q, k_new, v_new, cache_k, cache_v, page_table, layer, seqs):
    bsz, nq, dh = q.shape
    depth, n_pool = cache_k.shape[:2]
    n_pages = page_table.shape[1]
    prow = PAGE_SIZE * MOBA_HEADS
    ck = cache_k.reshape(depth, n_pool, prow, dh)
    cv = cache_v.reshape(depth, n_pool, prow, dh)
    n_blocks = n_pages * PAGE_SIZE // MOBA_BLOCK
    seg = (jnp.arange(n_blocks * 8)[:, None] // 8 == jnp.arange(n_blocks)[None, :]).astype(F32)
    assert bsz % seqs == 0

    def page_spec(sq, j):
        return pl.BlockSpec((None, None, prow, dh), lambda b, pt: (layer, pt[(b * seqs + sq) * n_pages + j], 0, 0))

    pages = [page_spec(sq, j) for sq in range(seqs) for j in range(n_pages)]
    grid_spec = pltpu.PrefetchScalarGridSpec(
        num_scalar_prefetch=1,
        grid=(bsz // seqs,),
        in_specs=([pl.BlockSpec((seqs, nq, dh), lambda b, pt: (b, 0, 0))] * 3
                  + [pl.BlockSpec(seg.shape, lambda b, pt: (0, 0))] + pages * 2),
        out_specs=pl.BlockSpec((seqs, nq, dh), lambda b, pt: (b, 0, 0)),
        scratch_shapes=[pltpu.VMEM((seqs, nq, n_pages * prow), F32), pltpu.VMEM((seqs, n_blocks * 8, dh), F32)],
    )
    return pl.pallas_call(
        functools.partial(_moba_sample_kernel, n_pages=n_pages, seqs=seqs),
        grid_spec=grid_spec,
        out_shape=jax.ShapeDtypeStruct((bsz, nq, dh), F32),
        compiler_params=_cparams("arbitrary"),
        name="moba_sample",
    )(page_table.reshape(-1), q, k_new, v_new, seg, *([ck] * (seqs * n_pages)), *([cv] * (seqs * n_pages)))


def _mix_kernel(ya_ref, ob_ref, oc_ref, od_ref, gl_ref, x_ref, g1_ref, wglu_ref, wb_ref, wc_ref, wd_ref, wo_ref,
                lng_ref, lnb_ref, o_ref):
    d = D_MODEL
    z = _dot(ya_ref[...].astype(BF16), wglu_ref[...])
    merged = jax.nn.sigmoid(gl_ref[:, 0:d]) * (z[:, 0:d] * jax.nn.sigmoid(z[:, d:2 * d]))
    merged += jax.nn.sigmoid(gl_ref[:, d:2 * d]) * _dot(ob_ref[...].astype(BF16), wb_ref[...])
    merged += jax.nn.sigmoid(gl_ref[:, 2 * d:3 * d]) * _dot(oc_ref[...].astype(BF16), wc_ref[...])
    merged += jax.nn.sigmoid(gl_ref[:, 3 * d:4 * d]) * _dot(od_ref[...].astype(BF16), wd_ref[...])
    mix = _dot(merged.astype(BF16), wo_ref[...])
    o_ref[...] = _layer_norm(DEEPNORM_ALPHA * x_ref[...] + g1_ref[...] * mix, lng_ref[...], lnb_ref[...])


def _mix(ya, ob, oc, od, proj, x, mod, rows_per_seq, wl, layer, tm):
    n, d = x.shape
    w512 = SSM_WIDTH
    tok = lambda width, col=0: pl.BlockSpec((tm, width), lambda i: (i, col))
    const = lambda r, c: pl.BlockSpec((None, r, c), lambda i: (layer, 0, 0))
    return pl.pallas_call(
        _mix_kernel,
        grid=(n // tm,),
        in_specs=[tok(w512), tok(w512), tok(w512), tok(w512), tok(N_BRANCH * d, COL_GATES), tok(d),
                  _mod_spec(mod, tm, rows_per_seq, 2),
                  const(w512, 2 * d), const(w512, d), const(w512, d), const(w512, d), const(d, d),
                  const(1, d), const(1, d)],
        out_specs=tok(d),
        out_shape=jax.ShapeDtypeStruct((n, d), F32),
        compiler_params=_cparams("parallel"),
        name="mix",
    )(ya, ob, oc, od, proj, x, mod, wl["w_glu"], wl["w_hgrn"], wl["w_moba"], wl["w_gla"], wl["w_out"],
      wl["ln1_g"], wl["ln1_b"])


def _first_max(vals, lane, width):
    m = jnp.max(vals, axis=-1, keepdims=True)
    first = jnp.min(jnp.where(vals == m, lane, width), axis=-1, keepdims=True)
    return m, first


def _route(h2, wr_ref, br_ref):
    logits = _dot(h2, wr_ref[...], precision=HIGHEST)
    lane = lax.broadcasted_iota(jnp.int32, logits.shape, 1)
    width = logits.shape[-1]
    scores = jax.nn.sigmoid(logits)
    biased = jnp.where(lane < N_EXPERTS, scores + br_ref[...], -jnp.inf)
    group = lane // EXPERTS_PER_GROUP
    best = jnp.zeros((logits.shape[0], 1), jnp.int32)
    best_score = None
    for gidx in range(N_EXPERT_GROUPS):
        vals = jnp.where(group == gidx, biased, -jnp.inf)
        m1, i1 = _first_max(vals, lane, width)
        m2 = jnp.max(jnp.where(lane == i1, -jnp.inf, vals), axis=-1, keepdims=True)
        gs = m1 + m2
        if best_score is None:
            best_score = gs
        else:
            better = gs > best_score
            best = jnp.where(better, gidx, best)
            best_score = jnp.where(better, gs, best_score)
    masked = jnp.where(group == best, biased, -jnp.inf)
    _, i1 = _first_max(masked, lane, width)
    _, i2 = _first_max(jnp.where(lane == i1, -jnp.inf, masked), lane, width)
    w1 = jnp.sum(jnp.where(lane == i1, scores, 0.0), axis=-1, keepdims=True)
    w2 = jnp.sum(jnp.where(lane == i2, scores, 0.0), axis=-1, keepdims=True)
    tot = w1 + w2
    return jnp.where(lane == i1, w1 / tot, 0.0) + jnp.where(lane == i2, w2 / tot, 0.0)


def _moe_kernel(x_ref, sh_ref, sc_ref, g2_ref, wr_ref, br_ref, wg_ref, wu_ref, wd_ref, lng_ref, lnb_ref, o_ref,
                h_ref, comb_ref, acc_ref, *, experts_per_step):
    e = pl.program_id(1)

    @pl.when(e == 0)
    def _():
        h2 = x_ref[...] * (1.0 + sc_ref[...]) + sh_ref[...]
        h_ref[...] = h2.astype(BF16)
        comb_ref[...] = _route(h2, wr_ref, br_ref)
        acc_ref[...] = jnp.zeros(acc_ref.shape, F32)

    hb = h_ref[...]
    lane = lax.broadcasted_iota(jnp.int32, comb_ref.shape, 1)
    for j in range(experts_per_step):
        eid = e * experts_per_step + j
        cw = jnp.sum(jnp.where(lane == eid, comb_ref[...], 0.0), axis=-1, keepdims=True)
        hid = _silu(_dot(hb, wg_ref[j])) * _dot(hb, wu_ref[j]) * cw
        acc_ref[...] += _dot(hid.astype(BF16), wd_ref[j])

    @pl.when(e == pl.num_programs(1) - 1)
    def _():
        o_ref[...] = _layer_norm(DEEPNORM_ALPHA * x_ref[...] + g2_ref[...] * acc_ref[...], lng_ref[...], lnb_ref[...])


def _moe(x, mod, rows_per_seq, wl, layer, w_router, b_router, tm, experts_per_step=4):
    n, d = x.shape
    nh = EXPERT_HIDDEN
    tok = pl.BlockSpec((tm, d), lambda i, e: (i, 0))
    const = lambda r, c: pl.BlockSpec((r, c), lambda i, e: (0, 0))
    per_layer = lambda r, c: pl.BlockSpec((None, r, c), lambda i, e: (layer, 0, 0))
    return pl.pallas_call(
        functools.partial(_moe_kernel, experts_per_step=experts_per_step),
        grid=(n // tm, N_EXPERTS // experts_per_step),
        in_specs=[tok, _mod_spec(mod, tm, rows_per_seq, 3), _mod_spec(mod, tm, rows_per_seq, 4),
                  _mod_spec(mod, tm, rows_per_seq, 5), const(d, LANES), const(1, LANES),
                  pl.BlockSpec((None, experts_per_step, d, nh), lambda i, e: (layer, e, 0, 0)),
                  pl.BlockSpec((None, experts_per_step, d, nh), lambda i, e: (layer, e, 0, 0)),
                  pl.BlockSpec((None, experts_per_step, nh, d), lambda i, e: (layer, e, 0, 0)),
                  per_layer(1, d), per_layer(1, d)],
        out_specs=tok,
        out_shape=jax.ShapeDtypeStruct((n, d), F32),
        scratch_shapes=[pltpu.VMEM((tm, d), BF16), pltpu.VMEM((tm, LANES), F32), pltpu.VMEM((tm, d), F32)],
        compiler_params=_cparams("parallel", "arbitrary"),
        name="moe",
    )(x, mod, mod, mod, w_router, b_router, wl["moe_gate"], wl["moe_up"], wl["moe_down"], wl["ln2_g"], wl["ln2_b"])


def _token_mixers_prompt(proj, k_rows, v_rows, bsz, t, wl, ws, layer):
    proj3 = proj.reshape(bsz, t, IN_COLS_PAD)
    ya, ssm_re, ssm_im = _ssm_prompt(proj3, ws["ssm_prompt"], layer)
    ob, hgrn_s = _lin_rec("hgrn", proj3, (COL_QB, COL_FB, COL_IB, COL_GB), wl["hgrn_extras"], wl["hgrn_norm"],
                          None, 0, REC_CHUNK_PROMPT)
    od, gla_s = _lin_rec("gla", proj3, (COL_QD, COL_KD, COL_VD, COL_GD), wl["gla_extras"], wl["gla_norm"],
                         None, 0, REC_CHUNK_PROMPT)
    oc = _moba_prompt(proj3)
    n = bsz * t
    k_new = k_rows.reshape(bsz, t, MOBA_HEADS, MOBA_DH)
    v_new = v_rows.reshape(bsz, t, MOBA_HEADS, MOBA_DH)
    return (ya.reshape(n, -1), ob.reshape(n, -1), oc.reshape(n, -1), od.reshape(n, -1),
            (k_new, v_new, ssm_re, ssm_im, hgrn_s, gla_s))


def _token_mixers_sample(proj, k_rows, v_rows, bsz, t, wl, layer, cache_k, cache_v, page_table, st_re, st_im,
                         st_hgrn, st_gla):
    proj3 = proj.reshape(bsz, t, IN_COLS_PAD)
    u_tm = proj3[:, :, COL_UA:COL_UA + SSM_WIDTH].transpose(1, 0, 2)
    ns = SSM_GROUPS * SSM_STATE
    ya_tm, xr, xi = _ssm_sample(u_tm, st_re[layer].reshape(bsz, ns), st_im[layer].reshape(bsz, ns), wl["ssm"])
    ya = ya_tm.transpose(1, 0, 2)
    seqs = math.gcd(bsz, REC_SAMPLE_SEQS)
    ob, hgrn_s = _lin_rec("hgrn", proj3, (COL_QB, COL_FB, COL_IB, COL_GB), wl["hgrn_extras"], wl["hgrn_norm"],
                          st_hgrn, layer, SUB_CHUNK, seqs)
    od, gla_s = _lin_rec("gla", proj3, (COL_QD, COL_KD, COL_VD, COL_GD), wl["gla_extras"], wl["gla_norm"],
                         st_gla, layer, SUB_CHUNK, seqs)
    nq = t * MOBA_HEADS
    q = proj3[:, :, COL_QC:COL_QC + MOBA_WIDTH].reshape(bsz, nq, MOBA_DH)
    oc = _moba_sample(q, k_rows.reshape(bsz, nq, MOBA_DH), v_rows.reshape(bsz, nq, MOBA_DH), cache_k, cache_v,
                      page_table, layer, math.gcd(bsz, MOBA_SAMPLE_SEQS))
    n = bsz * t
    return (ya.reshape(n, -1), ob.reshape(n, -1), oc.reshape(n, -1), od.reshape(n, -1),
            (k_rows.reshape(bsz, t, MOBA_HEADS, MOBA_DH), v_rows.reshape(bsz, t, MOBA_HEADS, MOBA_DH),
             xr.reshape(bsz, SSM_GROUPS, SSM_STATE), xi.reshape(bsz, SSM_GROUPS, SSM_STATE), hgrn_s, gla_s))


SSM_PROMPT_TABLES = ("toep", "in_re", "in_im", "out_re", "out_im", "scan_re", "scan_im")


def _stacked_weights(w_in_p, ssm_tabs, ssm_w_glu, hgrn_w_proj, moba_w_proj, gla_w_proj, w_out, ln1_g, ln1_b, ln2_g,
                     ln2_b, moe_w_gate, moe_w_up, moe_w_down):
    rows = lambda a: a.reshape(a.shape[0], 1, a.shape[1])
    ssm = {k: ssm_tabs[k] for k in SSM_PROMPT_TABLES}
    ssm["d_row"] = ssm_tabs["d_row"]
    return dict(
        w_in=w_in_p, ssm_prompt=ssm,
        w_glu=ssm_w_glu.astype(BF16), w_hgrn=hgrn_w_proj.astype(BF16), w_moba=moba_w_proj.astype(BF16),
        w_gla=gla_w_proj.astype(BF16), w_out=w_out.astype(BF16),
        ln1_g=rows(ln1_g), ln1_b=rows(ln1_b), ln2_g=rows(ln2_g), ln2_b=rows(ln2_b),
        moe_gate=moe_w_gate.astype(BF16), moe_up=moe_w_up.astype(BF16), moe_down=moe_w_down.astype(BF16))


def _layer_weights(l, ssm_tabs, lower_bounds, gla_w_gk2, gla_b_gk, hgrn_norm, gla_norm):
    lb = lower_bounds[l].reshape(1, -1)
    wgk = jnp.pad(gla_w_gk2[l], ((0, LANES - GLA_RANK), (0, 0))).astype(BF16)
    row = lambda a: a.reshape(1, -1)
    return dict(
        ssm={k: v[l] for k, v in ssm_tabs.items() if k not in SSM_PROMPT_TABLES},
        hgrn_extras=(jnp.log(lb), jnp.log1p(-lb), 1.0 - lb), hgrn_norm=hgrn_norm[l],
        gla_extras=(wgk, row(gla_b_gk[l])), gla_norm=gla_norm[l])


def kernel(x_prompt, x_sample, cache_k, cache_v, state_ssm_re, state_ssm_im, state_hgrn, state_gla, page_table,
           c_prompt, c_sample, ln_in_g, ln_in_b, w_ada, b_ada, w_in, ssm_lam_re, ssm_lam_im, ssm_log_step,
           ssm_b_re, ssm_b_im, ssm_c_re, ssm_c_im, ssm_d, ssm_w_glu, hgrn_lb, hgrn_norm, hgrn_w_proj, moba_w_proj,
           gla_w_gk2, gla_b_gk, gla_norm, gla_w_proj, w_out, ln1_g, ln1_b, ln2_g, ln2_b, w_router, b_router,
           moe_w_gate, moe_w_up, moe_w_down):
    bp, tp, d = x_prompt.shape
    bs, ts, _ = x_sample.shape
    depth = w_in.shape[0]
    n_pool = cache_k.shape[1]
    np_tok, ns_tok = bp * tp, bs * ts
    tm_p = min(1024, tp)
    tm_s = min(512, ns_tok)

    lb_cum = jnp.cumsum(jax.nn.softmax(hgrn_lb.astype(F32), axis=0), axis=0)
    lower_bounds = lb_cum - lb_cum[0:1]
    w_in_p = jnp.concatenate(
        [w_in[:, :, REF_GATE_START:IN_COLS_REF], w_in[:, :, :REF_RD_START], w_in[:, :, REF_RD_START:REF_GATE_START],
         jnp.zeros((depth, d, IN_COLS_PAD - IN_COLS_REF), w_in.dtype)], axis=-1).astype(BF16)
    n_scan = max(tp // SSM_CHUNK - 1, 0).bit_length()
    ssm_tabs = jax.vmap(functools.partial(_ssm_tables, n_scan=n_scan))(
        ssm_lam_re, ssm_lam_im, ssm_log_step, ssm_b_re, ssm_b_im, ssm_c_re, ssm_c_im, ssm_d)
    w_router_p = jnp.pad(w_router, ((0, 0), (0, LANES - N_EXPERTS)))
    b_router_p = jnp.pad(b_router, (0, LANES - N_EXPERTS)).reshape(1, LANES)

    nc = bs + bp
    nc_pad = -(-nc // 8) * 8
    c_all = jnp.pad(jnp.concatenate([c_sample, c_prompt], axis=0), ((0, nc_pad - nc), (0, 0)))
    mod_all = _ada_all(c_all, w_ada, b_ada)

    xp = _ln_rows(x_prompt.reshape(np_tok, d), ln_in_g, ln_in_b, tm_p)
    xs = _ln_rows(x_sample.reshape(ns_tok, d), ln_in_g, ln_in_b, tm_s)

    ws = _stacked_weights(w_in_p, ssm_tabs, ssm_w_glu, hgrn_w_proj, moba_w_proj, gla_w_proj, w_out, ln1_g, ln1_b,
                          ln2_g, ln2_b, moe_w_gate, moe_w_up, moe_w_down)
    outs_p, outs_s = [], []
    for l in range(depth):
        wl = _layer_weights(l, ssm_tabs, lower_bounds, gla_w_gk2, gla_b_gk, hgrn_norm, gla_norm)
        mod_p = mod_all[l, bs:bs + bp].reshape(bp, 1, 6 * d)
        mod_s = jnp.repeat(mod_all[l, :bs], ts, axis=0)

        proj_p, k_rows, v_rows = _in_proj(xp, mod_p, tp, ws["w_in"], l, tm_p)
        ya, ob, oc, od, st_p = _token_mixers_prompt(proj_p, k_rows, v_rows, bp, tp, wl, ws, l)
        xp = _mix(ya, ob, oc, od, proj_p, xp, mod_p, tp, ws, l, min(256, tp))
        xp = _moe(xp, mod_p, tp, ws, l, w_router_p, b_router_p, tm_p)
        outs_p.append(st_p)

        proj_s, k_rows, v_rows = _in_proj(xs, mod_s, ts, ws["w_in"], l, tm_s)
        ya, ob, oc, od, st_s = _token_mixers_sample(proj_s, k_rows, v_rows, bs, ts, wl, l, cache_k, cache_v,
                                                    page_table, state_ssm_re, state_ssm_im, state_hgrn, state_gla)
        xs = _mix(ya, ob, oc, od, proj_s, xs, mod_s, ts, ws, l, min(256, ns_tok))
        xs = _moe(xs, mod_s, ts, ws, l, w_router_p, b_router_p, tm_s)
        outs_s.append(st_s)

    stack = lambda outs, idx: jnp.stack([o[idx] for o in outs])
    return (xp.reshape(bp, tp, d), xs.reshape(bs, ts, d),
            stack(outs_p, 0), stack(outs_p, 1), stack(outs_s, 0), stack(outs_s, 1),
            stack(outs_p, 2), stack(outs_p, 3), stack(outs_s, 2), stack(outs_s, 3),
            stack(outs_p, 4), stack(outs_s, 4), stack(outs_p, 5), stack(outs_s, 5))
```

```python
import functools
import math

import jax
import jax.numpy as jnp
from jax import lax
from jax.experimental import pallas as pl
from jax.experimental.pallas import tpu as pltpu

F32 = jnp.float32
BF16 = jnp.bfloat16
HIGHEST = lax.Precision.HIGHEST

D_MODEL = 1024
DEPTH = 4
PAGE_SIZE = 128
SSM_WIDTH = 512
SSM_GROUP = 16
SSM_GROUPS = 32
SSM_STATE = 64
SSM_MAX_RE = -1e-4
SSM_CHUNK = 16
SSM_QUAD_GROUPS = 8
HGRN_HEADS = 4
HGRN_DK = 128
HGRN_DV = 128
MOBA_HEADS = 4
MOBA_DH = 128
MOBA_WIDTH = 512
MOBA_BLOCK = 256
MOBA_TOPK = 3
GLA_HEADS = 4
GLA_DK = 64
GLA_DV = 128
GLA_RANK = 16
GLA_TAU = 16.0
N_BRANCH = 4
N_EXPERTS = 16
N_EXPERT_GROUPS = 4
EXPERTS_PER_GROUP = 4
EXPERT_HIDDEN = 256
DEEPNORM_ALPHA = (2 * DEPTH) ** 0.25
NORM_EPS = 1e-5

LANES = 128
SUB_CHUNK = 16
REC_CHUNK_PROMPT = 128
REC_SAMPLE_SEQS = 4
MOBA_SAMPLE_SEQS = 1
VMEM_LIMIT = 56 * 1024 * 1024
NEG = -1e30

COL_GATES = 0
COL_UA = 4096
COL_QB = 4608
COL_FB = 5120
COL_IB = 5632
COL_GB = 6144
COL_QC = 6656
COL_KC = 7168
COL_VC = 7680
COL_QD = 8192
COL_KD = 8448
COL_VD = 8704
COL_GD = 9216
COL_RD = 9728
IN_COLS_PAD = 10240
IN_COLS_REF = 9744
REF_GATE_START = 5648
REF_RD_START = 5632


def _cparams(*sem):
    return pltpu.CompilerParams(dimension_semantics=sem, vmem_limit_bytes=VMEM_LIMIT)


def _silu(x):
    return x * jax.nn.sigmoid(x)


def _log_sigmoid(x):
    return jnp.minimum(x, 0.0) - jnp.log1p(jnp.exp(-jnp.abs(x)))


def _dot(a, b, **kw):
    return jnp.dot(a, b, preferred_element_type=F32, **kw)


def _dot_nt(a, b, **kw):
    return lax.dot_general(a, b, (((1,), (1,)), ((), ())), preferred_element_type=F32, **kw)


def _dot_tn(a, b, **kw):
    return lax.dot_general(a, b, (((0,), (0,)), ((), ())), preferred_element_type=F32, **kw)


def _layer_norm(x, g, b):
    mu = jnp.mean(x, axis=-1, keepdims=True)
    xc = x - mu
    var = jnp.mean(xc * xc, axis=-1, keepdims=True)
    return xc * lax.rsqrt(var + NORM_EPS) * g + b


def _ln_kernel(x_ref, g_ref, b_ref, o_ref):
    o_ref[...] = _layer_norm(x_ref[...], g_ref[...], b_ref[...])


def _ln_rows(x, g, b, tm):
    n, d = x.shape
    return pl.pallas_call(
        _ln_kernel,
        grid=(n // tm,),
        in_specs=[pl.BlockSpec((tm, d), lambda i: (i, 0)),
                  pl.BlockSpec((1, d), lambda i: (0, 0)),
                  pl.BlockSpec((1, d), lambda i: (0, 0))],
        out_specs=pl.BlockSpec((tm, d), lambda i: (i, 0)),
        out_shape=jax.ShapeDtypeStruct((n, d), F32),
        compiler_params=_cparams("parallel"),
    )(x, g.reshape(1, d), b.reshape(1, d))


def _ada_kernel(c_ref, w_ref, b_ref, o_ref):
    h = _silu(c_ref[...]).astype(BF16)
    o_ref[...] = _dot(h, w_ref[...].astype(BF16)) + b_ref[...]


def _ada_all(c, w_ada, b_ada, tn=1536):
    nb, d = c.shape
    depth, _, n6 = w_ada.shape
    return pl.pallas_call(
        _ada_kernel,
        grid=(depth, n6 // tn),
        in_specs=[pl.BlockSpec((nb, d), lambda l, j: (0, 0)),
                  pl.BlockSpec((None, d, tn), lambda l, j: (l, 0, j)),
                  pl.BlockSpec((None, 1, tn), lambda l, j: (l, 0, j))],
        out_specs=pl.BlockSpec((None, nb, tn), lambda l, j: (l, 0, j)),
        out_shape=jax.ShapeDtypeStruct((depth, nb, n6), F32),
        compiler_params=_cparams("parallel", "parallel"),
    )(c, w_ada, b_ada.reshape(depth, 1, n6))


def _mod_spec(mod, tm, rows_per_seq, chunk):
    if mod.ndim == 3:
        tiles_per_seq = rows_per_seq // tm
        return pl.BlockSpec((None, 1, D_MODEL), lambda i, *_: (i // tiles_per_seq, 0, chunk))
    return pl.BlockSpec((tm, D_MODEL), lambda i, *_: (i, chunk))


def _inproj_kernel(x_ref, sh_ref, sc_ref, w_ref, *rest, kv_tile):
    o_ref, k_ref, v_ref, h_ref = rest[-4:]
    j = pl.program_id(1)

    @pl.when(j == 0)
    def _():
        h_ref[...] = (x_ref[...] * (1.0 + sc_ref[...]) + sh_ref[...]).astype(BF16)

    acc = _dot(h_ref[...], w_ref[...])
    o_ref[...] = acc

    @pl.when(j == kv_tile)
    def _():
        tm = acc.shape[0]
        for h in range(MOBA_HEADS):
            k_ref[pl.ds(h, tm, stride=MOBA_HEADS), :] = acc[:, h * MOBA_DH:(h + 1) * MOBA_DH]
            v_ref[pl.ds(h, tm, stride=MOBA_HEADS), :] = acc[:, MOBA_WIDTH + h * MOBA_DH:MOBA_WIDTH + (h + 1) * MOBA_DH]


def _in_proj(x, mod, rows_per_seq, w_all, layer, tm, kv_acc=None, tn=1024):
    n, d = x.shape
    depth, _, ncol = w_all.shape
    assert COL_KC % tn == 0 and COL_VC == COL_KC + MOBA_WIDTH and tn == 2 * MOBA_WIDTH
    kv_spec = pl.BlockSpec((None, tm * MOBA_HEADS, MOBA_DH), lambda i, j: (layer, i, 0))
    kv_shape = jax.ShapeDtypeStruct((depth, n * MOBA_HEADS, MOBA_DH), F32)
    in_specs = [pl.BlockSpec((tm, d), lambda i, j: (i, 0)),
                _mod_spec(mod, tm, rows_per_seq, 0),
                _mod_spec(mod, tm, rows_per_seq, 1),
                pl.BlockSpec((None, d, tn), lambda i, j: (layer, 0, j))]
    args = [x, mod, mod, w_all]
    aliases = {}
    if kv_acc is not None:
        in_specs += [pl.BlockSpec(memory_space=pl.ANY)] * 2
        aliases = {len(args): 1, len(args) + 1: 2}
        args += list(kv_acc)
    return pl.pallas_call(
        functools.partial(_inproj_kernel, kv_tile=COL_KC // tn),
        grid=(n // tm, ncol // tn),
        in_specs=in_specs,
        out_specs=[pl.BlockSpec((tm, tn), lambda i, j: (i, j)), kv_spec, kv_spec],
        out_shape=[jax.ShapeDtypeStruct((n, ncol), F32), kv_shape, kv_shape],
        scratch_shapes=[pltpu.VMEM((tm, d), BF16)],
        input_output_aliases=aliases,
        compiler_params=_cparams("parallel", "arbitrary"),
        name="in_proj",
    )(*args)


def _ssm_prompt_kernel(u_ref, toep_ref, inre_ref, inim_ref, outre_ref, outim_ref, d_ref, are_ref, aim_ref,
                       y_ref, sre_ref, sim_ref, ucat_ref, *, n_steps):
    nrow = ucat_ref.shape[0]
    for tau in range(SSM_CHUNK):
        ucat_ref[:, tau * LANES:(tau + 1) * LANES] = u_ref[pl.ds(tau, nrow, stride=SSM_CHUNK), :].astype(BF16)
    ucat = ucat_ref[...]
    xr = _dot(ucat, inre_ref[...])
    xi = _dot(ucat, inim_ref[...])
    row = lax.broadcasted_iota(jnp.int32, xr.shape, 0)
    for k in range(n_steps):
        dist = 1 << k
        ar = are_ref[k:k + 1, :]
        ai = aim_ref[k:k + 1, :]
        keep = row >= dist
        sr = jnp.where(keep, pltpu.roll(xr, dist, 0), 0.0)
        si = jnp.where(keep, pltpu.roll(xi, dist, 0), 0.0)
        xr, xi = xr + (ar * sr - ai * si), xi + (ar * si + ai * sr)
    keep = row >= 1
    pr = jnp.where(keep, pltpu.roll(xr, 1, 0), 0.0)
    pi = jnp.where(keep, pltpu.roll(xi, 1, 0), 0.0)
    out = (_dot(ucat, toep_ref[...]) + _dot(pr.astype(BF16), outre_ref[...])
           + _dot(pi.astype(BF16), outim_ref[...]))
    for tau in range(SSM_CHUNK):
        u_tau = u_ref[pl.ds(tau, nrow, stride=SSM_CHUNK), :]
        y_ref[pl.ds(tau, nrow, stride=SSM_CHUNK), :] = jax.nn.gelu(out[:, tau * LANES:(tau + 1) * LANES]
                                                                   + d_ref[...] * u_tau)
    sre_ref[...] = xr[nrow - 1:nrow, :]
    sim_ref[...] = xi[nrow - 1:nrow, :]


def _ssm_prompt(proj3, sp, layer):
    bsz, t, _ = proj3.shape
    nrow = t // SSM_CHUNK
    n_steps = max(nrow - 1, 0).bit_length()
    nquad = SSM_WIDTH // LANES
    kw = SSM_CHUNK * LANES
    ns = SSM_QUAD_GROUPS * SSM_STATE
    once = pl.Buffered(1)
    table = lambda r, c, **kw_: pl.BlockSpec((None, None, r, c), lambda q, b: (layer, q, 0, 0), **kw_)
    y, sre, sim = pl.pallas_call(
        functools.partial(_ssm_prompt_kernel, n_steps=n_steps),
        grid=(nquad, bsz),
        in_specs=[pl.BlockSpec((None, t, LANES), lambda q, b: (b, 0, COL_UA // LANES + q)),
                  table(kw, kw, pipeline_mode=once),
                  table(kw, ns, pipeline_mode=once), table(kw, ns, pipeline_mode=once),
                  table(ns, kw, pipeline_mode=once), table(ns, kw, pipeline_mode=once),
                  pl.BlockSpec((None, 1, LANES), lambda q, b: (layer, 0, q)),
                  table(sp["scan_re"].shape[2], ns), table(sp["scan_im"].shape[2], ns)],
        out_specs=[pl.BlockSpec((None, t, LANES), lambda q, b: (b, 0, q)),
                   pl.BlockSpec((None, None, 1, ns), lambda q, b: (b, q, 0, 0)),
                   pl.BlockSpec((None, None, 1, ns), lambda q, b: (b, q, 0, 0))],
        out_shape=[jax.ShapeDtypeStruct((bsz, t, SSM_WIDTH), F32),
                   jax.ShapeDtypeStruct((bsz, nquad, 1, ns), F32),
                   jax.ShapeDtypeStruct((bsz, nquad, 1, ns), F32)],
        scratch_shapes=[pltpu.VMEM((nrow, kw), BF16)],
        compiler_params=_cparams("arbitrary", "arbitrary"),
        name="ssm_prompt",
    )(proj3, sp["toep"], sp["in_re"], sp["in_im"], sp["out_re"], sp["out_im"], sp["d_row"], sp["scan_re"],
      sp["scan_im"])
    return y, sre.reshape(bsz, SSM_GROUPS, SSM_STATE), sim.reshape(bsz, SSM_GROUPS, SSM_STATE)


def _ssm_sample_kernel(u_ref, x0r_ref, x0i_ref, bbr_ref, bbi_ref, cr_ref, ci_ref, ar_ref, ai_ref, d_ref,
                       y_ref, xr_ref, xi_ref):
    xr = x0r_ref[...]
    xi = x0i_ref[...]
    ar = ar_ref[...]
    ai = ai_ref[...]
    for t in range(u_ref.shape[0]):
        u = u_ref[t]
        ub = u.astype(BF16)
        xr, xi = (ar * xr - ai * xi + _dot(ub, bbr_ref[...]),
                  ar * xi + ai * xr + _dot(ub, bbi_ref[...]))
        y = _dot(xr.astype(BF16), cr_ref[...]) - _dot(xi.astype(BF16), ci_ref[...]) + d_ref[...] * u
        y_ref[t] = jax.nn.gelu(y)
    xr_ref[...] = xr
    xi_ref[...] = xi


def _ssm_sample(u_tm, x0r, x0i, sp):
    t, bsz, w = u_tm.shape
    ns = SSM_GROUPS * SSM_STATE
    full = lambda *shape: pl.BlockSpec(shape, lambda i: (0,) * len(shape))
    return pl.pallas_call(
        _ssm_sample_kernel,
        grid=(1,),
        in_specs=[full(t, bsz, w), full(bsz, ns), full(bsz, ns), full(w, ns), full(w, ns), full(ns, w), full(ns, w),
                  full(1, ns), full(1, ns), full(1, w)],
        out_specs=[full(t, bsz, w), full(bsz, ns), full(bsz, ns)],
        out_shape=[jax.ShapeDtypeStruct((t, bsz, w), F32), jax.ShapeDtypeStruct((bsz, ns), F32),
                   jax.ShapeDtypeStruct((bsz, ns), F32)],
        compiler_params=_cparams("arbitrary"),
    )(u_tm, x0r, x0i, sp["bb_re"], sp["bb_im"], sp["c_re"], sp["c_im"], sp["a_re"], sp["a_im"], sp["d_row"])


def _ssm_tables(lam_re, lam_im, log_step, b_re, b_im, c_re, c_im, d, n_scan):
    hp = dict(precision=HIGHEST)
    g, p, hch = SSM_GROUPS, SSM_STATE, SSM_GROUP
    step = jnp.exp(log_step)[:, None]
    lr = jnp.minimum(lam_re, SSM_MAX_RE)
    li = lam_im
    mag = jnp.exp(lr * step)
    ab_re = mag * jnp.cos(li * step)
    ab_im = mag * jnp.sin(li * step)
    den = lr * lr + li * li
    coef_re = ((ab_re - 1.0) * lr + ab_im * li) / den
    coef_im = (ab_im * lr - (ab_re - 1.0) * li) / den
    bb_re = coef_re[..., None] * b_re - coef_im[..., None] * b_im
    bb_im = coef_re[..., None] * b_im + coef_im[..., None] * b_re

    def cmul(xr, xi, yr, yi):
        return xr * yr - xi * yi, xr * yi + xi * yr

    pows_re, pows_im = [jnp.ones_like(ab_re)], [jnp.zeros_like(ab_im)]
    for _ in range(SSM_CHUNK):
        nr, ni = cmul(pows_re[-1], pows_im[-1], ab_re, ab_im)
        pows_re.append(nr)
        pows_im.append(ni)
    pw_re = jnp.stack(pows_re)
    pw_im = jnp.stack(pows_im)
    ajb_re = pw_re[:SSM_CHUNK, :, :, None] * bb_re - pw_im[:SSM_CHUNK, :, :, None] * bb_im
    ajb_im = pw_re[:SSM_CHUNK, :, :, None] * bb_im + pw_im[:SSM_CHUNK, :, :, None] * bb_re
    klag = (jnp.einsum("jgpi,gop->jgio", ajb_re, c_re, **hp) - jnp.einsum("jgpi,gop->jgio", ajb_im, c_im, **hp))
    rev = jnp.arange(SSM_CHUNK - 1, -1, -1)
    rev_re = pw_re[rev]
    rev_im = pw_im[rev]
    p_re = (rev_re[..., None] * bb_re - rev_im[..., None] * bb_im).transpose(1, 0, 3, 2).reshape(g, SSM_CHUNK * hch, p)
    p_im = (rev_re[..., None] * bb_im + rev_im[..., None] * bb_re).transpose(1, 0, 3, 2).reshape(g, SSM_CHUNK * hch, p)
    a1_re = pw_re[1:]
    a1_im = pw_im[1:]
    cr_t = c_re.transpose(0, 2, 1)
    ci_t = c_im.transpose(0, 2, 1)
    q_re = (cr_t[None] * a1_re[..., None] - ci_t[None] * a1_im[..., None]).transpose(1, 2, 0, 3).reshape(g, p, SSM_CHUNK * hch)
    q_im = (-(cr_t[None] * a1_im[..., None] + ci_t[None] * a1_re[..., None])).transpose(1, 2, 0, 3).reshape(g, p, SSM_CHUNK * hch)
    d_chunk = jnp.broadcast_to(d[:, None, :], (g, SSM_CHUNK, hch)).reshape(g, 1, SSM_CHUNK * hch)
    sc_re, sc_im = [pw_re[SSM_CHUNK]], [pw_im[SSM_CHUNK]]
    for _ in range(max(n_scan, 1) - 1):
        nr, ni = cmul(sc_re[-1], sc_im[-1], sc_re[-1], sc_im[-1])
        sc_re.append(nr)
        sc_im.append(ni)
    scan_re = jnp.stack(sc_re, axis=1)
    scan_im = jnp.stack(sc_im, axis=1)

    nq, gq = g // SSM_QUAD_GROUPS, SSM_QUAD_GROUPS
    kw = SSM_CHUNK * gq * hch
    qw = gq * hch

    def group_mask(rows_per_group, cols_per_group):
        r = jnp.arange(gq * rows_per_group)[:, None] // rows_per_group
        c = jnp.arange(gq * cols_per_group)[None, :] // cols_per_group
        return (r == c).astype(F32)

    def replicate(width):
        return (jnp.arange(width)[:, None] == jnp.arange(gq * width)[None, :] % width).astype(F32)

    klag_r = klag.reshape(SSM_CHUNK, nq, gq * hch, hch).transpose(1, 0, 2, 3)
    lag_blk = (jnp.einsum("njro,oc->njrc", klag_r, replicate(hch), **hp) * group_mask(hch, hch)).astype(BF16)
    lag_row = lag_blk.transpose(0, 2, 1, 3).reshape(nq, qw, kw)
    lag_row = jnp.concatenate([jnp.zeros_like(lag_row), lag_row], axis=-1)
    toep_q = jnp.concatenate([lag_row[:, :, (SSM_CHUNK - ts) * qw:(2 * SSM_CHUNK - ts) * qw]
                              for ts in range(SSM_CHUNK)], axis=1)

    def in_state(m):
        m = m.reshape(nq, gq, SSM_CHUNK, hch, p).transpose(0, 2, 1, 3, 4).reshape(nq, SSM_CHUNK, qw, p)
        m = jnp.einsum("ntrp,pc->ntrc", m, replicate(p), **hp) * group_mask(hch, p)
        return m.reshape(nq, kw, gq * p)

    def state_out(m):
        m = m.reshape(nq, gq * p, SSM_CHUNK, hch)
        m = jnp.einsum("nrto,oc->nrtc", m, replicate(hch), **hp) * group_mask(p, hch)[:, None, :]
        return m.reshape(nq, gq * p, kw)

    def scan_rows(m):
        return m.reshape(nq, gq, m.shape[1], p).transpose(0, 2, 1, 3).reshape(nq, m.shape[1], gq * p)

    def dense_diag(blocks):
        eye_g = jnp.eye(g, dtype=blocks.dtype)
        return (blocks[:, :, None, :] * eye_g[:, None, :, None]).reshape(g * blocks.shape[1], g * blocks.shape[2])

    return dict(
        toep=toep_q.astype(BF16),
        in_re=in_state(p_re).astype(BF16), in_im=in_state(p_im).astype(BF16),
        out_re=state_out(q_re).astype(BF16), out_im=state_out(q_im).astype(BF16),
        scan_re=scan_rows(scan_re), scan_im=scan_rows(scan_im),
        bb_re=dense_diag(bb_re.transpose(0, 2, 1)).astype(BF16), bb_im=dense_diag(bb_im.transpose(0, 2, 1)).astype(BF16),
        c_re=dense_diag(cr_t).astype(BF16), c_im=dense_diag(ci_t).astype(BF16),
        a_re=ab_re.reshape(1, g * p), a_im=ab_im.reshape(1, g * p), d_row=d.reshape(1, g * hch))


def _rec_levels(chunk):
    return [1 << p for p in range(chunk.bit_length() - 1)]


def _rec_tables(chunk):
    t = jnp.arange(chunk)[:, None]
    u = jnp.arange(chunk)[None, :]
    tabs = [u <= t]
    for half in _rec_levels(chunk):
        if half < 8:
            tabs.append(u <= (t // (2 * half)) * (2 * half) + half - 1)
    return jnp.concatenate(tabs, axis=0).astype(BF16)


def _linrec_kernel(*refs, mode, chunk, rows, heads, dk, dv, has_s0, seqs):
    it = iter(refs)
    q_ref, k_ref, v_ref, g_ref = next(it), next(it), next(it), next(it)
    if mode == "hgrn":
        loglb_ref, log1mlb_ref, onemlb_ref = next(it), next(it), next(it)
    else:
        r_ref, wgk_ref, bgk_ref = next(it), next(it), next(it)
    normw_ref = next(it)
    s0_ref = next(it) if has_s0 else None
    tstack_ref = next(it)
    o_ref, sfin_ref = next(it), next(it)
    st_ref = next(it)
    pad_refs = [next(it) for _ in range(5)] if rows != chunk else None

    c = pl.program_id(1)
    pack = LANES // dk
    ngrp = heads // pack
    gw = pack * dk

    @pl.when(c == 0)
    def _():
        for sg in range(seqs * ngrp):
            if has_s0:
                sq0, h0 = sg // ngrp, (sg % ngrp) * pack
                st_ref[sg] = jnp.concatenate([s0_ref[sq0, h0 + hh].T for hh in range(pack)], axis=1)
            else:
                st_ref[sg] = jnp.zeros((dv, gw), F32)

    def load(ref, slot, sq):
        if pad_refs is None:
            return ref[sq]
        buf = pad_refs[slot]
        buf[sq] = jnp.zeros(buf.shape[1:], F32)
        buf[sq, 0:rows, :] = ref[sq]
        return buf[sq]

    trow = lax.broadcasted_iota(jnp.int32, (chunk, chunk), 0)
    tcol = lax.broadcasted_iota(jnp.int32, (chunk, chunk), 1)
    dxor = trow ^ tcol
    rowi = lax.broadcasted_iota(jnp.int32, (chunk, 1), 0)
    valid = rowi < rows
    levels = _rec_levels(chunk)
    tstack = tstack_ref[...]

    lane_head = lax.broadcasted_iota(jnp.int32, (1, gw), 1) // dk

    def own_lanes(x, hh):
        return x if pack == 1 else jnp.where(lane_head == hh, x, 0.0)

    for sq, grp in [(a, b) for a in range(seqs) for b in range(ngrp)]:
        if grp == 0:
            q_all, k_all, v_all, g_all = load(q_ref, 0, sq), load(k_ref, 1, sq), load(v_ref, 2, sq), load(g_ref, 3, sq)
            if mode == "gla":
                gk_all = _dot(load(r_ref, 4, sq).astype(BF16), wgk_ref[...]) + bgk_ref[...]
        sg = sq * ngrp + grp
        ksl = slice(grp * gw, (grp + 1) * gw)
        if mode == "hgrn":
            q = _silu(q_all[:, ksl])
            z = k_all[:, ksl]
            la = loglb_ref[:, ksl]
            lc = log1mlb_ref[:, ksl] + _log_sigmoid(z)
            lf = jnp.maximum(la, lc) + jnp.log1p(jnp.exp(-jnp.abs(la - lc)))
            k = onemlb_ref[:, ksl] * jax.nn.sigmoid(-z)
        else:
            q = q_all[:, ksl] * (dk ** -0.5)
            k = k_all[:, ksl]
            lf = _log_sigmoid(gk_all[:, ksl]) / GLA_TAU
        if rows != chunk:
            lf = jnp.where(valid, lf, 0.0)
        hi = lf.astype(BF16)
        rest = lf - hi.astype(F32)
        mid = rest.astype(BF16)
        lo = (rest - mid.astype(F32)).astype(BF16)
        sums = _dot(tstack, hi) + _dot(tstack, mid) + _dot(tstack, lo)
        b = sums[0:chunk]
        st = st_ref[sg]
        stb = st.astype(BF16)
        qe = q * jnp.exp(b)
        b_last = b[chunk - 1:chunk, :]
        k_dec = k * jnp.exp(b_last - b)
        qk = q * k

        atts = [jnp.broadcast_to(jnp.sum(own_lanes(qk, hh), axis=-1, keepdims=True), (chunk, chunk))
                for hh in range(pack)]
        for li, half in enumerate(levels):
            if half < 8:
                beta = sums[(li + 1) * chunk:(li + 2) * chunk]
            else:
                nblk = chunk // (2 * half)
                b3 = b.reshape(nblk, 2 * half, gw)
                beta = jnp.broadcast_to(b3[:, half - 1:half, :], (nblk, 2 * half, gw)).reshape(chunk, gw)
            e = jnp.exp(-jnp.abs(b - beta))
            upper = (rowi & half) != 0
            qh = jnp.where(upper, q * e, 0.0)
            kh = jnp.where(upper, 0.0, k * e).astype(BF16)
            for hh in range(pack):
                atts[hh] = jnp.where(dxor >= half, _dot_nt(own_lanes(qh, hh).astype(BF16), kh), atts[hh])
        st_new = st * jnp.exp(b_last)
        for hh in range(pack):
            h = grp * pack + hh
            vsl = slice(h * dv, (h + 1) * dv)
            vb = v_all[:, vsl].astype(BF16)
            att = jnp.where(trow >= tcol, atts[hh], 0.0)
            o = _dot_nt(own_lanes(qe, hh).astype(BF16), stb) + _dot(att.astype(BF16), vb)
            o = o * lax.rsqrt(jnp.mean(o * o, axis=-1, keepdims=True) + NORM_EPS) * normw_ref[...]
            o = o * _silu(g_all[:, vsl])
            o_ref[sq, :, vsl] = o[0:rows, :]
            st_new = st_new + _dot_tn(vb, own_lanes(k_dec, hh).astype(BF16))
        st_ref[sg] = st_new

    @pl.when(c == pl.num_programs(1) - 1)
    def _():
        for sg in range(seqs * ngrp):
            for hh in range(pack):
                sfin_ref[sg // ngrp, (sg % ngrp) * pack + hh] = st_ref[sg][:, hh * dk:(hh + 1) * dk].T


def _lin_rec(mode, proj3, cols, extras, normw, s0, layer, chunk, seqs=1):
    bsz, t, _ = proj3.shape
    heads, dk, dv = (HGRN_HEADS, HGRN_DK, HGRN_DV) if mode == "hgrn" else (GLA_HEADS, GLA_DK, GLA_DV)
    rows = min(chunk, t)
    nchunk = max(t // chunk, 1)
    wk, wv = heads * dk, heads * dv
    assert bsz % seqs == 0

    def col_spec(width, col):
        return pl.BlockSpec((seqs, rows, width), lambda b, c: (b, c, col // width))

    in_specs = [col_spec(wk, cols[0]), col_spec(wk, cols[1]), col_spec(wv, cols[2]), col_spec(wv, cols[3])]
    args = [proj3, proj3, proj3, proj3]
    if mode == "hgrn":
        in_specs += [pl.BlockSpec((1, wk), lambda b, c: (0, 0))] * 3
        args += list(extras)
    else:
        in_specs += [col_spec(LANES, COL_RD), pl.BlockSpec((LANES, wk), lambda b, c: (0, 0)),
                     pl.BlockSpec((1, wk), lambda b, c: (0, 0))]
        args += [proj3] + list(extras)
    in_specs.append(pl.BlockSpec((1, dv), lambda b, c: (0, 0)))
    args.append(normw.reshape(1, dv))
    has_s0 = s0 is not None
    if has_s0:
        in_specs.append(pl.BlockSpec((None, seqs, heads, dk, dv), lambda b, c: (layer, b, 0, 0, 0)))
        args.append(s0)
    tables = _rec_tables(chunk)
    in_specs.append(pl.BlockSpec(tables.shape, lambda b, c: (0, 0)))
    args.append(tables)
    scratch = [pltpu.VMEM((seqs * heads * dk // LANES, dv, LANES), F32)]
    if rows != chunk:
        scratch += [pltpu.VMEM((seqs, chunk, wk), F32), pltpu.VMEM((seqs, chunk, wk), F32),
                    pltpu.VMEM((seqs, chunk, wv), F32), pltpu.VMEM((seqs, chunk, wv), F32),
                    pltpu.VMEM((seqs, chunk, LANES), F32)]
    return pl.pallas_call(
        functools.partial(_linrec_kernel, mode=mode, chunk=chunk, rows=rows, heads=heads, dk=dk, dv=dv,
                          has_s0=has_s0, seqs=seqs),
        grid=(bsz // seqs, nchunk),
        in_specs=in_specs,
        out_specs=[pl.BlockSpec((seqs, rows, wv), lambda b, c: (b, c, 0)),
                   pl.BlockSpec((seqs, heads, dk, dv), lambda b, c: (b, 0, 0, 0))],
        out_shape=[jax.ShapeDtypeStruct((bsz, t, wv), F32), jax.ShapeDtypeStruct((bsz, heads, dk, dv), F32)],
        scratch_shapes=scratch,
        compiler_params=_cparams("parallel", "arbitrary"),
        name=f"{mode}_rec",
    )(*args)


def _kmean_kernel(k_ref, o_ref):
    o_ref[...] = jnp.mean(k_ref[...], axis=0, keepdims=True)


def _block_key_means(proj3):
    bsz, t, _ = proj3.shape
    nb = t // MOBA_BLOCK
    out = pl.pallas_call(
        _kmean_kernel,
        grid=(bsz, nb),
        in_specs=[pl.BlockSpec((None, MOBA_BLOCK, MOBA_WIDTH), lambda b, n: (b, n, COL_KC // MOBA_WIDTH))],
        out_specs=pl.BlockSpec((None, None, 1, MOBA_WIDTH), lambda b, n: (b, n, 0, 0)),
        out_shape=jax.ShapeDtypeStruct((bsz, nb, 1, MOBA_WIDTH), F32),
        compiler_params=_cparams("parallel", "parallel"),
    )(proj3)
    return out.reshape(bsz, nb, MOBA_WIDTH)


def _select_top_blocks(gscore, n_valid, idx, axis):
    g = jnp.where(idx < n_valid, gscore, -jnp.inf)
    sel = jnp.zeros(gscore.shape, F32)
    width = gscore.shape[axis]
    for _ in range(MOBA_TOPK):
        m = jnp.max(g, axis=axis, keepdims=True)
        first = jnp.min(jnp.where(g == m, idx, width), axis=axis, keepdims=True)
        pick = idx == jnp.where(m > -jnp.inf, first, -1)
        sel = jnp.where(pick, 1.0, sel)
        g = jnp.where(pick, -jnp.inf, g)
    return sel


def _moba_prompt_kernel(q_ref, k_ref, v_ref, km_ref, o_ref, kb_ref, vt_ref, sel_ref):
    i = pl.program_id(2)
    blk = MOBA_BLOCK
    dh = MOBA_DH
    heads = q_ref.shape[-1] // dh
    hs = [slice(j * dh, (j + 1) * dh) for j in range(heads)]

    @pl.when(i == 0)
    def _():
        kb_ref[...] = k_ref[...].astype(BF16)

        def transpose_block(n, carry):
            r0 = pl.multiple_of(n * blk, blk)
            vt_ref[:, pl.ds(r0, blk)] = v_ref[pl.ds(r0, blk), :].T.astype(BF16)
            return carry

        lax.fori_loop(0, k_ref.shape[0] // blk, transpose_block, 0)

    qbs = []
    for j in range(heads):
        q = q_ref[:, hs[j]]
        gscore = _dot_nt(km_ref[:, hs[j]], q, precision=HIGHEST)
        blk_idx = lax.broadcasted_iota(jnp.int32, gscore.shape, 0)
        sel_ref[j] = _select_top_blocks(gscore, i, blk_idx, 0)
        qbs.append((q * (dh ** -0.5)).astype(BF16))

    def scores(j, n):
        return _dot_nt(kb_ref[pl.ds(pl.multiple_of(n * blk, blk), blk), hs[j]], qbs[j])

    def values(j, n, p):
        return _dot(vt_ref[hs[j], pl.ds(pl.multiple_of(n * blk, blk), blk)], p)

    def softmax_step(s, m, l, acc, pv):
        m_new = jnp.maximum(m, jnp.max(s, axis=0, keepdims=True))
        alpha = jnp.exp(m - m_new)
        p = jnp.exp(s - m_new)
        l = alpha * l + jnp.sum(p, axis=0, keepdims=True)
        return m_new, l, alpha * (acc + pv), p.astype(BF16)

    def past_block(n, carry):
        out = []
        for j in range(heads):
            m, l, acc, s, p_prev = carry[j]
            pv = values(j, jnp.maximum(n - 1, 0), p_prev)
            s_next = scores(j, n + 1)
            bias = jnp.where(sel_ref[j, pl.ds(n, 1), :] > 0.5, 0.0, NEG)
            m, l, acc, p = softmax_step(s + bias, m, l, acc, pv)
            out.append((m, l, acc, s_next, p))
        return tuple(out)

    init = tuple((jnp.full((1, blk), NEG, F32), jnp.zeros((1, blk), F32), jnp.zeros((dh, blk), F32),
                  scores(j, 0), jnp.zeros((blk, blk), BF16)) for j in range(heads))
    carry = lax.fori_loop(0, i, past_block, init)
    kpos = lax.broadcasted_iota(jnp.int32, (blk, blk), 0)
    qpos = lax.broadcasted_iota(jnp.int32, (blk, blk), 1)
    for j in range(heads):
        m, l, acc, s, p_prev = carry[j]
        pv = values(j, jnp.maximum(i - 1, 0), p_prev)
        m, l, acc, p = softmax_step(jnp.where(kpos <= qpos, s, NEG), m, l, acc, pv)
        acc = acc + values(j, i, p)
        o_ref[:, hs[j]] = (acc / l).T


def _moba_prompt(proj3, heads_per_step=2):
    bsz, t, _ = proj3.shape
    nb = t // MOBA_BLOCK
    km = _block_key_means(proj3)
    w = heads_per_step * MOBA_DH
    hq, hk, hv = COL_QC // w, COL_KC // w, COL_VC // w
    once = pl.Buffered(1)
    return pl.pallas_call(
        _moba_prompt_kernel,
        grid=(bsz, MOBA_HEADS // heads_per_step, nb),
        in_specs=[pl.BlockSpec((None, MOBA_BLOCK, w), lambda b, h, i: (b, i, hq + h)),
                  pl.BlockSpec((None, t, w), lambda b, h, i: (b, 0, hk + h), pipeline_mode=once),
                  pl.BlockSpec((None, t, w), lambda b, h, i: (b, 0, hv + h), pipeline_mode=once),
                  pl.BlockSpec((None, nb, w), lambda b, h, i: (b, 0, h))],
        out_specs=pl.BlockSpec((None, MOBA_BLOCK, w), lambda b, h, i: (b, i, h)),
        out_shape=jax.ShapeDtypeStruct((bsz, t, MOBA_WIDTH), F32),
        scratch_shapes=[pltpu.VMEM((t, w), BF16), pltpu.VMEM((w, t), BF16),
                        pltpu.VMEM((heads_per_step, nb, MOBA_BLOCK), F32)],
        compiler_params=_cparams("parallel", "parallel", "arbitrary"),
        name="moba_prompt",
    )(proj3, proj3, proj3, km)


def _moba_sample_kernel(pt_ref, q_ref, kn_ref, vn_ref, seg_ref, *refs, n_pages, seqs):
    npg = seqs * n_pages
    o_ref = refs[2 * npg]
    s_ref, km_ref = refs[2 * npg + 1], refs[2 * npg + 2]
    nh = MOBA_HEADS
    prow = PAGE_SIZE * nh
    pages_per_block = MOBA_BLOCK // PAGE_SIZE
    n_blocks = n_pages // pages_per_block
    sub = 8
    nq = q_ref.shape[1]
    q_head = lax.broadcasted_iota(jnp.int32, (nq, 1), 0) % nh
    q_tok = lax.broadcasted_iota(jnp.int32, (nq, 1), 0) // nh
    own_head = lax.broadcasted_iota(jnp.int32, (nq, prow), 1) % nh == q_head
    for sq in range(seqs):
        k_pages = refs[sq * n_pages:(sq + 1) * n_pages]
        v_pages = refs[npg + sq * n_pages:npg + (sq + 1) * n_pages]
        q = q_ref[sq]
        for n in range(n_blocks):
            tot = jnp.sum(k_pages[n * pages_per_block][...].reshape(prow // sub, sub, MOBA_DH), axis=0)
            for j in range(1, pages_per_block):
                tot = tot + jnp.sum(k_pages[n * pages_per_block + j][...].reshape(prow // sub, sub, MOBA_DH), axis=0)
            km_ref[sq, n * sub:(n + 1) * sub, :] = tot
        g_all = _dot_nt(q, km_ref[sq], precision=HIGHEST)
        g_lane = lax.broadcasted_iota(jnp.int32, g_all.shape, 1)
        g_own = jnp.where(g_lane % nh == q_head, g_all, 0.0)
        gscore = _dot(g_own, seg_ref[...], precision=HIGHEST) * (1.0 / MOBA_BLOCK)
        blk_idx = lax.broadcasted_iota(jnp.int32, gscore.shape, 1)
        sel = _select_top_blocks(gscore, n_blocks, blk_idx, 1)
        qb = (q * (MOBA_DH ** -0.5)).astype(BF16)
        k_all = jnp.concatenate([k_pages[j][...].astype(BF16) for j in range(n_pages)], axis=0)
        s_all = _dot_nt(qb, k_all)
        for j in range(n_pages):
            nblk = j // pages_per_block
            keep = jnp.where(own_head, sel[:, nblk:nblk + 1], 0.0) > 0.5
            s_ref[sq, :, j * prow:(j + 1) * prow] = jnp.where(keep, s_all[:, j * prow:(j + 1) * prow], NEG)
        s_new = _dot_nt(qb, kn_ref[sq].astype(BF16))
        coln = lax.broadcasted_iota(jnp.int32, s_new.shape, 1)
        ok_new = jnp.where(coln % nh == q_head, coln // nh, nq) <= q_tok
        s_new = jnp.where(ok_new, s_new, NEG)
        s_past = s_ref[sq]
        m = jnp.maximum(jnp.max(s_past, axis=-1, keepdims=True), jnp.max(s_new, axis=-1, keepdims=True))
        p_past = jnp.exp(s_past - m)
        p_new = jnp.exp(s_new - m)
        l = jnp.sum(p_past, axis=-1, keepdims=True) + jnp.sum(p_new, axis=-1, keepdims=True)
        v_all = jnp.concatenate([v_pages[j][...].astype(BF16) for j in range(n_pages)], axis=0)
        o = _dot(p_new.astype(BF16), vn_ref[sq].astype(BF16)) + _dot(p_past.astype(BF16), v_all)
        o_ref[sq] = o / l


def _moba_sample(q, k_new, v_new, cache_k, cache_v, page_table, layer, seqs):
    bsz, nq, dh = q.shape
    depth, n_pool = cache_k.shape[:2]
    n_pages = page_table.shape[1]
    prow = PAGE_SIZE * MOBA_HEADS
    ck = cache_k.reshape(depth, n_pool, prow, dh)
    cv = cache_v.reshape(depth, n_pool, prow, dh)
    n_blocks = n_pages * PAGE_SIZE // MOBA_BLOCK
    seg = (jnp.arange(n_blocks * 8)[:, None] // 8 == jnp.arange(n_blocks)[None, :]).astype(F32)
    assert bsz % seqs == 0

    def page_spec(sq, j):
        return pl.BlockSpec((None, None, prow, dh), lambda b, pt: (layer, pt[(b * seqs + sq) * n_pages + j], 0, 0))

    pages = [page_spec(sq, j) for sq in range(seqs) for j in range(n_pages)]
    grid_spec = pltpu.PrefetchScalarGridSpec(
        num_scalar_prefetch=1,
        grid=(bsz // seqs,),
        in_specs=([pl.BlockSpec((seqs, nq, dh), lambda b, pt: (b, 0, 0))] * 3
                  + [pl.BlockSpec(seg.shape, lambda b, pt: (0, 0))] + pages * 2),
        out_specs=pl.BlockSpec((seqs, nq, dh), lambda b, pt: (b, 0, 0)),
        scratch_shapes=[pltpu.VMEM((seqs, nq, n_pages * prow), F32), pltpu.VMEM((seqs, n_blocks * 8, dh), F32)],
    )
    return pl.pallas_call(
        functools.partial(_moba_sample_kernel, n_pages=n_pages, seqs=seqs),
        grid_spec=grid_spec,
        out_shape=jax.ShapeDtypeStruct((bsz, nq, dh), F32),
        compiler_params=_cparams("arbitrary"),
        name="moba_sample",
    )(page_table.reshape(-1), q, k_new, v_new, seg, *([ck] * (seqs * n_pages)), *([cv] * (seqs * n_pages)))


def _mix_kernel(ya_ref, ob_ref, oc_ref, od_ref, gl_ref, x_ref, g1_ref, wglu_ref, wb_ref, wc_ref, wd_ref, wo_ref,
                lng_ref, lnb_ref, o_ref):
    d = D_MODEL
    z = _dot(ya_ref[...].astype(BF16), wglu_ref[...])
    merged = jax.nn.sigmoid(gl_ref[:, 0:d]) * (z[:, 0:d] * jax.nn.sigmoid(z[:, d:2 * d]))
    merged += jax.nn.sigmoid(gl_ref[:, d:2 * d]) * _dot(ob_ref[...].astype(BF16), wb_ref[...])
    merged += jax.nn.sigmoid(gl_ref[:, 2 * d:3 * d]) * _dot(oc_ref[...].astype(BF16), wc_ref[...])
    merged += jax.nn.sigmoid(gl_ref[:, 3 * d:4 * d]) * _dot(od_ref[...].astype(BF16), wd_ref[...])
    mix = _dot(merged.astype(BF16), wo_ref[...])
    o_ref[...] = _layer_norm(DEEPNORM_ALPHA * x_ref[...] + g1_ref[...] * mix, lng_ref[...], lnb_ref[...])


def _mix(ya, ob, oc, od, proj, x, mod, rows_per_seq, wl, layer, tm):
    n, d = x.shape
    w512 = SSM_WIDTH
    tok = lambda width, col=0: pl.BlockSpec((tm, width), lambda i: (i, col))
    const = lambda r, c: pl.BlockSpec((None, r, c), lambda i: (layer, 0, 0))
    return pl.pallas_call(
        _mix_kernel,
        grid=(n // tm,),
        in_specs=[tok(w512), tok(w512), tok(w512), tok(w512), tok(N_BRANCH * d, COL_GATES), tok(d),
                  _mod_spec(mod, tm, rows_per_seq, 2),
                  const(w512, 2 * d), const(w512, d), const(w512, d), const(w512, d), const(d, d),
                  const(1, d), const(1, d)],
        out_specs=tok(d),
        out_shape=jax.ShapeDtypeStruct((n, d), F32),
        compiler_params=_cparams("parallel"),
        name="mix",
    )(ya, ob, oc, od, proj, x, mod, wl["w_glu"], wl["w_hgrn"], wl["w_moba"], wl["w_gla"], wl["w_out"],
      wl["ln1_g"], wl["ln1_b"])


def _first_max(vals, lane, width):
    m = jnp.max(vals, axis=-1, keepdims=True)
    first = jnp.min(jnp.where(vals == m, lane, width), axis=-1, keepdims=True)
    return m, first


def _route(h2, wr_ref, br_ref):
    logits = _dot(h2, wr_ref[...], precision=HIGHEST)
    lane = lax.broadcasted_iota(jnp.int32, logits.shape, 1)
    width = logits.shape[-1]
    scores = jax.nn.sigmoid(logits)
    biased = jnp.where(lane < N_EXPERTS, scores + br_ref[...], -jnp.inf)
    group = lane // EXPERTS_PER_GROUP
    best = jnp.zeros((logits.shape[0], 1), jnp.int32)
    best_score = None
    for gidx in range(N_EXPERT_GROUPS):
        vals = jnp.where(group == gidx, biased, -jnp.inf)
        m1, i1 = _first_max(vals, lane, width)
        m2 = jnp.max(jnp.where(lane == i1, -jnp.inf, vals), axis=-1, keepdims=True)
        gs = m1 + m2
        if best_score is None:
            best_score = gs
        else:
            better = gs > best_score
            best = jnp.where(better, gidx, best)
            best_score = jnp.where(better, gs, best_score)
    masked = jnp.where(group == best, biased, -jnp.inf)
    _, i1 = _first_max(masked, lane, width)
    _, i2 = _first_max(jnp.where(lane == i1, -jnp.inf, masked), lane, width)
    w1 = jnp.sum(jnp.where(lane == i1, scores, 0.0), axis=-1, keepdims=True)
    w2 = jnp.sum(jnp.where(lane == i2, scores, 0.0), axis=-1, keepdims=True)
    tot = w1 + w2
    return jnp.where(lane == i1, w1 / tot, 0.0) + jnp.where(lane == i2, w2 / tot, 0.0)


def _moe_kernel(x_ref, sh_ref, sc_ref, g2_ref, wr_ref, br_ref, wg_ref, wu_ref, wd_ref, lng_ref, lnb_ref, o_ref,
                h_ref, comb_ref, acc_ref, *, experts_per_step):
    e = pl.program_id(1)

    @pl.when(e == 0)
    def _():
        h2 = x_ref[...] * (1.0 + sc_ref[...]) + sh_ref[...]
        h_ref[...] = h2.astype(BF16)
        comb_ref[...] = _route(h2, wr_ref, br_ref)
        acc_ref[...] = jnp.zeros(acc_ref.shape, F32)

    hb = h_ref[...]
    lane = lax.broadcasted_iota(jnp.int32, comb_ref.shape, 1)
    for j in range(experts_per_step):
        eid = e * experts_per_step + j
        cw = jnp.sum(jnp.where(lane == eid, comb_ref[...], 0.0), axis=-1, keepdims=True)
        hid = _silu(_dot(hb, wg_ref[j])) * _dot(hb, wu_ref[j]) * cw
        acc_ref[...] += _dot(hid.astype(BF16), wd_ref[j])

    @pl.when(e == pl.num_programs(1) - 1)
    def _():
        o_ref[...] = _layer_norm(DEEPNORM_ALPHA * x_ref[...] + g2_ref[...] * acc_ref[...], lng_ref[...], lnb_ref[...])


def _moe(x, mod, rows_per_seq, wl, layer, w_router, b_router, tm, experts_per_step=4):
    n, d = x.shape
    nh = EXPERT_HIDDEN
    tok = pl.BlockSpec((tm, d), lambda i, e: (i, 0))
    const = lambda r, c: pl.BlockSpec((r, c), lambda i, e: (0, 0))
    per_layer = lambda r, c: pl.BlockSpec((None, r, c), lambda i, e: (layer, 0, 0))
    return pl.pallas_call(
        functools.partial(_moe_kernel, experts_per_step=experts_per_step),
        grid=(n // tm, N_EXPERTS // experts_per_step),
        in_specs=[tok, _mod_spec(mod, tm, rows_per_seq, 3), _mod_spec(mod, tm, rows_per_seq, 4),
                  _mod_spec(mod, tm, rows_per_seq, 5), const(d, LANES), const(1, LANES),
                  pl.BlockSpec((None, experts_per_step, d, nh), lambda i, e: (layer, e, 0, 0)),
                  pl.BlockSpec((None, experts_per_step, d, nh), lambda i, e: (layer, e, 0, 0)),
                  pl.BlockSpec((None, experts_per_step, nh, d), lambda i, e: (layer, e, 0, 0)),
                  per_layer(1, d), per_layer(1, d)],
        out_specs=tok,
        out_shape=jax.ShapeDtypeStruct((n, d), F32),
        scratch_shapes=[pltpu.VMEM((tm, d), BF16), pltpu.VMEM((tm, LANES), F32), pltpu.VMEM((tm, d), F32)],
        compiler_params=_cparams("parallel", "arbitrary"),
        name="moe",
    )(x, mod, mod, mod, w_router, b_router, wl["moe_gate"], wl["moe_up"], wl["moe_down"], wl["ln2_g"], wl["ln2_b"])


def _token_mixers_prompt(proj, k_rows, v_rows, bsz, t, wl, ws, layer):
    proj3 = proj.reshape(bsz, t, IN_COLS_PAD)
    ya, ssm_re, ssm_im = _ssm_prompt(proj3, ws["ssm_prompt"], layer)
    ob, hgrn_s = _lin_rec("hgrn", proj3, (COL_QB, COL_FB, COL_IB, COL_GB), wl["hgrn_extras"], wl["hgrn_norm"],
                          None, 0, REC_CHUNK_PROMPT)
    od, gla_s = _lin_rec("gla", proj3, (COL_QD, COL_KD, COL_VD, COL_GD), wl["gla_extras"], wl["gla_norm"],
                         None, 0, REC_CHUNK_PROMPT)
    oc = _moba_prompt(proj3)
    n = bsz * t
    k_new = k_rows.reshape(bsz, t, MOBA_HEADS, MOBA_DH)
    v_new = v_rows.reshape(bsz, t, MOBA_HEADS, MOBA_DH)
    return (ya.reshape(n, -1), ob.reshape(n, -1), oc.reshape(n, -1), od.reshape(n, -1),
            (k_new, v_new, ssm_re, ssm_im, hgrn_s, gla_s))


def _token_mixers_sample(proj, k_rows, v_rows, bsz, t, wl, layer, cache_k, cache_v, page_table, st_re, st_im,
                         st_hgrn, st_gla):
    proj3 = proj.reshape(bsz, t, IN_COLS_PAD)
    u_tm = proj3[:, :, COL_UA:COL_UA + SSM_WIDTH].transpose(1, 0, 2)
    ns = SSM_GROUPS * SSM_STATE
    ya_tm, xr, xi = _ssm_sample(u_tm, st_re[layer].reshape(bsz, ns), st_im[layer].reshape(bsz, ns), wl["ssm"])
    ya = ya_tm.transpose(1, 0, 2)
    seqs = math.gcd(bsz, REC_SAMPLE_SEQS)
    ob, hgrn_s = _lin_rec("hgrn", proj3, (COL_QB, COL_FB, COL_IB, COL_GB), wl["hgrn_extras"], wl["hgrn_norm"],
                          st_hgrn, layer, SUB_CHUNK, seqs)
    od, gla_s = _lin_rec("gla", proj3, (COL_QD, COL_KD, COL_VD, COL_GD), wl["gla_extras"], wl["gla_norm"],
                         st_gla, layer, SUB_CHUNK, seqs)
    nq = t * MOBA_HEADS
    q = proj3[:, :, COL_QC:COL_QC + MOBA_WIDTH].reshape(bsz, nq, MOBA_DH)
    oc = _moba_sample(q, k_rows.reshape(bsz, nq, MOBA_DH), v_rows.reshape(bsz, nq, MOBA_DH), cache_k, cache_v,
                      page_table, layer, math.gcd(bsz, MOBA_SAMPLE_SEQS))
    n = bsz * t
    return (ya.reshape(n, -1), ob.reshape(n, -1), oc.reshape(n, -1), od.reshape(n, -1),
            (k_rows.reshape(bsz, t, MOBA_HEADS, MOBA_DH), v_rows.reshape(bsz, t, MOBA_HEADS, MOBA_DH),
             xr.reshape(bsz, SSM_GROUPS, SSM_STATE), xi.reshape(bsz, SSM_GROUPS, SSM_STATE), hgrn_s, gla_s))


SSM_PROMPT_TABLES = ("toep", "in_re", "in_im", "out_re", "out_im", "scan_re", "scan_im")


def _stacked_weights(w_in_p, ssm_tabs, ssm_w_glu, hgrn_w_proj, moba_w_proj, gla_w_proj, w_out, ln1_g, ln1_b, ln2_g,
                     ln2_b, moe_w_gate, moe_w_up, moe_w_down):
    rows = lambda a: a.reshape(a.shape[0], 1, a.shape[1])
    ssm = {k: ssm_tabs[k] for k in SSM_PROMPT_TABLES}
    ssm["d_row"] = ssm_tabs["d_row"]
    return dict(
        w_in=w_in_p, ssm_prompt=ssm,
        w_glu=ssm_w_glu.astype(BF16), w_hgrn=hgrn_w_proj.astype(BF16), w_moba=moba_w_proj.astype(BF16),
        w_gla=gla_w_proj.astype(BF16), w_out=w_out.astype(BF16),
        ln1_g=rows(ln1_g), ln1_b=rows(ln1_b), ln2_g=rows(ln2_g), ln2_b=rows(ln2_b),
        moe_gate=moe_w_gate.astype(BF16), moe_up=moe_w_up.astype(BF16), moe_down=moe_w_down.astype(BF16))


def _layer_weights(l, ssm_tabs, lower_bounds, gla_w_gk2, gla_b_gk, hgrn_norm, gla_norm):
    lb = lower_bounds[l].reshape(1, -1)
    wgk = jnp.pad(gla_w_gk2[l], ((0, LANES - GLA_RANK), (0, 0))).astype(BF16)
    row = lambda a: a.reshape(1, -1)
    return dict(
        ssm={k: v[l] for k, v in ssm_tabs.items() if k not in SSM_PROMPT_TABLES},
        hgrn_extras=(jnp.log(lb), jnp.log1p(-lb), 1.0 - lb), hgrn_norm=hgrn_norm[l],
        gla_extras=(wgk, row(gla_b_gk[l])), gla_norm=gla_norm[l])


def kernel(x_prompt, x_sample, cache_k, cache_v, state_ssm_re, state_ssm_im, state_hgrn, state_gla, page_table,
           c_prompt, c_sample, ln_in_g, ln_in_b, w_ada, b_ada, w_in, ssm_lam_re, ssm_lam_im, ssm_log_step,
           ssm_b_re, ssm_b_im, ssm_c_re, ssm_c_im, ssm_d, ssm_w_glu, hgrn_lb, hgrn_norm, hgrn_w_proj, moba_w_proj,
           gla_w_gk2, gla_b_gk, gla_norm, gla_w_proj, w_out, ln1_g, ln1_b, ln2_g, ln2_b, w_router, b_router,
           moe_w_gate, moe_w_up, moe_w_down):
    bp, tp, d = x_prompt.shape
    bs, ts, _ = x_sample.shape
    depth = w_in.shape[0]
    n_pool = cache_k.shape[1]
    np_tok, ns_tok = bp * tp, bs * ts
    tm_p = min(1024, tp)
    tm_s = min(512, ns_tok)

    lb_cum = jnp.cumsum(jax.nn.softmax(hgrn_lb.astype(F32), axis=0), axis=0)
    lower_bounds = lb_cum - lb_cum[0:1]
    w_in_p = jnp.concatenate(
        [w_in[:, :, REF_GATE_START:IN_COLS_REF], w_in[:, :, :REF_RD_START], w_in[:, :, REF_RD_START:REF_GATE_START],
         jnp.zeros((depth, d, IN_COLS_PAD - IN_COLS_REF), w_in.dtype)], axis=-1).astype(BF16)
    n_scan = max(tp // SSM_CHUNK - 1, 0).bit_length()
    ssm_tabs = jax.vmap(functools.partial(_ssm_tables, n_scan=n_scan))(
        ssm_lam_re, ssm_lam_im, ssm_log_step, ssm_b_re, ssm_b_im, ssm_c_re, ssm_c_im, ssm_d)
    w_router_p = jnp.pad(w_router, ((0, 0), (0, LANES - N_EXPERTS)))
    b_router_p = jnp.pad(b_router, (0, LANES - N_EXPERTS)).reshape(1, LANES)

    nc = bs + bp
    nc_pad = -(-nc // 8) * 8
    c_all = jnp.pad(jnp.concatenate([c_sample, c_prompt], axis=0), ((0, nc_pad - nc), (0, 0)))
    mod_all = _ada_all(c_all, w_ada, b_ada)

    xp = _ln_rows(x_prompt.reshape(np_tok, d), ln_in_g, ln_in_b, tm_p)
    xs = _ln_rows(x_sample.reshape(ns_tok, d), ln_in_g, ln_in_b, tm_s)

    ws = _stacked_weights(w_in_p, ssm_tabs, ssm_w_glu, hgrn_w_proj, moba_w_proj, gla_w_proj, w_out, ln1_g, ln1_b,
                          ln2_g, ln2_b, moe_w_gate, moe_w_up, moe_w_down)
    outs_p, outs_s = [], []
    kv_p = kv_s = None
    for l in range(depth):
        wl = _layer_weights(l, ssm_tabs, lower_bounds, gla_w_gk2, gla_b_gk, hgrn_norm, gla_norm)
        mod_p = mod_all[l, bs:bs + bp].reshape(bp, 1, 6 * d)
        mod_s = jnp.repeat(mod_all[l, :bs], ts, axis=0)

        proj_p, *kv_p = _in_proj(xp, mod_p, tp, ws["w_in"], l, tm_p, kv_p)
        ya, ob, oc, od, st_p = _token_mixers_prompt(proj_p, kv_p[0][l], kv_p[1][l], bp, tp, wl, ws, l)
        xp = _mix(ya, ob, oc, od, proj_p, xp, mod_p, tp, ws, l, min(256, tp))
        xp = _moe(xp, mod_p, tp, ws, l, w_router_p, b_router_p, tm_p)
        outs_p.append(st_p)

        proj_s, *kv_s = _in_proj(xs, mod_s, ts, ws["w_in"], l, tm_s, kv_s)
        ya, ob, oc, od, st_s = _token_mixers_sample(proj_s, kv_s[0][l], kv_s[1][l], bs, ts, wl, l, cache_k, cache_v,
                                                    page_table, state_ssm_re, state_ssm_im, state_hgrn, state_gla)
        xs = _mix(ya, ob, oc, od, proj_s, xs, mod_s, ts, ws, l, min(256, ns_tok))
        xs = _moe(xs, mod_s, ts, ws, l, w_router_p, b_router_p, tm_s)
        outs_s.append(st_s)

    stack = lambda outs, idx: jnp.stack([o[idx] for o in outs])
    kv_out = lambda a, b, t: a.reshape(depth, b, t, MOBA_HEADS, MOBA_DH)
    return (xp.reshape(bp, tp, d), xs.reshape(bs, ts, d),
            kv_out(kv_p[0], bp, tp), kv_out(kv_p[1], bp, tp), kv_out(kv_s[0], bs, ts), kv_out(kv_s[1], bs, ts),
            stack(outs_p, 2), stack(outs_p, 3), stack(outs_s, 2), stack(outs_s, 3),
            stack(outs_p, 4), stack(outs_s, 4), stack(outs_p, 5), stack(outs_s, 5))
```
